```python
import math
import jax, jax.numpy as jnp
from jax import lax
import numpy as np

D_MODEL = 1024
BATCH = 4
SEQ = 4096
DEPTH = 2

PLE_DIM = 256
HEAD_DIM = 64
DIFF_HEADS = 4
DIFF_QK = 2 * HEAD_DIM
DIFF_V = 2 * HEAD_DIM
FOX_HEADS = 4
FOX_DIM = HEAD_DIM
RET_HEADS = 4
RET_DIM = HEAD_DIM
WA = DIFF_HEADS * DIFF_V
WB = FOX_HEADS * FOX_DIM
WC = RET_HEADS * RET_DIM
MIX_WIDTH = WA + WB + WC
IN_SIZES = (DIFF_HEADS * DIFF_QK, DIFF_HEADS * DIFF_QK, WA, WB, WB, WB, FOX_HEADS, WC, WC, WC, WC)
IN_COLS = sum(IN_SIZES)
Q_BLOCK = 128
RET_CHUNK = 128
RET_THETA = 10000.0
D_FF_DENSE = 2816
N_EXPERTS = 8
TOP_K = 2
D_FF_EXPERT = 3584
N_DENSE = (DEPTH + 1) // 2
N_MOE = DEPTH // 2
EPS = 1e-6
NEG_INF = -1e30

kernel_name = "hymba_diff_fox_retnet_moe_trunk"


def rmsnorm(x, g):
    xf = x.astype(jnp.float32)
    y = xf * lax.rsqrt(jnp.mean(xf * xf, axis=-1, keepdims=True) + EPS)
    return (y * g.astype(jnp.float32)).astype(x.dtype)


def head_groupnorm(x):
    xf = x.astype(jnp.float32)
    mu = jnp.mean(xf, axis=-1, keepdims=True)
    var = jnp.mean(jnp.square(xf - mu), axis=-1, keepdims=True)
    return ((xf - mu) * lax.rsqrt(var + EPS)).astype(x.dtype)


def split_heads(t, n_heads):
    B, S, _ = t.shape
    return t.reshape(B, S, n_heads, -1).transpose(0, 2, 1, 3)


def merge_heads(t):
    B, H, S, D = t.shape
    return t.transpose(0, 2, 1, 3).reshape(B, S, H * D)


def rotary(x):
    S, D = x.shape[2], x.shape[3]
    half = D // 2
    inv = RET_THETA ** (-jnp.arange(half, dtype=jnp.float32) / half)
    ang = jnp.arange(S, dtype=jnp.float32)[:, None] * inv[None, :]
    cos, sin = jnp.cos(ang).astype(x.dtype), jnp.sin(ang).astype(x.dtype)
    x1, x2 = x[..., :half], x[..., half:]
    return jnp.concatenate([x1 * cos - x2 * sin, x1 * sin + x2 * cos], axis=-1)


def causal_block_attention(q, k, v, scale, cum_logf=None):
    B, H, S, Dk = q.shape
    nb = S // Q_BLOCK
    q_blocks = q.reshape(B, H, nb, Q_BLOCK, Dk).transpose(2, 0, 1, 3, 4)
    k_pos = jnp.arange(S)
    xs = [jnp.arange(nb), q_blocks]
    if cum_logf is not None:
        xs.append(cum_logf.reshape(B, H, nb, Q_BLOCK).transpose(2, 0, 1, 3))

    def one_block(blk):
        i, q_i = blk[0], blk[1]
        s = jnp.einsum('bhqd,bhkd->bhqk', q_i, k).astype(jnp.float32) * scale
        if cum_logf is not None:
            s = s + blk[2][..., :, None] - cum_logf[:, :, None, :]
        q_pos = i * Q_BLOCK + jnp.arange(Q_BLOCK)
        s = jnp.where(k_pos[None, :] <= q_pos[:, None], s, NEG_INF)
        w = jax.nn.softmax(s, axis=-1).astype(v.dtype)
        return jnp.einsum('bhqk,bhkv->bhqv', w, v)

    o = lax.map(one_block, tuple(xs))
    return o.transpose(1, 2, 0, 3, 4).reshape(B, H, S, v.shape[-1])


def retention_chunkwise(q, k, v):
    B, H, S, Dk = q.shape
    Dv = v.shape[-1]
    nc = S // RET_CHUNK
    log_g = jnp.log1p(-jnp.exp2(-5.0 - jnp.arange(H, dtype=jnp.float32)))
    j = jnp.arange(RET_CHUNK, dtype=jnp.float32)
    diff = j[:, None] - j[None, :]
    inner = jnp.exp(jnp.where(diff[None] >= 0, diff[None] * log_g[:, None, None], -jnp.inf))
    q_dec = jnp.exp((j + 1.0)[None, :] * log_g[:, None])
    k_dec = jnp.exp((RET_CHUNK - 1.0 - j)[None, :] * log_g[:, None])
    c_dec = jnp.exp(RET_CHUNK * log_g)

    def to_chunks(t):
        return t.astype(jnp.float32).reshape(B, H, nc, RET_CHUNK, -1).transpose(2, 0, 1, 3, 4)

    def step(state, qkv):
        qc, kc, vc = qkv
        a = jnp.einsum('bhqd,bhkd->bhqk', qc, kc) * inner
        o = (jnp.einsum('bhqk,bhkv->bhqv', a, vc)
             + jnp.einsum('bhqd,bhdv->bhqv', qc * q_dec[:, :, None], state))
        state = state * c_dec[:, None, None] + jnp.einsum('bhkd,bhkv->bhdv', kc * k_dec[:, :, None], vc)
        return state, o

    state0 = jnp.zeros((B, H, Dk, Dv), jnp.float32)
    _, o = lax.scan(step, state0, (to_chunks(q), to_chunks(k), to_chunks(v)))
    return o.transpose(1, 2, 0, 3, 4).reshape(B, H, S, Dv).astype(v.dtype)


def hybrid_mixer(xn, w_in, b_forget, lq1, lk1, lq2, lk2, subln_g, ret_gn_g, w_out, layer_idx):
    B, S, _ = xn.shape
    proj = xn @ w_in
    qa, ka, va, qb, kb, vb, fb, qc, kc, vc, gc = jnp.split(proj, list(np.cumsum(IN_SIZES)[:-1]), axis=-1)

    qa = split_heads(qa, 2 * DIFF_HEADS)
    ka = split_heads(ka, 2 * DIFF_HEADS)
    va = jnp.repeat(split_heads(va, DIFF_HEADS), 2, axis=1)
    o = causal_block_attention(qa, ka, va, HEAD_DIM ** -0.5)
    o = o.reshape(B, DIFF_HEADS, 2, S, DIFF_V)
    lam_init = 0.8 - 0.6 * math.exp(-0.3 * layer_idx)
    lam = (jnp.exp(jnp.sum(lq1.astype(jnp.float32) * lk1.astype(jnp.float32)))
           - jnp.exp(jnp.sum(lq2.astype(jnp.float32) * lk2.astype(jnp.float32))) + lam_init)
    o_a = o[:, :, 0] - lam.astype(o.dtype) * o[:, :, 1]
    o_a = merge_heads(rmsnorm(o_a, subln_g) * (1.0 - lam_init))

    log_f = jax.nn.log_sigmoid((fb + b_forget).astype(jnp.float32))
    cum = jnp.cumsum(log_f, axis=1).transpose(0, 2, 1)
    o_b = causal_block_attention(split_heads(qb, FOX_HEADS), split_heads(kb, FOX_HEADS),
                                 split_heads(vb, FOX_HEADS), FOX_DIM ** -0.5, cum)
    o_b = merge_heads(o_b)

    qr = rotary(split_heads(qc, RET_HEADS))
    kr = rotary(split_heads(kc, RET_HEADS)) * (RET_DIM ** -0.5)
    o_c = retention_chunkwise(qr, kr, split_heads(vc, RET_HEADS))
    o_c = merge_heads(head_groupnorm(o_c)) * ret_gn_g
    o_c = jax.nn.silu(gc) * o_c

    return jnp.concatenate([o_a, o_b, o_c], axis=-1) @ w_out


def swiglu(x, wg, wu, wd):
    return (jax.nn.silu(x @ wg) * (x @ wu)) @ wd


def moe_swiglu(xn, router, wg, wu, wd):
    B, S, D = xn.shape
    t = xn.reshape(B * S, D)
    logits = (t @ router).astype(jnp.float32)
    top_val, top_idx = lax.top_k(logits, TOP_K)
    gates = jax.nn.softmax(top_val, axis=-1)
    combine = jnp.sum(jax.nn.one_hot(top_idx, N_EXPERTS, dtype=jnp.float32) * gates[..., None], axis=1)
    out = jnp.zeros_like(t)
    for e in range(N_EXPERTS):
        out = out + combine[:, e:e + 1].astype(t.dtype) * swiglu(t, wg[e], wu[e], wd[e])
    return out.reshape(B, S, D)


def setup_inputs(seed: int = 0) -> dict:
    key = jax.random.key(seed)
    ks = jax.random.split(key, 26)
    f32 = jnp.float32

    def nrm(k, shape, scale):
        return jax.random.normal(k, shape, f32) * scale

    def gain(k, shape):
        return 1.0 + 0.05 * jax.random.normal(k, shape, f32)

    return {
        "x": nrm(ks[0], (BATCH, SEQ, D_MODEL), 1.0),
        "p": nrm(ks[1], (DEPTH, BATCH, SEQ, PLE_DIM), 1.0),
        "norm_mix": gain(ks[2], (DEPTH, D_MODEL)),
        "w_in": nrm(ks[3], (DEPTH, D_MODEL, IN_COLS), D_MODEL ** -0.5),
        "b_forget": 3.0 + 0.5 * jax.random.normal(ks[4], (DEPTH, FOX_HEADS), f32),
        "lambda_q1": nrm(ks[5], (DEPTH, HEAD_DIM), 0.1),
        "lambda_k1": nrm(ks[6], (DEPTH, HEAD_DIM), 0.1),
        "lambda_q2": nrm(ks[7], (DEPTH, HEAD_DIM), 0.1),
        "lambda_k2": nrm(ks[8], (DEPTH, HEAD_DIM), 0.1),
        "diff_subln": gain(ks[9], (DEPTH, DIFF_V)),
        "ret_gn": gain(ks[10], (DEPTH, WC)),
        "w_out": nrm(ks[11], (DEPTH, MIX_WIDTH, D_MODEL), MIX_WIDTH ** -0.5),
        "norm_ffn": gain(ks[12], (DEPTH, D_MODEL)),
        "dense_w_gate": nrm(ks[13], (N_DENSE, D_MODEL, D_FF_DENSE), D_MODEL ** -0.5),
        "dense_w_up": nrm(ks[14], (N_DENSE, D_MODEL, D_FF_DENSE), D_MODEL ** -0.5),
        "dense_w_down": nrm(ks[15], (N_DENSE, D_FF_DENSE, D_MODEL), D_FF_DENSE ** -0.5),
        "router": nrm(ks[16], (N_MOE, D_MODEL, N_EXPERTS), D_MODEL ** -0.5),
        "moe_w_gate": nrm(ks[17], (N_MOE, N_EXPERTS, D_MODEL, D_FF_EXPERT), D_MODEL ** -0.5),
        "moe_w_up": nrm(ks[18], (N_MOE, N_EXPERTS, D_MODEL, D_FF_EXPERT), D_MODEL ** -0.5),
        "moe_w_down": nrm(ks[19], (N_MOE, N_EXPERTS, D_FF_EXPERT, D_MODEL), D_FF_EXPERT ** -0.5),
        "ple_norm": gain(ks[20], (DEPTH, D_MODEL)),
        "ple_gate": nrm(ks[21], (DEPTH, D_MODEL, D_MODEL), D_MODEL ** -0.5),
        "ple_proj": nrm(ks[22], (DEPTH, PLE_DIM, D_MODEL), PLE_DIM ** -0.5),
        "final_norm": gain(ks[23], (D_MODEL,)),
    }


def reference(x, p, norm_mix, w_in, b_forget, lambda_q1, lambda_k1, lambda_q2, lambda_k2,
              diff_subln, ret_gn, w_out, norm_ffn, dense_w_gate, dense_w_up, dense_w_down,
              router, moe_w_gate, moe_w_up, moe_w_down, ple_norm, ple_gate, ple_proj, final_norm):
    h = x
    for i in range(DEPTH):
        xn = rmsnorm(h, norm_mix[i])
        h = h + hybrid_mixer(xn, w_in[i], b_forget[i], lambda_q1[i], lambda_k1[i], lambda_q2[i],
                             lambda_k2[i], diff_subln[i], ret_gn[i], w_out[i], i)
        xn = rmsnorm(h, norm_ffn[i])
        j = i // 2
        if i % 2 == 0:
            h = h + swiglu(xn, dense_w_gate[j], dense_w_up[j], dense_w_down[j])
        else:
            h = h + moe_swiglu(xn, router[j], moe_w_gate[j], moe_w_up[j], moe_w_down[j])
        gate = jax.nn.sigmoid(rmsnorm(h, ple_norm[i]) @ ple_gate[i])
        h = h + gate * (p[i] @ ple_proj[i])
    return rmsnorm(h, final_norm)
```

```python
import functools
import math
from typing import NamedTuple

import numpy as np
import jax
import jax.numpy as jnp
from jax import lax
from jax.experimental import pallas as pl
from jax.experimental.pallas import tpu as pltpu

F32 = jnp.float32
BF16 = jnp.bfloat16

LANES = 128
VMEM_LIMIT_BYTES = 56 * 1024 * 1024

HEAD_DIM = 64
DIFF_HEADS = 4
FOX_HEADS = 4
RET_HEADS = 4
RET_THETA = 10000.0
N_EXPERTS = 8
EPS = 1e-6
NEG_INF = -1e30

QA, KA, VA = 0, 4, 8
QB, KB, VB = 12, 14, 16
QC, KC, VC, GC = 18, 20, 22, 24
N_PROJ_BLOCKS = 26


def _params(semantics):
    return pltpu.CompilerParams(dimension_semantics=semantics,
                                vmem_limit_bytes=VMEM_LIMIT_BYTES)


def _resident(shape):
    nd = len(shape)
    return pl.BlockSpec(shape, lambda *_: (0,) * nd, pipeline_mode=pl.Buffered(1))


def _rms(x, g):
    return x * lax.rsqrt(jnp.mean(x * x, axis=-1, keepdims=True) + EPS) * g


def _sigmoid(x):
    return 1.0 / (1.0 + jnp.exp(-x))


def _dot(a, b):
    return jnp.dot(a, b, preferred_element_type=F32)


def _dot_nt(a, b):
    return lax.dot_general(a, b, (((1,), (1,)), ((), ())), preferred_element_type=F32)


def _lane_masks(shape):
    lane = lax.broadcasted_iota(jnp.int32, shape, len(shape) - 1)
    return lane < HEAD_DIM


def _norm_proj_kernel(h_ref, g_ref, w_ref, wf_ref, out_ref, f_ref, *, col_chunk):
    xn = _rms(h_ref[...], g_ref[...]).astype(BF16)
    ncol = w_ref.shape[1]
    for c in range(0, ncol, col_chunk):
        w = min(col_chunk, ncol - c)
        out_ref[:, c:c + w] = _dot(xn, w_ref[:, c:c + w]).astype(out_ref.dtype)
    f_ref[...] = _dot(xn, wf_ref[...])


def norm_proj(h, g, w_main, w_f, *, tm):
    n, d = h.shape
    ncol = w_main.shape[1]
    return pl.pallas_call(
        functools.partial(_norm_proj_kernel, col_chunk=512),
        grid=(n // tm,),
        in_specs=[pl.BlockSpec((tm, d), lambda i: (i, 0)),
                  _resident((1, d)), _resident((d, ncol)), _resident((d, LANES))],
        out_specs=[pl.BlockSpec((tm, ncol), lambda i: (i, 0)),
                   pl.BlockSpec((tm, LANES), lambda i: (i, 0))],
        out_shape=[jax.ShapeDtypeStruct((n, ncol), BF16),
                   jax.ShapeDtypeStruct((n, LANES), F32)],
        compiler_params=_params(("arbitrary",)),
        name="norm_proj",
    )(h, g, w_main, w_f)


def _fox_prep_kernel(f_ref, b_ref, col_ref, row_ref, carry_ref):
    @pl.when(pl.program_id(1) == 0)
    def _():
        carry_ref[...] = jnp.zeros_like(carry_ref)

    x = f_ref[...] + b_ref[...]
    log_f = jnp.minimum(x, 0.0) - jnp.log1p(jnp.exp(-jnp.abs(x)))
    blk = x.shape[0]
    r = lax.broadcasted_iota(jnp.int32, (blk, blk), 0)
    c = lax.broadcasted_iota(jnp.int32, (blk, blk), 1)
    tri = (c <= r).astype(F32)
    cum = jnp.dot(tri, log_f, precision=lax.Precision.HIGHEST,
                  preferred_element_type=F32) + carry_ref[...]
    col_ref[...] = cum
    row_ref[...] = cum.T[:8, :]
    carry_ref[...] = cum[blk - 1:blk, :]


def fox_prep(f, b_pad, *, blk):
    bsz, s, _ = f.shape
    return pl.pallas_call(
        _fox_prep_kernel,
        grid=(bsz, s // blk),
        in_specs=[pl.BlockSpec((None, blk, LANES), lambda b, j: (b, j, 0)),
                  pl.BlockSpec((1, LANES), lambda b, j: (0, 0))],
        out_specs=[pl.BlockSpec((None, blk, LANES), lambda b, j: (b, j, 0)),
                   pl.BlockSpec((None, 8, blk), lambda b, j: (b, 0, j))],
        out_shape=[jax.ShapeDtypeStruct((bsz, s, LANES), F32),
                   jax.ShapeDtypeStruct((bsz, 8, s), F32)],
        scratch_shapes=[pltpu.VMEM((1, LANES), F32)],
        compiler_params=_params(("arbitrary", "arbitrary")),
        name="fox_prep",
    )(f, b_pad)


def _attn_kernel(*refs, mode, tq, out_scale):
    if mode == "diff":
        lam_ref, q_ref, k_ref, v_ref, gain_ref, o_ref, m_scr, l_scr, acc_scr = refs
    else:
        q_ref, k_ref, v_ref, ccol_ref, crow_ref, o_ref, m_scr, l_scr, acc_scr = refs
    qi = pl.program_id(2)
    pair = pl.program_id(1)

    q = q_ref[...]
    lo = _lane_masks(q.shape)
    zero = jnp.zeros_like(q)
    q_maps = (jnp.where(lo, q, zero), jnp.where(lo, zero, q))

    m_scr[...] = jnp.full_like(m_scr, NEG_INF)
    l_scr[...] = jnp.zeros_like(l_scr)
    acc_scr[...] = jnp.zeros_like(acc_scr)

    if mode == "fox":
        ccol = ccol_ref[...]
        lane = lax.broadcasted_iota(jnp.int32, ccol.shape, 1)
        ct = tuple(jnp.sum(jnp.where(lane == 2 * pair + i, ccol, 0.0), axis=-1, keepdims=True)
                   for i in range(2))

    def chunk(start, masked):
        k = k_ref[pl.ds(start, tq), :]
        v = v_ref[pl.ds(start, tq), :]
        for i in range(2):
            s = _dot_nt(q_maps[i], k)
            if mode == "fox":
                cs = crow_ref[pl.ds(2 * pair + i, 1), pl.ds(start, tq)]
                s = s + (ct[i] - cs)
            if masked:
                row = lax.broadcasted_iota(jnp.int32, s.shape, 0)
                col = lax.broadcasted_iota(jnp.int32, s.shape, 1)
                s = jnp.where(col <= row, s, NEG_INF)
            m_old = m_scr[i]
            m_new = jnp.maximum(m_old, jnp.max(s, axis=-1, keepdims=True))
            alpha = jnp.exp(m_old - m_new)
            p = jnp.exp(s - m_new)
            l_scr[i] = alpha * l_scr[i] + jnp.sum(p, axis=-1, keepdims=True)
            acc_scr[i] = alpha * acc_scr[i] + _dot(p.astype(BF16), v)
            m_scr[i] = m_new

    def body(j, carry):
        chunk(pl.multiple_of(j * tq, tq), False)
        return carry

    lax.fori_loop(0, qi, body, 0)
    chunk(pl.multiple_of(qi * tq, tq), True)

    o1 = acc_scr[0] * (1.0 / l_scr[0])
    o2 = acc_scr[1] * (1.0 / l_scr[1])
    if mode == "diff":
        o = o1 - lam_ref[0] * o2
        o = _rms(o, gain_ref[...]) * out_scale
    else:
        o = jnp.where(_lane_masks(o1.shape), o1, o2)
    o_ref[...] = o.astype(o_ref.dtype)


def _attn_scratch(tq):
    return [pltpu.VMEM((2, tq, 1), F32), pltpu.VMEM((2, tq, 1), F32),
            pltpu.VMEM((2, tq, LANES), F32)]


def diff_attention(proj, lam, gain, *, tq, out_scale):
    bsz, s, _ = proj.shape
    kern = functools.partial(_attn_kernel, mode="diff", tq=tq, out_scale=out_scale)
    return pl.pallas_call(
        kern,
        grid=(bsz, DIFF_HEADS, s // tq),
        in_specs=[pl.BlockSpec(memory_space=pltpu.SMEM),
                  pl.BlockSpec((None, tq, LANES), lambda b, h, i: (b, i, QA + h)),
                  pl.BlockSpec((None, s, LANES), lambda b, h, i: (b, 0, KA + h)),
                  pl.BlockSpec((None, s, LANES), lambda b, h, i: (b, 0, VA + h)),
                  pl.BlockSpec((1, LANES), lambda b, h, i: (0, 0))],
        out_specs=pl.BlockSpec((None, tq, LANES), lambda b, h, i: (b, i, h)),
        out_shape=jax.ShapeDtypeStruct((bsz, s, DIFF_HEADS * LANES), BF16),
        scratch_shapes=_attn_scratch(tq),
        compiler_params=_params(("arbitrary", "arbitrary", "arbitrary")),
        name="diff_attention",
    )(lam, proj, proj, proj, gain)


def fox_attention(proj, cum_col, cum_row, *, tq):
    bsz, s, _ = proj.shape
    npair = FOX_HEADS // 2
    kern = functools.partial(_attn_kernel, mode="fox", tq=tq, out_scale=None)
    return pl.pallas_call(
        kern,
        grid=(bsz, npair, s // tq),
        in_specs=[pl.BlockSpec((None, tq, LANES), lambda b, h, i: (b, i, QB + h)),
                  pl.BlockSpec((None, s, LANES), lambda b, h, i: (b, 0, KB + h)),
                  pl.BlockSpec((None, s, LANES), lambda b, h, i: (b, 0, VB + h)),
                  pl.BlockSpec((None, tq, LANES), lambda b, h, i: (b, i, 0)),
                  pl.BlockSpec((None, 8, s), lambda b, h, i: (b, 0, 0))],
        out_specs=pl.BlockSpec((None, tq, LANES), lambda b, h, i: (b, i, h)),
        out_shape=jax.ShapeDtypeStruct((bsz, s, npair * LANES), BF16),
        scratch_shapes=_attn_scratch(tq),
        compiler_params=_params(("arbitrary", "arbitrary", "arbitrary")),
        name="fox_attention",
    )(proj, proj, proj, cum_col, cum_row)


def _retention_tables(s, chunk):
    half = HEAD_DIM // 2
    inv = RET_THETA ** (-np.arange(half, dtype=np.float64) / half)
    ang = np.arange(s, dtype=np.float64)[:, None] * inv[None, :]
    cos = np.tile(np.cos(ang), (1, 4))
    sin = np.tile(np.concatenate([-np.sin(ang), np.sin(ang)], axis=1), (1, 2))
    log_g = np.log1p(-np.exp2(-5.0 - np.arange(RET_HEADS, dtype=np.float64)))
    j = np.arange(chunk, dtype=np.float64)
    diff = j[:, None] - j[None, :]
    inner = np.where(diff[None] >= 0, np.exp(np.where(diff[None] >= 0, diff[None], 0.0)
                                             * log_g[:, None, None]), 0.0)
    q_dec = np.exp((j + 1.0)[None, :] * log_g[:, None])
    k_dec = np.exp((chunk - 1.0 - j)[None, :] * log_g[:, None])
    c_dec = np.exp(chunk * log_g)
    npair = RET_HEADS // 2

    def lanes(t):
        t = t.reshape(npair, 2, chunk)
        return np.repeat(t.transpose(0, 2, 1), HEAD_DIM, axis=2)

    blk = np.kron(np.eye(2), np.ones((HEAD_DIM, HEAD_DIM)))
    c_mat = np.stack([np.kron(np.diag(c_dec[2 * p:2 * p + 2]), np.ones((HEAD_DIM, HEAD_DIM)))
                      for p in range(npair)])
    f = lambda a: jnp.asarray(a, F32)
    return dict(cos=f(cos), sin=f(sin), inner=f(inner.reshape(npair, 2, chunk, chunk)),
                q_dec=f(lanes(q_dec)), k_dec=f(lanes(k_dec)), c_mat=f(c_mat), blk=f(blk))


def _retention_kernel(q_ref, k_ref, v_ref, g_ref, cos_ref, sin_ref, inner_ref, qdec_ref,
                      kdec_ref, cmat_ref, blk_ref, gain_ref, o_ref, state_ref):
    @pl.when(pl.program_id(2) == 0)
    def _():
        state_ref[...] = jnp.zeros_like(state_ref)

    cos = cos_ref[...]
    sin = sin_ref[...]
    lane = lax.broadcasted_iota(jnp.int32, cos.shape, 1)
    first_half = (lane % HEAD_DIM) < (HEAD_DIM // 2)

    def rope(x):
        x = x.astype(F32)
        swapped = jnp.where(first_half, pltpu.roll(x, LANES - HEAD_DIM // 2, 1),
                            pltpu.roll(x, HEAD_DIM // 2, 1))
        return x * cos + swapped * sin

    qr = rope(q_ref[...])
    kr = rope(k_ref[...])
    v = v_ref[...]
    lo = _lane_masks(qr.shape)
    qb = qr.astype(BF16)
    kb = kr.astype(BF16)
    zero = jnp.zeros_like(qb)
    a1 = (_dot_nt(jnp.where(lo, qb, zero), kb) * inner_ref[0]).astype(BF16)
    a2 = (_dot_nt(jnp.where(lo, zero, qb), kb) * inner_ref[1]).astype(BF16)
    state = state_ref[...]
    o = jnp.where(lo, _dot(a1, v), _dot(a2, v))
    o = o + _dot((qr * qdec_ref[...]).astype(BF16), state.astype(BF16))
    kd = (kr * kdec_ref[...]).astype(BF16)
    state_ref[...] = state * cmat_ref[...] + _dot(kd.T, v) * blk_ref[...]

    inv_n = 1.0 / HEAD_DIM
    sum_lo = jnp.sum(jnp.where(lo, o, 0.0), axis=-1, keepdims=True)
    sum_hi = jnp.sum(jnp.where(lo, 0.0, o), axis=-1, keepdims=True)
    d = o - jnp.where(lo, sum_lo, sum_hi) * inv_n
    d2 = d * d
    var_lo = jnp.sum(jnp.where(lo, d2, 0.0), axis=-1, keepdims=True)
    var_hi = jnp.sum(jnp.where(lo, 0.0, d2), axis=-1, keepdims=True)
    y = d * lax.rsqrt(jnp.where(lo, var_lo, var_hi) * inv_n + EPS)
    g = g_ref[...].astype(F32)
    o_ref[...] = (g * _sigmoid(g) * (y * gain_ref[...])).astype(o_ref.dtype)


def retention(proj, gain, *, chunk):
    bsz, s, _ = proj.shape
    npair = RET_HEADS // 2
    t = _retention_tables(s, chunk)

    def act(off):
        return pl.BlockSpec((None, chunk, LANES), lambda b, p, c: (b, c, off + p))

    pos = pl.BlockSpec((chunk, LANES), lambda b, p, c: (c, 0))
    per_pair3 = lambda shape: pl.BlockSpec((None,) + shape, lambda b, p, c: (p,) + (0,) * len(shape))
    return pl.pallas_call(
        _retention_kernel,
        grid=(bsz, npair, s // chunk),
        in_specs=[act(QC), act(KC), act(VC), act(GC), pos, pos,
                  per_pair3((2, chunk, chunk)), per_pair3((chunk, LANES)),
                  per_pair3((chunk, LANES)), per_pair3((LANES, LANES)),
                  pl.BlockSpec((LANES, LANES), lambda b, p, c: (0, 0)),
                  pl.BlockSpec((1, LANES), lambda b, p, c: (0, p))],
        out_specs=pl.BlockSpec((None, chunk, LANES), lambda b, p, c: (b, c, p)),
        out_shape=jax.ShapeDtypeStruct((bsz, s, npair * LANES), BF16),
        scratch_shapes=[pltpu.VMEM((LANES, LANES), F32)],
        compiler_params=_params(("arbitrary", "arbitrary", "arbitrary")),
        name="retention",
    )(proj, proj, proj, proj, t["cos"], t["sin"], t["inner"], t["q_dec"], t["k_dec"],
      t["c_mat"], t["blk"], gain)


def _mix_residual(h_ref, oa_ref, ob_ref, oc_ref, wo_ref):
    wa = oa_ref.shape[1]
    wb = ob_ref.shape[1]
    return (h_ref[...] + _dot(oa_ref[...], wo_ref[0:wa, :])
            + _dot(ob_ref[...], wo_ref[wa:wa + wb, :])
            + _dot(oc_ref[...], wo_ref[wa + wb:, :]))


def _swiglu_accumulate(xn, wg_ref, wu_ref, wd_ref, acc_ref, chunk):
    def body(i, carry):
        c = pl.multiple_of(i * chunk, chunk)
        g = _dot(xn, wg_ref[:, pl.ds(c, chunk)])
        u = _dot(xn, wu_ref[:, pl.ds(c, chunk)])
        a = (g * _sigmoid(g) * u).astype(BF16)
        acc_ref[...] += _dot(a, wd_ref[pl.ds(c, chunk), :])
        return carry

    lax.fori_loop(0, wg_ref.shape[1] // chunk, body, 0)


def _ple(h, p_ref, pn_ref, pg_ref, pp_ref):
    gate = _sigmoid(_dot(_rms(h, pn_ref[...]).astype(BF16), pg_ref[...]))
    return h + gate * _dot(p_ref[...].astype(BF16), pp_ref[...])


def _dense_layer_kernel(h_ref, oa_ref, ob_ref, oc_ref, p_ref, wo_ref, gn_ref, wg_ref, wu_ref,
                        wd_ref, pn_ref, pg_ref, pp_ref, out_ref, acc_ref, *, ff_chunk):
    h1 = _mix_residual(h_ref, oa_ref, ob_ref, oc_ref, wo_ref)
    xn = _rms(h1, gn_ref[...]).astype(BF16)
    acc_ref[...] = h1
    _swiglu_accumulate(xn, wg_ref, wu_ref, wd_ref, acc_ref, ff_chunk)
    out_ref[...] = _ple(acc_ref[...], p_ref, pn_ref, pg_ref, pp_ref)


def dense_layer(h, oa, ob, oc, p, wo, gn, wg, wu, wd, pn, pg, pp, *, tm, ff_chunk):
    n, d = h.shape
    row = lambda w: pl.BlockSpec((tm, w), lambda i: (i, 0))
    weights = (wo, gn, wg, wu, wd, pn, pg, pp)
    return pl.pallas_call(
        functools.partial(_dense_layer_kernel, ff_chunk=ff_chunk),
        grid=(n // tm,),
        in_specs=[row(d), row(oa.shape[1]), row(ob.shape[1]), row(oc.shape[1]), row(p.shape[1])]
                 + [_resident(w.shape) for w in weights],
        out_specs=row(d),
        out_shape=jax.ShapeDtypeStruct((n, d), F32),
        scratch_shapes=[pltpu.VMEM((tm, d), F32)],
        compiler_params=_params(("arbitrary",)),
        name="dense_layer",
    )(h, oa, ob, oc, p, *weights)


def _pack_halves(x):
    half = x.shape[1] // 2
    lo = pltpu.bitcast(x[:, :half].astype(BF16).astype(F32), jnp.uint32)
    hi = pltpu.bitcast(x[:, half:].astype(BF16).astype(F32), jnp.uint32)
    return (lo >> 16) | (hi & jnp.uint32(0xFFFF0000))


def _unpack_halves(w):
    lo = pltpu.bitcast(w << 16, F32).astype(BF16)
    hi = pltpu.bitcast(w & jnp.uint32(0xFFFF0000), F32).astype(BF16)
    return lo, hi


def _router_kernel(h_ref, oa_ref, ob_ref, oc_ref, wo_ref, gn_ref, wr_ref,
                   h1_ref, xpk_ref, meta_ref, cnt_ref):
    @pl.when(pl.program_id(0) == 0)
    def _():
        cnt_ref[...] = jnp.zeros_like(cnt_ref)

    h1 = _mix_residual(h_ref, oa_ref, ob_ref, oc_ref, wo_ref)
    h1_ref[...] = h1
    xn = _rms(h1, gn_ref[...])
    xpk_ref[...] = _pack_halves(xn)

    logits = jnp.dot(xn, wr_ref[...], precision=lax.Precision.HIGHEST,
                     preferred_element_type=F32)
    tm = logits.shape[0]
    lane = lax.broadcasted_iota(jnp.int32, logits.shape, 1)
    logits = jnp.where(lane < N_EXPERTS, logits, -jnp.inf)

    def top(vals):
        best = jnp.max(vals, axis=-1, keepdims=True)
        idx = jnp.min(jnp.where(vals == best, lane, LANES), axis=-1, keepdims=True)
        return best, idx

    v1, i1 = top(logits)
    v2, i2 = top(jnp.where(lane == i1, -jnp.inf, logits))
    e = jnp.exp(v2 - v1)
    g1 = 1.0 / (1.0 + e)
    g2 = e / (1.0 + e)

    oh1 = (lane == i1).astype(F32)
    oh2 = (lane == i2).astype(F32)
    both = oh1 + oh2
    r = lax.broadcasted_iota(jnp.int32, (tm, tm), 0)
    c = lax.broadcasted_iota(jnp.int32, (tm, tm), 1)
    before = _dot((c < r).astype(BF16), both.astype(BF16)) + cnt_ref[...]
    rank1 = jnp.sum(before * oh1, axis=-1, keepdims=True)
    rank2 = jnp.sum(before * oh2, axis=-1, keepdims=True)
    cnt_ref[...] += jnp.sum(both, axis=0, keepdims=True)

    meta = jnp.zeros(logits.shape, F32)
    for k, val in enumerate((i1.astype(F32), i2.astype(F32), rank1, rank2, g1, g2)):
        meta = jnp.where(lane == k, val, meta)
    meta_ref[...] = meta


def route_tokens(h, oa, ob, oc, wo, gn, wr, *, tm):
    n, d = h.shape
    row = lambda w: pl.BlockSpec((tm, w), lambda i: (i, 0))
    weights = (wo, gn, wr)
    return pl.pallas_call(
        _router_kernel,
        grid=(n // tm,),
        in_specs=[row(d), row(oa.shape[1]), row(ob.shape[1]), row(oc.shape[1])]
                 + [_resident(w.shape) for w in weights],
        out_specs=[row(d), row(d // 2), row(LANES), pl.BlockSpec((1, LANES), lambda i: (0, 0))],
        out_shape=[jax.ShapeDtypeStruct((n, d), F32),
                   jax.ShapeDtypeStruct((n, d // 2), jnp.uint32),
                   jax.ShapeDtypeStruct((n, LANES), F32),
                   jax.ShapeDtypeStruct((1, LANES), F32)],
        compiler_params=_params(("arbitrary",)),
        name="router",
    )(h, oa, ob, oc, *weights)


def _dispatch_kernel(pos_ref, x_ref, zero_ref, xs_ref, sem):
    del zero_ref
    tm = x_ref.shape[0]

    def row_copy(r, k):
        return pltpu.make_async_copy(x_ref.at[pl.ds(r, 1), :],
                                     xs_ref.at[pl.ds(pos_ref[0, k, r], 1), :], sem)

    def start(r, carry):
        row_copy(r, 0).start()
        row_copy(r, 1).start()
        return carry

    def wait(r, carry):
        row_copy(r, 0).wait()
        row_copy(r, 1).wait()
        return carry

    lax.fori_loop(0, tm, start, 0)
    lax.fori_loop(0, tm, wait, 0)


def dispatch(xpk, pos, n_slots, *, tm):
    n, w = xpk.shape
    zeros = jnp.zeros((n_slots, w), jnp.uint32)
    return pl.pallas_call(
        _dispatch_kernel,
        grid=(n // tm,),
        in_specs=[pl.BlockSpec((1, 2, tm), lambda i: (i, 0, 0), memory_space=pltpu.SMEM),
                  pl.BlockSpec((tm, w), lambda i: (i, 0)),
                  pl.BlockSpec(memory_space=pl.ANY)],
        out_specs=pl.BlockSpec(memory_space=pl.ANY),
        out_shape=jax.ShapeDtypeStruct((n_slots, w), jnp.uint32),
        scratch_shapes=[pltpu.SemaphoreType.DMA(())],
        input_output_aliases={2: 0},
        compiler_params=_params(("arbitrary",)),
        name="moe_dispatch",
    )(pos, xpk, zeros)


def _experts_kernel(te_ref, nt_ref, x_ref, wg_ref, wu_ref, wd_ref, o_ref, x_scr, acc_ref,
                    *, sub_chunk):
    del te_ref
    t = pl.program_id(0)
    c = pl.program_id(1)

    @pl.when(t < nt_ref[0])
    def _():
        @pl.when(c == 0)
        def _():
            lo, hi = _unpack_halves(x_ref[...])
            half = lo.shape[1]
            x_scr[:, :half] = lo
            x_scr[:, half:] = hi
            acc_ref[...] = jnp.zeros_like(acc_ref)

        _swiglu_accumulate(x_scr[...], wg_ref, wu_ref, wd_ref, acc_ref, sub_chunk)

        @pl.when(c == pl.num_programs(1) - 1)
        def _():
            o_ref[...] = acc_ref[...]

    @pl.when(jnp.logical_and(t >= nt_ref[0], c == pl.num_programs(1) - 1))
    def _():
        o_ref[...] = jnp.zeros_like(o_ref)


def experts(xs, tile_expert, n_tiles, wg, wu, wd, *, tile, ff_chunk, sub_chunk):
    n_slots, w = xs.shape
    d = 2 * w
    d_ff = wg.shape[2]
    nc = d_ff // ff_chunk
    grid_tiles = n_slots // tile

    def tile_idx(t, c, te, nt):
        return (jnp.maximum(jnp.minimum(t, nt[0] - 1), 0), 0)

    def chunk_idx(t, c, nt):
        return jnp.where(t < nt[0], c, nc - 1)

    grid_spec = pltpu.PrefetchScalarGridSpec(
        num_scalar_prefetch=2,
        grid=(grid_tiles, nc),
        in_specs=[pl.BlockSpec((tile, w), tile_idx),
                  pl.BlockSpec((None, d, ff_chunk), lambda t, c, te, nt: (te[t], 0, chunk_idx(t, c, nt))),
                  pl.BlockSpec((None, d, ff_chunk), lambda t, c, te, nt: (te[t], 0, chunk_idx(t, c, nt))),
                  pl.BlockSpec((None, ff_chunk, d), lambda t, c, te, nt: (te[t], chunk_idx(t, c, nt), 0))],
        out_specs=pl.BlockSpec((tile, d), lambda t, c, te, nt: (t, 0)),
        scratch_shapes=[pltpu.VMEM((tile, d), BF16), pltpu.VMEM((tile, d), F32)],
    )
    return pl.pallas_call(
        functools.partial(_experts_kernel, sub_chunk=sub_chunk),
        grid_spec=grid_spec,
        out_shape=jax.ShapeDtypeStruct((n_slots, d), F32),
        compiler_params=_params(("arbitrary", "arbitrary")),
        name="moe_experts",
    )(tile_expert, n_tiles, xs, wg, wu, wd)


def _combine_kernel(pos_ref, h_ref, meta_ref, p_ref, pn_ref, pg_ref, pp_ref, fn_ref, ys_ref,
                    out_ref, y1_scr, y2_scr, sem):
    tm = h_ref.shape[0]
    bufs = (y1_scr, y2_scr)

    def row_copy(r, k):
        return pltpu.make_async_copy(ys_ref.at[pl.ds(pos_ref[0, k, r], 1), :],
                                     bufs[k].at[pl.ds(r, 1), :], sem)

    def start(r, carry):
        row_copy(r, 0).start()
        row_copy(r, 1).start()
        return carry

    def wait(r, carry):
        row_copy(r, 0).wait()
        row_copy(r, 1).wait()
        return carry

    lax.fori_loop(0, tm, start, 0)
    lax.fori_loop(0, tm, wait, 0)

    meta = meta_ref[...]
    lane = lax.broadcasted_iota(jnp.int32, meta.shape, 1)
    g1 = jnp.sum(jnp.where(lane == 4, meta, 0.0), axis=-1, keepdims=True)
    g2 = jnp.sum(jnp.where(lane == 5, meta, 0.0), axis=-1, keepdims=True)
    h2 = h_ref[...] + g1 * y1_scr[...] + g2 * y2_scr[...]
    h3 = _ple(h2, p_ref, pn_ref, pg_ref, pp_ref)
    out_ref[...] = _rms(h3, fn_ref[...])


def combine(pos, h1, meta, p, pn, pg, pp, fn, ys, *, tm):
    n, d = h1.shape
    row = lambda w: pl.BlockSpec((tm, w), lambda i: (i, 0))
    weights = (pn, pg, pp, fn)
    return pl.pallas_call(
        _combine_kernel,
        grid=(n // tm,),
        in_specs=[pl.BlockSpec((1, 2, tm), lambda i: (i, 0, 0), memory_space=pltpu.SMEM),
                  row(d), row(LANES), row(p.shape[1])]
                 + [_resident(w.shape) for w in weights]
                 + [pl.BlockSpec(memory_space=pl.ANY)],
        out_specs=row(d),
        out_shape=jax.ShapeDtypeStruct((n, d), F32),
        scratch_shapes=[pltpu.VMEM((tm, d), F32), pltpu.VMEM((tm, d), F32),
                        pltpu.SemaphoreType.DMA(())],
        compiler_params=_params(("arbitrary",)),
        name="moe_combine",
    )(pos, h1, meta, p, *weights, ys)


def _prep_w_in(w):
    scale = HEAD_DIM ** -0.5
    qa, ka, va = w[:, 0:512] * scale, w[:, 512:1024], w[:, 1024:1536]
    qb, kb, vb = w[:, 1536:1792] * scale, w[:, 1792:2048], w[:, 2048:2304]
    fb = w[:, 2304:2308]
    qc, kc, vc, gc = w[:, 2308:2564], w[:, 2564:2820] * scale, w[:, 2820:3076], w[:, 3076:3332]
    main = jnp.concatenate([qa, ka, va, qb, kb, vb, qc, kc, vc, gc], axis=1).astype(BF16)
    f = jnp.pad(fb, ((0, 0), (0, LANES - fb.shape[1]))).astype(BF16)
    return main, f


def _pad_lanes(v):
    return jnp.pad(v, (0, LANES - v.shape[0]))[None, :]


class Tiles(NamedTuple):
    rows: int
    attn: int
    scan: int
    ret_chunk: int
    ff_chunk: int
    moe_tile: int
    moe_ff_chunk: int
    moe_sub_chunk: int


TILES = Tiles(rows=512, attn=512, scan=512, ret_chunk=256, ff_chunk=256,
              moe_tile=512, moe_ff_chunk=1792, moe_sub_chunk=256)


def _mixer(h, b, s, norm_g, w_in, b_forget, lq1, lk1, lq2, lk2, subln, ret_gn, layer_idx, t):
    w_main, w_f = _prep_w_in(w_in)
    proj, f = norm_proj(h, norm_g[None, :], w_main, w_f, tm=t.rows)
    proj = proj.reshape(b, s, -1)
    cum_col, cum_row = fox_prep(f.reshape(b, s, LANES), _pad_lanes(b_forget), blk=t.scan)
    lam_init = 0.8 - 0.6 * math.exp(-0.3 * layer_idx)
    lam = jnp.exp(jnp.sum(lq1 * lk1)) - jnp.exp(jnp.sum(lq2 * lk2)) + lam_init
    oa = diff_attention(proj, lam.reshape(1), subln[None, :], tq=t.attn,
                        out_scale=1.0 - lam_init)
    ob = fox_attention(proj, cum_col, cum_row, tq=t.attn)
    oc = retention(proj, ret_gn[None, :], chunk=t.ret_chunk)
    n = b * s
    return oa.reshape(n, -1), ob.reshape(n, -1), oc.reshape(n, -1)


def _route_plan(meta, counts, tile, n_slots, tm):
    n = meta.shape[0]
    idx = meta[:, 0:2].astype(jnp.int32)
    rank = meta[:, 2:4].astype(jnp.int32)
    cnt = counts[0, :N_EXPERTS].astype(jnp.int32)
    padded = ((cnt + tile - 1) // tile) * tile
    ends = jnp.cumsum(padded)
    starts = ends - padded
    pos = starts[idx] + rank
    n_tiles = ends[-1] // tile
    tiles = jnp.arange(n_slots // tile, dtype=jnp.int32)
    tile_expert = jnp.sum(tiles[:, None] >= (ends // tile)[None, :], axis=1)
    last = jnp.sum(n_tiles - 1 >= ends // tile)
    tile_expert = jnp.where(tiles < n_tiles, tile_expert, last).astype(jnp.int32)
    tile_expert = jnp.clip(tile_expert, 0, N_EXPERTS - 1)
    pos = jnp.clip(pos, 0, n_slots - 1)
    pos = pos.reshape(n // tm, tm, 2).transpose(0, 2, 1)
    return pos, tile_expert, n_tiles.reshape(1).astype(jnp.int32)


def forward(x, p, norm_mix, w_in, b_forget, lambda_q1, lambda_k1, lambda_q2, lambda_k2,
            diff_subln, ret_gn, w_out, norm_ffn, dense_w_gate, dense_w_up, dense_w_down,
            router, moe_w_gate, moe_w_up, moe_w_down, ple_norm, ple_gate, ple_proj, final_norm,
            t=TILES):
    assert w_in.shape[0] == 2, "two layers: dense SwiGLU, then routed experts"
    b, s, d = x.shape
    n = b * s
    h = x.reshape(n, d)
    pf = p.reshape(p.shape[0], n, p.shape[-1])
    bf = lambda a: a.astype(BF16)
    row = lambda a: a[None, :]

    oa, ob, oc = _mixer(h, b, s, norm_mix[0], w_in[0], b_forget[0], lambda_q1[0], lambda_k1[0],
                        lambda_q2[0], lambda_k2[0], diff_subln[0], ret_gn[0], 0, t)
    h = dense_layer(h, oa, ob, oc, pf[0], bf(w_out[0]), row(norm_ffn[0]), bf(dense_w_gate[0]),
                    bf(dense_w_up[0]), bf(dense_w_down[0]), row(ple_norm[0]), bf(ple_gate[0]),
                    bf(ple_proj[0]), tm=t.rows, ff_chunk=t.ff_chunk)

    oa, ob, oc = _mixer(h, b, s, norm_mix[1], w_in[1], b_forget[1], lambda_q1[1], lambda_k1[1],
                        lambda_q2[1], lambda_k2[1], diff_subln[1], ret_gn[1], 1, t)
    wr = jnp.pad(router[0], ((0, 0), (0, LANES - N_EXPERTS)))
    h1, xpk, meta, counts = route_tokens(h, oa, ob, oc, bf(w_out[1]), row(norm_ffn[1]), wr,
                                         tm=t.rows)
    n_slots = 2 * n + N_EXPERTS * t.moe_tile
    pos, tile_expert, n_tiles = _route_plan(meta, counts, t.moe_tile, n_slots, t.rows)
    xs = dispatch(xpk, pos, n_slots, tm=t.rows)
    ys = experts(xs, tile_expert, n_tiles, bf(moe_w_gate[0]), bf(moe_w_up[0]), bf(moe_w_down[0]),
                 tile=t.moe_tile, ff_chunk=t.moe_ff_chunk, sub_chunk=t.moe_sub_chunk)
    out = combine(pos, h1, meta, pf[1], row(ple_norm[1]), bf(ple_gate[1]), bf(ple_proj[1]),
                  row(final_norm), ys, tm=t.rows)
    return out.reshape(b, s, d)


def kernel(x, p, norm_mix, w_in, b_forget, lambda_q1, lambda_k1, lambda_q2, lambda_k2,
           diff_subln, ret_gn, w_out, norm_ffn, dense_w_gate, dense_w_up, dense_w_down,
           router, moe_w_gate, moe_w_up, moe_w_down, ple_norm, ple_gate, ple_proj, final_norm):
    return forward(x, p, norm_mix, w_in, b_forget, lambda_q1, lambda_k1, lambda_q2, lambda_k2,
                   diff_subln, ret_gn, w_out, norm_ffn, dense_w_gate, dense_w_up, dense_w_down,
                   router, moe_w_gate, moe_w_up, moe_w_down, ple_norm, ple_gate, ple_proj,
                   final_norm)
```

```python
import functools
import math
from typing import NamedTuple

import numpy as np
import jax
import jax.numpy as jnp
from jax import lax
from jax.experimental import pallas as pl
from jax.experimental.pallas import tpu as pltpu

F32 = jnp.float32
BF16 = jnp.bfloat16

LANES = 128
VMEM_LIMIT_BYTES = 56 * 1024 * 1024

HEAD_DIM = 64
DIFF_HEADS = 4
FOX_HEADS = 4
RET_HEADS = 4
RET_THETA = 10000.0
N_EXPERTS = 8
EPS = 1e-6
NEG_INF = -1e30
LOG2_E = math.log2(math.e)

QA, KA, VA = 0, 4, 8
QB, KB, VB = 12, 14, 16
QC, KC, VC, GC = 18, 20, 22, 24
N_PROJ_BLOCKS = 26


def _params(semantics):
    return pltpu.CompilerParams(dimension_semantics=semantics,
                                vmem_limit_bytes=VMEM_LIMIT_BYTES)


def _resident(shape):
    nd = len(shape)
    return pl.BlockSpec(shape, lambda *_: (0,) * nd, pipeline_mode=pl.Buffered(1))


def _rms(x, g):
    return x * lax.rsqrt(jnp.mean(x * x, axis=-1, keepdims=True) + EPS) * g


def _sigmoid(x):
    return 1.0 / (1.0 + jnp.exp(-x))


def _dot(a, b):
    return jnp.dot(a, b, preferred_element_type=F32)


def _dot_nt(a, b):
    return lax.dot_general(a, b, (((1,), (1,)), ((), ())), preferred_element_type=F32)


def _lane_masks(shape):
    lane = lax.broadcasted_iota(jnp.int32, shape, len(shape) - 1)
    return lane < HEAD_DIM


def _norm_proj_kernel(h_ref, g_ref, w_ref, wf_ref, out_ref, f_ref, *, col_chunk):
    xn = _rms(h_ref[...], g_ref[...]).astype(BF16)
    ncol = w_ref.shape[1]
    for c in range(0, ncol, col_chunk):
        w = min(col_chunk, ncol - c)
        out_ref[:, c:c + w] = _dot(xn, w_ref[:, c:c + w]).astype(out_ref.dtype)
    f_ref[...] = _dot(xn, wf_ref[...])


def norm_proj(h, g, w_main, w_f, *, tm):
    n, d = h.shape
    ncol = w_main.shape[1]
    return pl.pallas_call(
        functools.partial(_norm_proj_kernel, col_chunk=512),
        grid=(n // tm,),
        in_specs=[pl.BlockSpec((tm, d), lambda i: (i, 0)),
                  _resident((1, d)), _resident((d, ncol)), _resident((d, LANES))],
        out_specs=[pl.BlockSpec((tm, ncol), lambda i: (i, 0)),
                   pl.BlockSpec((tm, LANES), lambda i: (i, 0))],
        out_shape=[jax.ShapeDtypeStruct((n, ncol), BF16),
                   jax.ShapeDtypeStruct((n, LANES), F32)],
        compiler_params=_params(("arbitrary",)),
        name="norm_proj",
    )(h, g, w_main, w_f)


def _fox_prep_kernel(f_ref, b_ref, col_ref, row_ref, carry_ref):
    @pl.when(pl.program_id(1) == 0)
    def _():
        carry_ref[...] = jnp.zeros_like(carry_ref)

    x = f_ref[...] + b_ref[...]
    log_f = (jnp.minimum(x, 0.0) - jnp.log1p(jnp.exp(-jnp.abs(x)))) * LOG2_E
    blk = x.shape[0]
    r = lax.broadcasted_iota(jnp.int32, (blk, blk), 0)
    c = lax.broadcasted_iota(jnp.int32, (blk, blk), 1)
    tri = (c <= r).astype(F32)
    cum = jnp.dot(tri, log_f, precision=lax.Precision.HIGHEST,
                  preferred_element_type=F32) + carry_ref[...]
    col_ref[...] = cum
    row_ref[...] = cum.T[:8, :]
    carry_ref[...] = cum[blk - 1:blk, :]


def fox_prep(f, b_pad, *, blk):
    bsz, s, _ = f.shape
    return pl.pallas_call(
        _fox_prep_kernel,
        grid=(bsz, s // blk),
        in_specs=[pl.BlockSpec((None, blk, LANES), lambda b, j: (b, j, 0)),
                  pl.BlockSpec((1, LANES), lambda b, j: (0, 0))],
        out_specs=[pl.BlockSpec((None, blk, LANES), lambda b, j: (b, j, 0)),
                   pl.BlockSpec((None, 8, blk), lambda b, j: (b, 0, j))],
        out_shape=[jax.ShapeDtypeStruct((bsz, s, LANES), F32),
                   jax.ShapeDtypeStruct((bsz, 8, s), F32)],
        scratch_shapes=[pltpu.VMEM((1, LANES), F32)],
        compiler_params=_params(("arbitrary", "arbitrary")),
        name="fox_prep",
    )(f, b_pad)


def _attn_kernel(*refs, mode, tq, out_scale):
    if mode == "diff":
        lam_ref, q_ref, k_ref, vt_ref, gain_ref, o_ref, m_scr, l_scr, acc_scr = refs
    else:
        q_ref, k_ref, vt_ref, ccol_ref, crow_ref, o_ref, m_scr, l_scr, acc_scr, cs_scr = refs
    qi = pl.program_id(2)
    pair = pl.program_id(1)

    q = q_ref[...]
    lo = _lane_masks(q.shape)
    zero = jnp.zeros_like(q)
    q_maps = (jnp.where(lo, q, zero), jnp.where(lo, zero, q))

    m_scr[...] = jnp.full_like(m_scr, NEG_INF)
    l_scr[...] = jnp.zeros_like(l_scr)
    acc_scr[...] = jnp.zeros_like(acc_scr)

    if mode == "fox":
        @pl.when(qi == 0)
        def _():
            ccol = ccol_ref[...]
            lane = lax.broadcasted_iota(jnp.int32, ccol.shape, 1)
            for i in range(2):
                col = jnp.sum(jnp.where(lane == 2 * pair + i, ccol, 0.0), axis=-1, keepdims=True)
                cs_scr[i] = jnp.broadcast_to(col, ccol.shape)

        ct = tuple(crow_ref[pl.ds(2 * pair + i, 1), :] for i in range(2))

    def chunk(start, masked):
        k = k_ref[pl.ds(start, tq), :]
        vt = vt_ref[:, pl.ds(start, tq)]
        for i in range(2):
            st = _dot_nt(k, q_maps[i])
            if mode == "fox":
                cs = cs_scr[i, pl.ds(start, tq), :]
                st = st + ct[i] - jnp.tile(cs, (1, tq // LANES))
            if masked:
                key = lax.broadcasted_iota(jnp.int32, st.shape, 0)
                qry = lax.broadcasted_iota(jnp.int32, st.shape, 1)
                st = jnp.where(key <= qry, st, NEG_INF)
            m_old = m_scr[i]
            m_new = jnp.maximum(m_old, jnp.max(st, axis=0, keepdims=True))
            alpha = jnp.exp2(m_old - m_new)
            p = jnp.exp2(st - m_new)
            l_scr[i] = alpha * l_scr[i] + jnp.sum(p, axis=0, keepdims=True)
            acc_scr[i] = alpha * acc_scr[i] + _dot(vt, p.astype(BF16))
            m_scr[i] = m_new

    def body(j, carry):
        chunk(pl.multiple_of(j * tq, tq), False)
        return carry

    lax.fori_loop(0, qi, body, 0)
    chunk(pl.multiple_of(qi * tq, tq), True)

    o1 = acc_scr[0] * (1.0 / l_scr[0])
    o2 = acc_scr[1] * (1.0 / l_scr[1])
    if mode == "diff":
        o = (o1 - lam_ref[0] * o2).T
        o = _rms(o, gain_ref[...]) * out_scale
    else:
        chan = lax.broadcasted_iota(jnp.int32, o1.shape, 0)
        o = jnp.where(chan < HEAD_DIM, o1, o2).T
    o_ref[...] = o.astype(o_ref.dtype)


def _attn_scratch(tq):
    return [pltpu.VMEM((2, 1, tq), F32), pltpu.VMEM((2, 1, tq), F32),
            pltpu.VMEM((2, LANES, tq), F32)]


def diff_attention(proj, vt, lam, gain, *, tq, out_scale):
    bsz, s, _ = proj.shape
    kern = functools.partial(_attn_kernel, mode="diff", tq=tq, out_scale=out_scale)
    return pl.pallas_call(
        kern,
        grid=(bsz, DIFF_HEADS, s // tq),
        in_specs=[pl.BlockSpec(memory_space=pltpu.SMEM),
                  pl.BlockSpec((None, tq, LANES), lambda b, h, i: (b, i, QA + h)),
                  pl.BlockSpec((None, s, LANES), lambda b, h, i: (b, 0, KA + h)),
                  pl.BlockSpec((None, LANES, s), lambda b, h, i: (b, h, 0)),
                  pl.BlockSpec((1, LANES), lambda b, h, i: (0, 0))],
        out_specs=pl.BlockSpec((None, tq, LANES), lambda b, h, i: (b, i, h)),
        out_shape=jax.ShapeDtypeStruct((bsz, s, DIFF_HEADS * LANES), BF16),
        scratch_shapes=_attn_scratch(tq),
        compiler_params=_params(("arbitrary", "arbitrary", "arbitrary")),
        name="diff_attention",
    )(lam, proj, proj, vt, gain)


def fox_attention(proj, vt, cum_col, cum_row, *, tq):
    bsz, s, _ = proj.shape
    npair = FOX_HEADS // 2
    kern = functools.partial(_attn_kernel, mode="fox", tq=tq, out_scale=None)
    return pl.pallas_call(
        kern,
        grid=(bsz, npair, s // tq),
        in_specs=[pl.BlockSpec((None, tq, LANES), lambda b, h, i: (b, i, QB + h)),
                  pl.BlockSpec((None, s, LANES), lambda b, h, i: (b, 0, KB + h)),
                  pl.BlockSpec((None, LANES, s), lambda b, h, i: (b, h, 0)),
                  pl.BlockSpec((None, s, LANES), lambda b, h, i: (b, 0, 0)),
                  pl.BlockSpec((None, 8, tq), lambda b, h, i: (b, 0, i))],
        out_specs=pl.BlockSpec((None, tq, LANES), lambda b, h, i: (b, i, h)),
        out_shape=jax.ShapeDtypeStruct((bsz, s, npair * LANES), BF16),
        scratch_shapes=_attn_scratch(tq) + [pltpu.VMEM((2, s, LANES), F32)],
        compiler_params=_params(("arbitrary", "arbitrary", "arbitrary")),
        name="fox_attention",
    )(proj, proj, vt, cum_col, cum_row)


def _retention_tables(s, chunk):
    half = HEAD_DIM // 2
    inv = RET_THETA ** (-np.arange(half, dtype=np.float64) / half)
    ang = np.arange(s, dtype=np.float64)[:, None] * inv[None, :]
    cos = np.tile(np.cos(ang), (1, 4))
    sin = np.tile(np.concatenate([-np.sin(ang), np.sin(ang)], axis=1), (1, 2))
    log_g = np.log1p(-np.exp2(-5.0 - np.arange(RET_HEADS, dtype=np.float64)))
    j = np.arange(chunk, dtype=np.float64)
    diff = j[:, None] - j[None, :]
    inner = np.where(diff[None] >= 0, np.exp(np.where(diff[None] >= 0, diff[None], 0.0)
                                             * log_g[:, None, None]), 0.0)
    q_dec = np.exp((j + 1.0)[None, :] * log_g[:, None])
    k_dec = np.exp((chunk - 1.0 - j)[None, :] * log_g[:, None])
    c_dec = np.exp(chunk * log_g)
    npair = RET_HEADS // 2

    def lanes(t):
        t = t.reshape(npair, 2, chunk)
        return np.repeat(t.transpose(0, 2, 1), HEAD_DIM, axis=2)

    blk = np.kron(np.eye(2), np.ones((HEAD_DIM, HEAD_DIM)))
    c_mat = np.stack([np.kron(np.diag(c_dec[2 * p:2 * p + 2]), np.ones((HEAD_DIM, HEAD_DIM)))
                      for p in range(npair)])
    f = lambda a: jnp.asarray(a, F32)
    return dict(cos=f(cos), sin=f(sin), inner=f(inner.reshape(npair, 2, chunk, chunk)),
                q_dec=f(lanes(q_dec)), k_dec=f(lanes(k_dec)), c_mat=f(c_mat), blk=f(blk))


def _retention_kernel(q_ref, k_ref, v_ref, g_ref, cos_ref, sin_ref, inner_ref, qdec_ref,
                      kdec_ref, cmat_ref, blk_ref, gain_ref, o_ref, state_ref):
    @pl.when(pl.program_id(2) == 0)
    def _():
        state_ref[...] = jnp.zeros_like(state_ref)

    cos = cos_ref[...]
    sin = sin_ref[...]
    lane = lax.broadcasted_iota(jnp.int32, cos.shape, 1)
    first_half = (lane % HEAD_DIM) < (HEAD_DIM // 2)

    def rope(x):
        x = x.astype(F32)
        swapped = jnp.where(first_half, pltpu.roll(x, LANES - HEAD_DIM // 2, 1),
                            pltpu.roll(x, HEAD_DIM // 2, 1))
        return x * cos + swapped * sin

    qr = rope(q_ref[...])
    kr = rope(k_ref[...])
    v = v_ref[...]
    lo = _lane_masks(qr.shape)
    qb = qr.astype(BF16)
    kb = kr.astype(BF16)
    zero = jnp.zeros_like(qb)
    a1 = (_dot_nt(jnp.where(lo, qb, zero), kb) * inner_ref[0]).astype(BF16)
    a2 = (_dot_nt(jnp.where(lo, zero, qb), kb) * inner_ref[1]).astype(BF16)
    state = state_ref[...]
    o = jnp.where(lo, _dot(a1, v), _dot(a2, v))
    o = o + _dot((qr * qdec_ref[...]).astype(BF16), state.astype(BF16))
    kd = (kr * kdec_ref[...]).astype(BF16)
    state_ref[...] = state * cmat_ref[...] + _dot(kd.T, v) * blk_ref[...]

    inv_n = 1.0 / HEAD_DIM
    sum_lo = jnp.sum(jnp.where(lo, o, 0.0), axis=-1, keepdims=True)
    sum_hi = jnp.sum(jnp.where(lo, 0.0, o), axis=-1, keepdims=True)
    d = o - jnp.where(lo, sum_lo, sum_hi) * inv_n
    d2 = d * d
    var_lo = jnp.sum(jnp.where(lo, d2, 0.0), axis=-1, keepdims=True)
    var_hi = jnp.sum(jnp.where(lo, 0.0, d2), axis=-1, keepdims=True)
    y = d * lax.rsqrt(jnp.where(lo, var_lo, var_hi) * inv_n + EPS)
    g = g_ref[...].astype(F32)
    o_ref[...] = (g * _sigmoid(g) * (y * gain_ref[...])).astype(o_ref.dtype)


def retention(proj, gain, *, chunk):
    bsz, s, _ = proj.shape
    npair = RET_HEADS // 2
    t = _retention_tables(s, chunk)

    def act(off):
        return pl.BlockSpec((None, chunk, LANES), lambda b, p, c: (b, c, off + p))

    pos = pl.BlockSpec((chunk, LANES), lambda b, p, c: (c, 0))
    per_pair3 = lambda shape: pl.BlockSpec((None,) + shape, lambda b, p, c: (p,) + (0,) * len(shape))
    return pl.pallas_call(
        _retention_kernel,
        grid=(bsz, npair, s // chunk),
        in_specs=[act(QC), act(KC), act(VC), act(GC), pos, pos,
                  per_pair3((2, chunk, chunk)), per_pair3((chunk, LANES)),
                  per_pair3((chunk, LANES)), per_pair3((LANES, LANES)),
                  pl.BlockSpec((LANES, LANES), lambda b, p, c: (0, 0)),
                  pl.BlockSpec((1, LANES), lambda b, p, c: (0, p))],
        out_specs=pl.BlockSpec((None, chunk, LANES), lambda b, p, c: (b, c, p)),
        out_shape=jax.ShapeDtypeStruct((bsz, s, npair * LANES), BF16),
        scratch_shapes=[pltpu.VMEM((LANES, LANES), F32)],
        compiler_params=_params(("arbitrary", "arbitrary", "arbitrary")),
        name="retention",
    )(proj, proj, proj, proj, t["cos"], t["sin"], t["inner"], t["q_dec"], t["k_dec"],
      t["c_mat"], t["blk"], gain)


def _mix_residual(h_ref, oa_ref, ob_ref, oc_ref, wo_ref):
    wa = oa_ref.shape[1]
    wb = ob_ref.shape[1]
    return (h_ref[...] + _dot(oa_ref[...], wo_ref[0:wa, :])
            + _dot(ob_ref[...], wo_ref[wa:wa + wb, :])
            + _dot(oc_ref[...], wo_ref[wa + wb:, :]))


def _swiglu_accumulate(xn, wg_ref, wu_ref, wd_ref, acc_ref, chunk):
    def body(i, carry):
        c = pl.multiple_of(i * chunk, chunk)
        g = _dot(xn, wg_ref[:, pl.ds(c, chunk)])
        u = _dot(xn, wu_ref[:, pl.ds(c, chunk)])
        a = (g * _sigmoid(g) * u).astype(BF16)
        acc_ref[...] += _dot(a, wd_ref[pl.ds(c, chunk), :])
        return carry

    lax.fori_loop(0, wg_ref.shape[1] // chunk, body, 0)


def _ple(h, p_ref, pn_ref, pg_ref, pp_ref):
    gate = _sigmoid(_dot(_rms(h, pn_ref[...]).astype(BF16), pg_ref[...]))
    return h + gate * _dot(p_ref[...].astype(BF16), pp_ref[...])


def _dense_layer_kernel(h_ref, oa_ref, ob_ref, oc_ref, p_ref, wo_ref, gn_ref, wg_ref, wu_ref,
                        wd_ref, pn_ref, pg_ref, pp_ref, out_ref, acc_ref, *, ff_chunk):
    h1 = _mix_residual(h_ref, oa_ref, ob_ref, oc_ref, wo_ref)
    xn = _rms(h1, gn_ref[...]).astype(BF16)
    acc_ref[...] = h1
    _swiglu_accumulate(xn, wg_ref, wu_ref, wd_ref, acc_ref, ff_chunk)
    out_ref[...] = _ple(acc_ref[...], p_ref, pn_ref, pg_ref, pp_ref)


def dense_layer(h, oa, ob, oc, p, wo, gn, wg, wu, wd, pn, pg, pp, *, tm, ff_chunk):
    n, d = h.shape
    row = lambda w: pl.BlockSpec((tm, w), lambda i: (i, 0))
    weights = (wo, gn, wg, wu, wd, pn, pg, pp)
    return pl.pallas_call(
        functools.partial(_dense_layer_kernel, ff_chunk=ff_chunk),
        grid=(n // tm,),
        in_specs=[row(d), row(oa.shape[1]), row(ob.shape[1]), row(oc.shape[1]), row(p.shape[1])]
                 + [_resident(w.shape) for w in weights],
        out_specs=row(d),
        out_shape=jax.ShapeDtypeStruct((n, d), F32),
        scratch_shapes=[pltpu.VMEM((tm, d), F32)],
        compiler_params=_params(("arbitrary",)),
        name="dense_layer",
    )(h, oa, ob, oc, p, *weights)


def _pack_halves(x):
    half = x.shape[1] // 2
    lo = pltpu.bitcast(x[:, :half].astype(BF16).astype(F32), jnp.uint32)
    hi = pltpu.bitcast(x[:, half:].astype(BF16).astype(F32), jnp.uint32)
    return (lo >> 16) | (hi & jnp.uint32(0xFFFF0000))


def _unpack_halves(w):
    lo = pltpu.bitcast(w << 16, F32).astype(BF16)
    hi = pltpu.bitcast(w & jnp.uint32(0xFFFF0000), F32).astype(BF16)
    return lo, hi


def _router_kernel(h_ref, oa_ref, ob_ref, oc_ref, wo_ref, gn_ref, wr_ref,
                   h1_ref, xpk_ref, meta_ref, cnt_ref):
    @pl.when(pl.program_id(0) == 0)
    def _():
        cnt_ref[...] = jnp.zeros_like(cnt_ref)

    h1 = _mix_residual(h_ref, oa_ref, ob_ref, oc_ref, wo_ref)
    h1_ref[...] = h1
    xn = _rms(h1, gn_ref[...])
    xpk_ref[...] = _pack_halves(xn)

    logits = jnp.dot(xn, wr_ref[...], precision=lax.Precision.HIGHEST,
                     preferred_element_type=F32)
    tm = logits.shape[0]
    lane = lax.broadcasted_iota(jnp.int32, logits.shape, 1)
    logits = jnp.where(lane < N_EXPERTS, logits, -jnp.inf)

    def top(vals):
        best = jnp.max(vals, axis=-1, keepdims=True)
        idx = jnp.min(jnp.where(vals == best, lane, LANES), axis=-1, keepdims=True)
        return best, idx

    v1, i1 = top(logits)
    v2, i2 = top(jnp.where(lane == i1, -jnp.inf, logits))
    e = jnp.exp(v2 - v1)
    g1 = 1.0 / (1.0 + e)
    g2 = e / (1.0 + e)

    oh1 = (lane == i1).astype(F32)
    oh2 = (lane == i2).astype(F32)
    both = oh1 + oh2
    r = lax.broadcasted_iota(jnp.int32, (tm, tm), 0)
    c = lax.broadcasted_iota(jnp.int32, (tm, tm), 1)
    before = _dot((c < r).astype(BF16), both.astype(BF16)) + cnt_ref[...]
    rank1 = jnp.sum(before * oh1, axis=-1, keepdims=True)
    rank2 = jnp.sum(before * oh2, axis=-1, keepdims=True)
    cnt_ref[...] += jnp.sum(both, axis=0, keepdims=True)

    meta = jnp.zeros(logits.shape, F32)
    for k, val in enumerate((i1.astype(F32), i2.astype(F32), rank1, rank2, g1, g2)):
        meta = jnp.where(lane == k, val, meta)
    meta_ref[...] = meta


def route_tokens(h, oa, ob, oc, wo, gn, wr, *, tm):
    n, d = h.shape
    row = lambda w: pl.BlockSpec((tm, w), lambda i: (i, 0))
    weights = (wo, gn, wr)
    return pl.pallas_call(
        _router_kernel,
        grid=(n // tm,),
        in_specs=[row(d), row(oa.shape[1]), row(ob.shape[1]), row(oc.shape[1])]
                 + [_resident(w.shape) for w in weights],
        out_specs=[row(d), row(d // 2), row(LANES), pl.BlockSpec((1, LANES), lambda i: (0, 0))],
        out_shape=[jax.ShapeDtypeStruct((n, d), F32),
                   jax.ShapeDtypeStruct((n, d // 2), jnp.uint32),
                   jax.ShapeDtypeStruct((n, LANES), F32),
                   jax.ShapeDtypeStruct((1, LANES), F32)],
        compiler_params=_params(("arbitrary",)),
        name="router",
    )(h, oa, ob, oc, *weights)


def _dispatch_kernel(pos_ref, x_ref, zero_ref, xs_ref, sem):
    del zero_ref
    tm = x_ref.shape[0]

    def row_copy(r, k):
        return pltpu.make_async_copy(x_ref.at[pl.ds(r, 1), :],
                                     xs_ref.at[pl.ds(pos_ref[0, k, r], 1), :], sem)

    def start(r, carry):
        row_copy(r, 0).start()
        row_copy(r, 1).start()
        return carry

    def wait(r, carry):
        row_copy(r, 0).wait()
        row_copy(r, 1).wait()
        return carry

    lax.fori_loop(0, tm, start, 0)
    lax.fori_loop(0, tm, wait, 0)


def dispatch(xpk, pos, n_slots, *, tm):
    n, w = xpk.shape
    zeros = jnp.zeros((n_slots, w), jnp.uint32)
    return pl.pallas_call(
        _dispatch_kernel,
        grid=(n // tm,),
        in_specs=[pl.BlockSpec((1, 2, tm), lambda i: (i, 0, 0), memory_space=pltpu.SMEM),
                  pl.BlockSpec((tm, w), lambda i: (i, 0)),
                  pl.BlockSpec(memory_space=pl.ANY)],
        out_specs=pl.BlockSpec(memory_space=pl.ANY),
        out_shape=jax.ShapeDtypeStruct((n_slots, w), jnp.uint32),
        scratch_shapes=[pltpu.SemaphoreType.DMA(())],
        input_output_aliases={2: 0},
        compiler_params=_params(("arbitrary",)),
        name="moe_dispatch",
    )(pos, xpk, zeros)


def _experts_kernel(te_ref, nt_ref, x_ref, wg_ref, wu_ref, wd_ref, o_ref, x_scr, acc_ref,
                    *, sub_chunk):
    del te_ref
    t = pl.program_id(0)
    c = pl.program_id(1)

    @pl.when(t < nt_ref[0])
    def _():
        @pl.when(c == 0)
        def _():
            lo, hi = _unpack_halves(x_ref[...])
            half = lo.shape[1]
            x_scr[:, :half] = lo
            x_scr[:, half:] = hi
            acc_ref[...] = jnp.zeros_like(acc_ref)

        _swiglu_accumulate(x_scr[...], wg_ref, wu_ref, wd_ref, acc_ref, sub_chunk)

        @pl.when(c == pl.num_programs(1) - 1)
        def _():
            o_ref[...] = acc_ref[...]

    @pl.when(jnp.logical_and(t >= nt_ref[0], c == pl.num_programs(1) - 1))
    def _():
        o_ref[...] = jnp.zeros_like(o_ref)


def experts(xs, tile_expert, n_tiles, wg, wu, wd, *, tile, ff_chunk, sub_chunk):
    n_slots, w = xs.shape
    d = 2 * w
    d_ff = wg.shape[2]
    nc = d_ff // ff_chunk
    grid_tiles = n_slots // tile

    def tile_idx(t, c, te, nt):
        return (jnp.maximum(jnp.minimum(t, nt[0] - 1), 0), 0)

    def chunk_idx(t, c, nt):
        return jnp.where(t < nt[0], c, nc - 1)

    grid_spec = pltpu.PrefetchScalarGridSpec(
        num_scalar_prefetch=2,
        grid=(grid_tiles, nc),
        in_specs=[pl.BlockSpec((tile, w), tile_idx),
                  pl.BlockSpec((None, d, ff_chunk), lambda t, c, te, nt: (te[t], 0, chunk_idx(t, c, nt))),
                  pl.BlockSpec((None, d, ff_chunk), lambda t, c, te, nt: (te[t], 0, chunk_idx(t, c, nt))),
                  pl.BlockSpec((None, ff_chunk, d), lambda t, c, te, nt: (te[t], chunk_idx(t, c, nt), 0))],
        out_specs=pl.BlockSpec((tile, d), lambda t, c, te, nt: (t, 0)),
        scratch_shapes=[pltpu.VMEM((tile, d), BF16), pltpu.VMEM((tile, d), F32)],
    )
    return pl.pallas_call(
        functools.partial(_experts_kernel, sub_chunk=sub_chunk),
        grid_spec=grid_spec,
        out_shape=jax.ShapeDtypeStruct((n_slots, d), F32),
        compiler_params=_params(("arbitrary", "arbitrary")),
        name="moe_experts",
    )(tile_expert, n_tiles, xs, wg, wu, wd)


def _combine_kernel(pos_ref, h_ref, meta_ref, p_ref, pn_ref, pg_ref, pp_ref, fn_ref, ys_ref,
                    out_ref, y1_scr, y2_scr, sem):
    tm = h_ref.shape[0]
    bufs = (y1_scr, y2_scr)

    def row_copy(r, k):
        return pltpu.make_async_copy(ys_ref.at[pl.ds(pos_ref[0, k, r], 1), :],
                                     bufs[k].at[pl.ds(r, 1), :], sem)

    def start(r, carry):
        row_copy(r, 0).start()
        row_copy(r, 1).start()
        return carry

    def wait(r, carry):
        row_copy(r, 0).wait()
        row_copy(r, 1).wait()
        return carry

    lax.fori_loop(0, tm, start, 0)
    lax.fori_loop(0, tm, wait, 0)

    meta = meta_ref[...]
    lane = lax.broadcasted_iota(jnp.int32, meta.shape, 1)
    g1 = jnp.sum(jnp.where(lane == 4, meta, 0.0), axis=-1, keepdims=True)
    g2 = jnp.sum(jnp.where(lane == 5, meta, 0.0), axis=-1, keepdims=True)
    h2 = h_ref[...] + g1 * y1_scr[...] + g2 * y2_scr[...]
    h3 = _ple(h2, p_ref, pn_ref, pg_ref, pp_ref)
    out_ref[...] = _rms(h3, fn_ref[...])


def combine(pos, h1, meta, p, pn, pg, pp, fn, ys, *, tm):
    n, d = h1.shape
    row = lambda w: pl.BlockSpec((tm, w), lambda i: (i, 0))
    weights = (pn, pg, pp, fn)
    return pl.pallas_call(
        _combine_kernel,
        grid=(n // tm,),
        in_specs=[pl.BlockSpec((1, 2, tm), lambda i: (i, 0, 0), memory_space=pltpu.SMEM),
                  row(d), row(LANES), row(p.shape[1])]
                 + [_resident(w.shape) for w in weights]
                 + [pl.BlockSpec(memory_space=pl.ANY)],
        out_specs=row(d),
        out_shape=jax.ShapeDtypeStruct((n, d), F32),
        scratch_shapes=[pltpu.VMEM((tm, d), F32), pltpu.VMEM((tm, d), F32),
                        pltpu.SemaphoreType.DMA(())],
        compiler_params=_params(("arbitrary",)),
        name="moe_combine",
    )(pos, h1, meta, p, *weights, ys)


def _prep_w_in(w):
    scale = HEAD_DIM ** -0.5
    qa, ka, va = w[:, 0:512] * (scale * LOG2_E), w[:, 512:1024], w[:, 1024:1536]
    qb, kb, vb = w[:, 1536:1792] * (scale * LOG2_E), w[:, 1792:2048], w[:, 2048:2304]
    fb = w[:, 2304:2308]
    qc, kc, vc, gc = w[:, 2308:2564], w[:, 2564:2820] * scale, w[:, 2820:3076], w[:, 3076:3332]
    main = jnp.concatenate([qa, ka, va, qb, kb, vb, qc, kc, vc, gc], axis=1).astype(BF16)
    f = jnp.pad(fb, ((0, 0), (0, LANES - fb.shape[1]))).astype(BF16)
    return main, f


def _pad_lanes(v):
    return jnp.pad(v, (0, LANES - v.shape[0]))[None, :]


class Tiles(NamedTuple):
    rows: int
    attn: int
    scan: int
    ret_chunk: int
    ff_chunk: int
    moe_tile: int
    moe_ff_chunk: int
    moe_sub_chunk: int


TILES = Tiles(rows=512, attn=512, scan=512, ret_chunk=256, ff_chunk=256,
              moe_tile=512, moe_ff_chunk=1792, moe_sub_chunk=256)


def _mixer(h, b, s, norm_g, w_in, b_forget, lq1, lk1, lq2, lk2, subln, ret_gn, layer_idx, t):
    w_main, w_f = _prep_w_in(w_in)
    proj, f = norm_proj(h, norm_g[None, :], w_main, w_f, tm=t.rows)
    proj = proj.reshape(b, s, -1)
    cum_col, cum_row = fox_prep(f.reshape(b, s, LANES), _pad_lanes(b_forget), blk=t.scan)
    lam_init = 0.8 - 0.6 * math.exp(-0.3 * layer_idx)
    lam = jnp.exp(jnp.sum(lq1 * lk1)) - jnp.exp(jnp.sum(lq2 * lk2)) + lam_init
    vt_a = proj[:, :, VA * LANES:QB * LANES].transpose(0, 2, 1)
    vt_b = proj[:, :, VB * LANES:QC * LANES].transpose(0, 2, 1)
    oa = diff_attention(proj, vt_a, lam.reshape(1), subln[None, :], tq=t.attn,
                        out_scale=1.0 - lam_init)
    ob = fox_attention(proj, vt_b, cum_col, cum_row, tq=t.attn)
    oc = retention(proj, ret_gn[None, :], chunk=t.ret_chunk)
    n = b * s
    return oa.reshape(n, -1), ob.reshape(n, -1), oc.reshape(n, -1)


def _route_plan(meta, counts, tile, n_slots, tm):
    n = meta.shape[0]
    idx = meta[:, 0:2].astype(jnp.int32)
    rank = meta[:, 2:4].astype(jnp.int32)
    cnt = counts[0, :N_EXPERTS].astype(jnp.int32)
    padded = ((cnt + tile - 1) // tile) * tile
    ends = jnp.cumsum(padded)
    starts = ends - padded
    pos = starts[idx] + rank
    n_tiles = ends[-1] // tile
    tiles = jnp.arange(n_slots // tile, dtype=jnp.int32)
    tile_expert = jnp.sum(tiles[:, None] >= (ends // tile)[None, :], axis=1)
    last = jnp.sum(n_tiles - 1 >= ends // tile)
    tile_expert = jnp.where(tiles < n_tiles, tile_expert, last).astype(jnp.int32)
    tile_expert = jnp.clip(tile_expert, 0, N_EXPERTS - 1)
    pos = jnp.clip(pos, 0, n_slots - 1)
    pos = pos.reshape(n // tm, tm, 2).transpose(0, 2, 1)
    return pos, tile_expert, n_tiles.reshape(1).astype(jnp.int32)


def forward(x, p, norm_mix, w_in, b_forget, lambda_q1, lambda_k1, lambda_q2, lambda_k2,
            diff_subln, ret_gn, w_out, norm_ffn, dense_w_gate, dense_w_up, dense_w_down,
            router, moe_w_gate, moe_w_up, moe_w_down, ple_norm, ple_gate, ple_proj, final_norm,
            t=TILES):
    assert w_in.shape[0] == 2, "two layers: dense SwiGLU, then routed experts"
    b, s, d = x.shape
    n = b * s
    h = x.reshape(n, d)
    pf = p.reshape(p.shape[0], n, p.shape[-1])
    bf = lambda a: a.astype(BF16)
    row = lambda a: a[None, :]

    oa, ob, oc = _mixer(h, b, s, norm_mix[0], w_in[0], b_forget[0], lambda_q1[0], lambda_k1[0],
                        lambda_q2[0], lambda_k2[0], diff_subln[0], ret_gn[0], 0, t)
    h = dense_layer(h, oa, ob, oc, pf[0], bf(w_out[0]), row(norm_ffn[0]), bf(dense_w_gate[0]),
                    bf(dense_w_up[0]), bf(dense_w_down[0]), row(ple_norm[0]), bf(ple_gate[0]),
                    bf(ple_proj[0]), tm=t.rows, ff_chunk=t.ff_chunk)

    oa, ob, oc = _mixer(h, b, s, norm_mix[1], w_in[1], b_forget[1], lambda_q1[1], lambda_k1[1],
                        lambda_q2[1], lambda_k2[1], diff_subln[1], ret_gn[1], 1, t)
    wr = jnp.pad(router[0], ((0, 0), (0, LANES - N_EXPERTS)))
    h1, xpk, meta, counts = route_tokens(h, oa, ob, oc, bf(w_out[1]), row(norm_ffn[1]), wr,
                                         tm=t.rows)
    n_slots = 2 * n + N_EXPERTS * t.moe_tile
    pos, tile_expert, n_tiles = _route_plan(meta, counts, t.moe_tile, n_slots, t.rows)
    xs = dispatch(xpk, pos, n_slots, tm=t.rows)
    ys = experts(xs, tile_expert, n_tiles, bf(moe_w_gate[0]), bf(moe_w_up[0]), bf(moe_w_down[0]),
                 tile=t.moe_tile, ff_chunk=t.moe_ff_chunk, sub_chunk=t.moe_sub_chunk)
    out = combine(pos, h1, meta, pf[1], row(ple_norm[1]), bf(ple_gate[1]), bf(ple_proj[1]),
                  row(final_norm), ys, tm=t.rows)
    return out.reshape(b, s, d)


def kernel(x, p, norm_mix, w_in, b_forget, lambda_q1, lambda_k1, lambda_q2, lambda_k2,
           diff_subln, ret_gn, w_out, norm_ffn, dense_w_gate, dense_w_up, dense_w_down,
           router, moe_w_gate, moe_w_up, moe_w_down, ple_norm, ple_gate, ple_proj, final_norm):
    return forward(x, p, norm_mix, w_in, b_forget, lambda_q1, lambda_k1, lambda_q2, lambda_k2,
                   diff_subln, ret_gn, w_out, norm_ffn, dense_w_gate, dense_w_up, dense_w_down,
                   router, moe_w_gate, moe_w_up, moe_w_down, ple_norm, ple_gate, ple_proj,
                   final_norm)
```

```python
import functools
import math
from typing import NamedTuple

import numpy as np
import jax
import jax.numpy as jnp
from jax import lax
from jax.experimental import pallas as pl
from jax.experimental.pallas import tpu as pltpu

F32 = jnp.float32
BF16 = jnp.bfloat16

LANES = 128
ROW_UNROLL = 8
VMEM_LIMIT_BYTES = 56 * 1024 * 1024

HEAD_DIM = 64
DIFF_HEADS = 4
FOX_HEADS = 4
RET_HEADS = 4
RET_THETA = 10000.0
N_EXPERTS = 8
EPS = 1e-6
NEG_INF = -1e30
LOG2_E = math.log2(math.e)

QA, KA, VA = 0, 4, 8
QB, KB, VB = 12, 14, 16
QC, KC, VC, GC = 18, 20, 22, 24
N_PROJ_BLOCKS = 26


def _params(semantics):
    return pltpu.CompilerParams(dimension_semantics=semantics,
                                vmem_limit_bytes=VMEM_LIMIT_BYTES)


def _resident(shape):
    nd = len(shape)
    return pl.BlockSpec(shape, lambda *_: (0,) * nd, pipeline_mode=pl.Buffered(1))


def _rms(x, g):
    return x * lax.rsqrt(jnp.mean(x * x, axis=-1, keepdims=True) + EPS) * g


def _sigmoid(x):
    return 1.0 / (1.0 + jnp.exp(-x))


def _dot(a, b):
    return jnp.dot(a, b, preferred_element_type=F32)


def _dot_nt(a, b):
    return lax.dot_general(a, b, (((1,), (1,)), ((), ())), preferred_element_type=F32)


def _lane_masks(shape):
    lane = lax.broadcasted_iota(jnp.int32, shape, len(shape) - 1)
    return lane < HEAD_DIM


def _norm_proj_kernel(h_ref, g_ref, w_ref, wf_ref, out_ref, f_ref, *, col_chunk):
    xn = _rms(h_ref[...], g_ref[...]).astype(BF16)
    ncol = w_ref.shape[1]
    for c in range(0, ncol, col_chunk):
        w = min(col_chunk, ncol - c)
        out_ref[:, c:c + w] = _dot(xn, w_ref[:, c:c + w]).astype(out_ref.dtype)
    f_ref[...] = _dot(xn, wf_ref[...])


def norm_proj(h, g, w_main, w_f, *, tm):
    n, d = h.shape
    ncol = w_main.shape[1]
    return pl.pallas_call(
        functools.partial(_norm_proj_kernel, col_chunk=512),
        grid=(n // tm,),
        in_specs=[pl.BlockSpec((tm, d), lambda i: (i, 0)),
                  _resident((1, d)), _resident((d, ncol)), _resident((d, LANES))],
        out_specs=[pl.BlockSpec((tm, ncol), lambda i: (i, 0)),
                   pl.BlockSpec((tm, LANES), lambda i: (i, 0))],
        out_shape=[jax.ShapeDtypeStruct((n, ncol), BF16),
                   jax.ShapeDtypeStruct((n, LANES), F32)],
        compiler_params=_params(("arbitrary",)),
        name="norm_proj",
    )(h, g, w_main, w_f)


def _fox_prep_kernel(f_ref, b_ref, col_ref, row_ref, carry_ref):
    @pl.when(pl.program_id(1) == 0)
    def _():
        carry_ref[...] = jnp.zeros_like(carry_ref)

    x = f_ref[...] + b_ref[...]
    log_f = (jnp.minimum(x, 0.0) - jnp.log1p(jnp.exp(-jnp.abs(x)))) * LOG2_E
    blk = x.shape[0]
    r = lax.broadcasted_iota(jnp.int32, (blk, blk), 0)
    c = lax.broadcasted_iota(jnp.int32, (blk, blk), 1)
    tri = (c <= r).astype(F32)
    cum = jnp.dot(tri, log_f, precision=lax.Precision.HIGHEST,
                  preferred_element_type=F32) + carry_ref[...]
    col_ref[...] = cum
    row_ref[...] = cum.T[:8, :]
    carry_ref[...] = cum[blk - 1:blk, :]


def fox_prep(f, b_pad, *, blk):
    bsz, s, _ = f.shape
    return pl.pallas_call(
        _fox_prep_kernel,
        grid=(bsz, s // blk),
        in_specs=[pl.BlockSpec((None, blk, LANES), lambda b, j: (b, j, 0)),
                  pl.BlockSpec((1, LANES), lambda b, j: (0, 0))],
        out_specs=[pl.BlockSpec((None, blk, LANES), lambda b, j: (b, j, 0)),
                   pl.BlockSpec((None, 8, blk), lambda b, j: (b, 0, j))],
        out_shape=[jax.ShapeDtypeStruct((bsz, s, LANES), F32),
                   jax.ShapeDtypeStruct((bsz, 8, s), F32)],
        scratch_shapes=[pltpu.VMEM((1, LANES), F32)],
        compiler_params=_params(("arbitrary", "arbitrary")),
        name="fox_prep",
    )(f, b_pad)


def _attn_kernel(*refs, mode, tq, out_scale):
    if mode == "diff":
        lam_ref, q_ref, k_ref, vt_ref, gain_ref, o_ref = refs[:6]
        state = (refs[6:9], refs[9:12])
    else:
        q_ref, k_ref, vt_ref, ccol_ref, crow_ref, o_ref = refs[:6]
        state = (refs[6:9], refs[9:12])
        cs_scr = refs[12]
    qi = pl.program_id(2)
    pair = pl.program_id(1)

    q = q_ref[...]
    lo = _lane_masks(q.shape)
    zero = jnp.zeros_like(q)
    q_maps = (jnp.where(lo, q, zero), jnp.where(lo, zero, q))

    for m_ref, l_ref, acc_ref in state:
        m_ref[...] = jnp.full_like(m_ref, NEG_INF)
        l_ref[...] = jnp.zeros_like(l_ref)
        acc_ref[...] = jnp.zeros_like(acc_ref)

    if mode == "fox":
        @pl.when(qi == 0)
        def _():
            ccol = ccol_ref[...]
            lane = lax.broadcasted_iota(jnp.int32, ccol.shape, 1)
            for i in range(2):
                col = jnp.sum(jnp.where(lane == 2 * pair + i, ccol, 0.0), axis=-1, keepdims=True)
                cs_scr[i] = jnp.broadcast_to(col, ccol.shape)

        ct = tuple(crow_ref[pl.ds(2 * pair + i, 1), :] for i in range(2))

    def chunk(start, nk, masked):
        k = k_ref[pl.ds(start, nk), :]
        vt = vt_ref[:, pl.ds(start, nk)]
        for i in range(2):
            m_ref, l_ref, acc_ref = state[i]
            st = _dot_nt(k, q_maps[i])
            if mode == "fox":
                cs = cs_scr[i, pl.ds(start, nk), :]
                st = st + ct[i] - jnp.tile(cs, (1, tq // LANES))
            if masked:
                key = lax.broadcasted_iota(jnp.int32, st.shape, 0)
                qry = lax.broadcasted_iota(jnp.int32, st.shape, 1) + (nk - tq)
                st = jnp.where(key <= qry, st, NEG_INF)
            m_old = m_ref[...]
            m_new = jnp.maximum(m_old, jnp.max(st, axis=0, keepdims=True))
            alpha = jnp.exp2(m_old - m_new)
            p = jnp.exp2(st - m_new)
            l_ref[...] = alpha * l_ref[...] + jnp.sum(p, axis=0, keepdims=True)
            acc_ref[...] = alpha * acc_ref[...] + _dot(vt, p.astype(BF16))
            m_ref[...] = m_new

    def body(j, carry):
        chunk(pl.multiple_of(j * (2 * tq), 2 * tq), 2 * tq, False)
        return carry

    lax.fori_loop(0, qi // 2, body, 0)

    @pl.when(qi % 2 == 1)
    def _():
        chunk(pl.multiple_of((qi - 1) * tq, tq), 2 * tq, True)

    @pl.when(qi % 2 == 0)
    def _():
        chunk(pl.multiple_of(qi * tq, tq), tq, True)

    o1, o2 = (acc_ref[...] * (1.0 / l_ref[...]) for _, l_ref, acc_ref in state)
    if mode == "diff":
        o = (o1 - lam_ref[0] * o2).T
        o = _rms(o, gain_ref[...]) * out_scale
    else:
        chan = lax.broadcasted_iota(jnp.int32, o1.shape, 0)
        o = jnp.where(chan < HEAD_DIM, o1, o2).T
    o_ref[...] = o.astype(o_ref.dtype)


def _attn_scratch(tq):
    per_map = [pltpu.VMEM((1, tq), F32), pltpu.VMEM((1, tq), F32), pltpu.VMEM((LANES, tq), F32)]
    return per_map + per_map


def diff_attention(proj, vt, lam, gain, *, tq, out_scale):
    bsz, s, _ = proj.shape
    kern = functools.partial(_attn_kernel, mode="diff", tq=tq, out_scale=out_scale)
    return pl.pallas_call(
        kern,
        grid=(bsz, DIFF_HEADS, s // tq),
        in_specs=[pl.BlockSpec(memory_space=pltpu.SMEM),
                  pl.BlockSpec((None, tq, LANES), lambda b, h, i: (b, i, QA + h)),
                  pl.BlockSpec((None, s, LANES), lambda b, h, i: (b, 0, KA + h)),
                  pl.BlockSpec((None, LANES, s), lambda b, h, i: (b, h, 0)),
                  pl.BlockSpec((1, LANES), lambda b, h, i: (0, 0))],
        out_specs=pl.BlockSpec((None, tq, LANES), lambda b, h, i: (b, i, h)),
        out_shape=jax.ShapeDtypeStruct((bsz, s, DIFF_HEADS * LANES), BF16),
        scratch_shapes=_attn_scratch(tq),
        compiler_params=_params(("arbitrary", "arbitrary", "arbitrary")),
        name="diff_attention",
    )(lam, proj, proj, vt, gain)


def fox_attention(proj, vt, cum_col, cum_row, *, tq):
    bsz, s, _ = proj.shape
    npair = FOX_HEADS // 2
    kern = functools.partial(_attn_kernel, mode="fox", tq=tq, out_scale=None)
    return pl.pallas_call(
        kern,
        grid=(bsz, npair, s // tq),
        in_specs=[pl.BlockSpec((None, tq, LANES), lambda b, h, i: (b, i, QB + h)),
                  pl.BlockSpec((None, s, LANES), lambda b, h, i: (b, 0, KB + h)),
                  pl.BlockSpec((None, LANES, s), lambda b, h, i: (b, h, 0)),
                  pl.BlockSpec((None, s, LANES), lambda b, h, i: (b, 0, 0)),
                  pl.BlockSpec((None, 8, tq), lambda b, h, i: (b, 0, i))],
        out_specs=pl.BlockSpec((None, tq, LANES), lambda b, h, i: (b, i, h)),
        out_shape=jax.ShapeDtypeStruct((bsz, s, npair * LANES), BF16),
        scratch_shapes=_attn_scratch(tq) + [pltpu.VMEM((2, s, LANES), F32)],
        compiler_params=_params(("arbitrary", "arbitrary", "arbitrary")),
        name="fox_attention",
    )(proj, proj, vt, cum_col, cum_row)


def _retention_tables(s, chunk):
    half = HEAD_DIM // 2
    inv = RET_THETA ** (-np.arange(half, dtype=np.float64) / half)
    ang = np.arange(s, dtype=np.float64)[:, None] * inv[None, :]
    cos = np.tile(np.cos(ang), (1, 4))
    sin = np.tile(np.concatenate([-np.sin(ang), np.sin(ang)], axis=1), (1, 2))
    log_g = np.log1p(-np.exp2(-5.0 - np.arange(RET_HEADS, dtype=np.float64)))
    j = np.arange(chunk, dtype=np.float64)
    diff = j[:, None] - j[None, :]
    inner = np.where(diff[None] >= 0, np.exp(np.where(diff[None] >= 0, diff[None], 0.0)
                                             * log_g[:, None, None]), 0.0)
    q_dec = np.exp((j + 1.0)[None, :] * log_g[:, None])
    k_dec = np.exp((chunk - 1.0 - j)[None, :] * log_g[:, None])
    c_dec = np.exp(chunk * log_g)
    npair = RET_HEADS // 2

    def lanes(t):
        t = t.reshape(npair, 2, chunk)
        return np.repeat(t.transpose(0, 2, 1), HEAD_DIM, axis=2)

    blk = np.kron(np.eye(2), np.ones((HEAD_DIM, HEAD_DIM)))
    c_mat = np.stack([np.kron(np.diag(c_dec[2 * p:2 * p + 2]), np.ones((HEAD_DIM, HEAD_DIM)))
                      for p in range(npair)])
    f = lambda a: jnp.asarray(a, F32)
    return dict(cos=f(cos), sin=f(sin), inner=f(inner.reshape(npair, 2, chunk, chunk)),
                q_dec=f(lanes(q_dec)), k_dec=f(lanes(k_dec)), c_mat=f(c_mat), blk=f(blk))


def _retention_kernel(q_ref, k_ref, v_ref, g_ref, cos_ref, sin_ref, inner_ref, qdec_ref,
                      kdec_ref, cmat_ref, blk_ref, gain_ref, o_ref, state_ref):
    @pl.when(pl.program_id(2) == 0)
    def _():
        state_ref[...] = jnp.zeros_like(state_ref)

    cos = cos_ref[...]
    sin = sin_ref[...]
    lane = lax.broadcasted_iota(jnp.int32, cos.shape, 1)
    first_half = (lane % HEAD_DIM) < (HEAD_DIM // 2)

    def rope(x):
        x = x.astype(F32)
        swapped = jnp.where(first_half, pltpu.roll(x, LANES - HEAD_DIM // 2, 1),
                            pltpu.roll(x, HEAD_DIM // 2, 1))
        return x * cos + swapped * sin

    qr = rope(q_ref[...])
    kr = rope(k_ref[...])
    v = v_ref[...]
    lo = _lane_masks(qr.shape)
    qb = qr.astype(BF16)
    kb = kr.astype(BF16)
    zero = jnp.zeros_like(qb)
    a1 = (_dot_nt(jnp.where(lo, qb, zero), kb) * inner_ref[0]).astype(BF16)
    a2 = (_dot_nt(jnp.where(lo, zero, qb), kb) * inner_ref[1]).astype(BF16)
    state = state_ref[...]
    o = jnp.where(lo, _dot(a1, v), _dot(a2, v))
    o = o + _dot((qr * qdec_ref[...]).astype(BF16), state.astype(BF16))
    kd = (kr * kdec_ref[...]).astype(BF16)
    state_ref[...] = state * cmat_ref[...] + _dot(kd.T, v) * blk_ref[...]

    inv_n = 1.0 / HEAD_DIM
    sum_lo = jnp.sum(jnp.where(lo, o, 0.0), axis=-1, keepdims=True)
    sum_hi = jnp.sum(jnp.where(lo, 0.0, o), axis=-1, keepdims=True)
    d = o - jnp.where(lo, sum_lo, sum_hi) * inv_n
    d2 = d * d
    var_lo = jnp.sum(jnp.where(lo, d2, 0.0), axis=-1, keepdims=True)
    var_hi = jnp.sum(jnp.where(lo, 0.0, d2), axis=-1, keepdims=True)
    y = d * lax.rsqrt(jnp.where(lo, var_lo, var_hi) * inv_n + EPS)
    g = g_ref[...].astype(F32)
    o_ref[...] = (g * _sigmoid(g) * (y * gain_ref[...])).astype(o_ref.dtype)


def retention(proj, gain, *, chunk):
    bsz, s, _ = proj.shape
    npair = RET_HEADS // 2
    t = _retention_tables(s, chunk)

    def act(off):
        return pl.BlockSpec((None, chunk, LANES), lambda b, p, c: (b, c, off + p))

    pos = pl.BlockSpec((chunk, LANES), lambda b, p, c: (c, 0))
    per_pair3 = lambda shape: pl.BlockSpec((None,) + shape, lambda b, p, c: (p,) + (0,) * len(shape))
    return pl.pallas_call(
        _retention_kernel,
        grid=(bsz, npair, s // chunk),
        in_specs=[act(QC), act(KC), act(VC), act(GC), pos, pos,
                  per_pair3((2, chunk, chunk)), per_pair3((chunk, LANES)),
                  per_pair3((chunk, LANES)), per_pair3((LANES, LANES)),
                  pl.BlockSpec((LANES, LANES), lambda b, p, c: (0, 0)),
                  pl.BlockSpec((1, LANES), lambda b, p, c: (0, p))],
        out_specs=pl.BlockSpec((None, chunk, LANES), lambda b, p, c: (b, c, p)),
        out_shape=jax.ShapeDtypeStruct((bsz, s, npair * LANES), BF16),
        scratch_shapes=[pltpu.VMEM((LANES, LANES), F32)],
        compiler_params=_params(("arbitrary", "arbitrary", "arbitrary")),
        name="retention",
    )(proj, proj, proj, proj, t["cos"], t["sin"], t["inner"], t["q_dec"], t["k_dec"],
      t["c_mat"], t["blk"], gain)


def _mix_residual(h_ref, oa_ref, ob_ref, oc_ref, wo_ref):
    wa = oa_ref.shape[1]
    wb = ob_ref.shape[1]
    return (h_ref[...] + _dot(oa_ref[...], wo_ref[0:wa, :])
            + _dot(ob_ref[...], wo_ref[wa:wa + wb, :])
            + _dot(oc_ref[...], wo_ref[wa + wb:, :]))


def _swiglu_accumulate(xn, wg_ref, wu_ref, wd_ref, acc_ref, chunk):
    def body(i, carry):
        c = pl.multiple_of(i * chunk, chunk)
        g = _dot(xn, wg_ref[:, pl.ds(c, chunk)])
        u = _dot(xn, wu_ref[:, pl.ds(c, chunk)])
        a = (g * _sigmoid(g) * u).astype(BF16)
        acc_ref[...] += _dot(a, wd_ref[pl.ds(c, chunk), :])
        return carry

    lax.fori_loop(0, wg_ref.shape[1] // chunk, body, 0)


def _ple(h, p_ref, pn_ref, pg_ref, pp_ref):
    gate = _sigmoid(_dot(_rms(h, pn_ref[...]).astype(BF16), pg_ref[...]))
    return h + gate * _dot(p_ref[...].astype(BF16), pp_ref[...])


def _dense_layer_kernel(h_ref, oa_ref, ob_ref, oc_ref, p_ref, wo_ref, gn_ref, wg_ref, wu_ref,
                        wd_ref, pn_ref, pg_ref, pp_ref, out_ref, acc_ref, *, ff_chunk):
    h1 = _mix_residual(h_ref, oa_ref, ob_ref, oc_ref, wo_ref)
    xn = _rms(h1, gn_ref[...]).astype(BF16)
    acc_ref[...] = h1
    _swiglu_accumulate(xn, wg_ref, wu_ref, wd_ref, acc_ref, ff_chunk)
    out_ref[...] = _ple(acc_ref[...], p_ref, pn_ref, pg_ref, pp_ref)


def dense_layer(h, oa, ob, oc, p, wo, gn, wg, wu, wd, pn, pg, pp, *, tm, ff_chunk):
    n, d = h.shape
    row = lambda w: pl.BlockSpec((tm, w), lambda i: (i, 0))
    weights = (wo, gn, wg, wu, wd, pn, pg, pp)
    return pl.pallas_call(
        functools.partial(_dense_layer_kernel, ff_chunk=ff_chunk),
        grid=(n // tm,),
        in_specs=[row(d), row(oa.shape[1]), row(ob.shape[1]), row(oc.shape[1]), row(p.shape[1])]
                 + [_resident(w.shape) for w in weights],
        out_specs=row(d),
        out_shape=jax.ShapeDtypeStruct((n, d), F32),
        scratch_shapes=[pltpu.VMEM((tm, d), F32)],
        compiler_params=_params(("arbitrary",)),
        name="dense_layer",
    )(h, oa, ob, oc, p, *weights)


def _pack_halves(x):
    half = x.shape[1] // 2
    lo = pltpu.bitcast(x[:, :half].astype(BF16).astype(F32), jnp.uint32)
    hi = pltpu.bitcast(x[:, half:].astype(BF16).astype(F32), jnp.uint32)
    return (lo >> 16) | (hi & jnp.uint32(0xFFFF0000))


def _unpack_halves(w):
    lo = pltpu.bitcast(w << 16, F32).astype(BF16)
    hi = pltpu.bitcast(w & jnp.uint32(0xFFFF0000), F32).astype(BF16)
    return lo, hi


def _router_kernel(h_ref, oa_ref, ob_ref, oc_ref, wo_ref, gn_ref, wr_ref,
                   h1_ref, xpk_ref, meta_ref, cnt_ref):
    @pl.when(pl.program_id(0) == 0)
    def _():
        cnt_ref[...] = jnp.zeros_like(cnt_ref)

    h1 = _mix_residual(h_ref, oa_ref, ob_ref, oc_ref, wo_ref)
    h1_ref[...] = h1
    xn = _rms(h1, gn_ref[...])
    xpk_ref[...] = _pack_halves(xn)

    logits = jnp.dot(xn, wr_ref[...], precision=lax.Precision.HIGHEST,
                     preferred_element_type=F32)
    tm = logits.shape[0]
    lane = lax.broadcasted_iota(jnp.int32, logits.shape, 1)
    logits = jnp.where(lane < N_EXPERTS, logits, -jnp.inf)

    def top(vals):
        best = jnp.max(vals, axis=-1, keepdims=True)
        idx = jnp.min(jnp.where(vals == best, lane, LANES), axis=-1, keepdims=True)
        return best, idx

    v1, i1 = top(logits)
    v2, i2 = top(jnp.where(lane == i1, -jnp.inf, logits))
    e = jnp.exp(v2 - v1)
    g1 = 1.0 / (1.0 + e)
    g2 = e / (1.0 + e)

    oh1 = (lane == i1).astype(F32)
    oh2 = (lane == i2).astype(F32)
    both = oh1 + oh2
    r = lax.broadcasted_iota(jnp.int32, (tm, tm), 0)
    c = lax.broadcasted_iota(jnp.int32, (tm, tm), 1)
    before = _dot((c < r).astype(BF16), both.astype(BF16)) + cnt_ref[...]
    rank1 = jnp.sum(before * oh1, axis=-1, keepdims=True)
    rank2 = jnp.sum(before * oh2, axis=-1, keepdims=True)
    cnt_ref[...] += jnp.sum(both, axis=0, keepdims=True)

    meta = jnp.zeros(logits.shape, F32)
    for k, val in enumerate((i1.astype(F32), i2.astype(F32), rank1, rank2, g1, g2)):
        meta = jnp.where(lane == k, val, meta)
    meta_ref[...] = meta


def route_tokens(h, oa, ob, oc, wo, gn, wr, *, tm):
    n, d = h.shape
    row = lambda w: pl.BlockSpec((tm, w), lambda i: (i, 0))
    weights = (wo, gn, wr)
    return pl.pallas_call(
        _router_kernel,
        grid=(n // tm,),
        in_specs=[row(d), row(oa.shape[1]), row(ob.shape[1]), row(oc.shape[1])]
                 + [_resident(w.shape) for w in weights],
        out_specs=[row(d), row(d // 2), row(LANES), pl.BlockSpec((1, LANES), lambda i: (0, 0))],
        out_shape=[jax.ShapeDtypeStruct((n, d), F32),
                   jax.ShapeDtypeStruct((n, d // 2), jnp.uint32),
                   jax.ShapeDtypeStruct((n, LANES), F32),
                   jax.ShapeDtypeStruct((1, LANES), F32)],
        compiler_params=_params(("arbitrary",)),
        name="router",
    )(h, oa, ob, oc, *weights)


def _dispatch_kernel(pos_ref, x_ref, zero_ref, xs_ref, sem):
    del zero_ref
    tm = x_ref.shape[0]

    def row_copy(r, k):
        return pltpu.make_async_copy(x_ref.at[pl.ds(r, 1), :],
                                     xs_ref.at[pl.ds(pos_ref[0, k, r], 1), :], sem)

    _start_rows(tm, row_copy)
    for k in range(2):
        pltpu.make_async_copy(x_ref, xs_ref.at[pl.ds(0, tm), :], sem).wait()


def _start_rows(n_rows, row_copy):
    def body(g, carry):
        base = pl.multiple_of(g * ROW_UNROLL, ROW_UNROLL)
        for j in range(ROW_UNROLL):
            row_copy(base + j, 0).start(priority=0)
            row_copy(base + j, 1).start(priority=1)
        return carry

    lax.fori_loop(0, n_rows // ROW_UNROLL, body, 0)


def dispatch(xpk, pos, n_slots, *, tm):
    n, w = xpk.shape
    zeros = jnp.zeros((n_slots, w), jnp.uint32)
    return pl.pallas_call(
        _dispatch_kernel,
        grid=(n // tm,),
        in_specs=[pl.BlockSpec((1, 2, tm), lambda i: (i, 0, 0), memory_space=pltpu.SMEM),
                  pl.BlockSpec((tm, w), lambda i: (i, 0)),
                  pl.BlockSpec(memory_space=pl.ANY)],
        out_specs=pl.BlockSpec(memory_space=pl.ANY),
        out_shape=jax.ShapeDtypeStruct((n_slots, w), jnp.uint32),
        scratch_shapes=[pltpu.SemaphoreType.DMA(())],
        input_output_aliases={2: 0},
        compiler_params=_params(("arbitrary",)),
        name="moe_dispatch",
    )(pos, xpk, zeros)


def _experts_kernel(te_ref, nt_ref, x_ref, wg_ref, wu_ref, wd_ref, o_ref, x_scr, *, sub_chunk):
    del te_ref
    t = pl.program_id(0)
    c = pl.program_id(1)

    @pl.when(c == 0)
    def _():
        o_ref[...] = jnp.zeros_like(o_ref)

    @pl.when(t < nt_ref[0])
    def _():
        @pl.when(c == 0)
        def _():
            lo, hi = _unpack_halves(x_ref[...])
            half = lo.shape[1]
            x_scr[:, :half] = lo
            x_scr[:, half:] = hi

        _swiglu_accumulate(x_scr[...], wg_ref, wu_ref, wd_ref, o_ref, sub_chunk)


def experts(xs, tile_expert, n_tiles, wg, wu, wd, *, tile, ff_chunk, sub_chunk):
    n_slots, w = xs.shape
    d = 2 * w
    d_ff = wg.shape[2]
    nc = d_ff // ff_chunk
    grid_tiles = n_slots // tile

    def tile_idx(t, c, te, nt):
        return (jnp.maximum(jnp.minimum(t, nt[0] - 1), 0), 0)

    def chunk_idx(t, c, nt):
        return jnp.where(t < nt[0], c, nc - 1)

    grid_spec = pltpu.PrefetchScalarGridSpec(
        num_scalar_prefetch=2,
        grid=(grid_tiles, nc),
        in_specs=[pl.BlockSpec((tile, w), tile_idx),
                  pl.BlockSpec((None, d, ff_chunk), lambda t, c, te, nt: (te[t], 0, chunk_idx(t, c, nt))),
                  pl.BlockSpec((None, d, ff_chunk), lambda t, c, te, nt: (te[t], 0, chunk_idx(t, c, nt))),
                  pl.BlockSpec((None, ff_chunk, d), lambda t, c, te, nt: (te[t], chunk_idx(t, c, nt), 0))],
        out_specs=pl.BlockSpec((tile, d), lambda t, c, te, nt: (t, 0)),
        scratch_shapes=[pltpu.VMEM((tile, d), BF16)],
    )
    return pl.pallas_call(
        functools.partial(_experts_kernel, sub_chunk=sub_chunk),
        grid_spec=grid_spec,
        out_shape=jax.ShapeDtypeStruct((n_slots, d), F32),
        compiler_params=_params(("arbitrary", "arbitrary")),
        name="moe_experts",
    )(tile_expert, n_tiles, xs, wg, wu, wd)


def _combine_kernel(pos_ref, h_ref, meta_ref, p_ref, pn_ref, pg_ref, pp_ref, fn_ref, ys_ref,
                    out_ref, y1_scr, y2_scr, sem):
    tm = h_ref.shape[0]
    bufs = (y1_scr, y2_scr)

    def row_copy(r, k):
        return pltpu.make_async_copy(ys_ref.at[pl.ds(pos_ref[0, k, r], 1), :],
                                     bufs[k].at[pl.ds(r, 1), :], sem)

    _start_rows(tm, row_copy)
    for buf in bufs:
        pltpu.make_async_copy(ys_ref.at[pl.ds(0, tm), :], buf, sem).wait()

    meta = meta_ref[...]
    lane = lax.broadcasted_iota(jnp.int32, meta.shape, 1)
    g1 = jnp.sum(jnp.where(lane == 4, meta, 0.0), axis=-1, keepdims=True)
    g2 = jnp.sum(jnp.where(lane == 5, meta, 0.0), axis=-1, keepdims=True)
    h2 = h_ref[...] + g1 * y1_scr[...] + g2 * y2_scr[...]
    h3 = _ple(h2, p_ref, pn_ref, pg_ref, pp_ref)
    out_ref[...] = _rms(h3, fn_ref[...])


def combine(pos, h1, meta, p, pn, pg, pp, fn, ys, *, tm):
    n, d = h1.shape
    row = lambda w: pl.BlockSpec((tm, w), lambda i: (i, 0))
    weights = (pn, pg, pp, fn)
    return pl.pallas_call(
        _combine_kernel,
        grid=(n // tm,),
        in_specs=[pl.BlockSpec((1, 2, tm), lambda i: (i, 0, 0), memory_space=pltpu.SMEM),
                  row(d), row(LANES), row(p.shape[1])]
                 + [_resident(w.shape) for w in weights]
                 + [pl.BlockSpec(memory_space=pl.ANY)],
        out_specs=row(d),
        out_shape=jax.ShapeDtypeStruct((n, d), F32),
        scratch_shapes=[pltpu.VMEM((tm, d), F32), pltpu.VMEM((tm, d), F32),
                        pltpu.SemaphoreType.DMA(())],
        compiler_params=_params(("arbitrary",)),
        name="moe_combine",
    )(pos, h1, meta, p, *weights, ys)


def _prep_w_in(w):
    scale = HEAD_DIM ** -0.5
    qa, ka, va = w[:, 0:512] * (scale * LOG2_E), w[:, 512:1024], w[:, 1024:1536]
    qb, kb, vb = w[:, 1536:1792] * (scale * LOG2_E), w[:, 1792:2048], w[:, 2048:2304]
    fb = w[:, 2304:2308]
    qc, kc, vc, gc = w[:, 2308:2564], w[:, 2564:2820] * scale, w[:, 2820:3076], w[:, 3076:3332]
    main = jnp.concatenate([qa, ka, va, qb, kb, vb, qc, kc, vc, gc], axis=1).astype(BF16)
    f = jnp.pad(fb, ((0, 0), (0, LANES - fb.shape[1]))).astype(BF16)
    return main, f


def _pad_lanes(v):
    return jnp.pad(v, (0, LANES - v.shape[0]))[None, :]


class Tiles(NamedTuple):
    rows: int
    attn: int
    scan: int
    ret_chunk: int
    ff_chunk: int
    moe_tile: int
    moe_ff_chunk: int
    moe_sub_chunk: int


TILES = Tiles(rows=512, attn=512, scan=512, ret_chunk=256, ff_chunk=256,
              moe_tile=512, moe_ff_chunk=1792, moe_sub_chunk=256)


def _mixer(h, b, s, norm_g, w_in, b_forget, lq1, lk1, lq2, lk2, subln, ret_gn, layer_idx, t):
    w_main, w_f = _prep_w_in(w_in)
    proj, f = norm_proj(h, norm_g[None, :], w_main, w_f, tm=t.rows)
    proj = proj.reshape(b, s, -1)
    cum_col, cum_row = fox_prep(f.reshape(b, s, LANES), _pad_lanes(b_forget), blk=t.scan)
    lam_init = 0.8 - 0.6 * math.exp(-0.3 * layer_idx)
    lam = jnp.exp(jnp.sum(lq1 * lk1)) - jnp.exp(jnp.sum(lq2 * lk2)) + lam_init
    vt_a = proj[:, :, VA * LANES:QB * LANES].transpose(0, 2, 1)
    vt_b = proj[:, :, VB * LANES:QC * LANES].transpose(0, 2, 1)
    oa = diff_attention(proj, vt_a, lam.reshape(1), subln[None, :], tq=t.attn,
                        out_scale=1.0 - lam_init)
    ob = fox_attention(proj, vt_b, cum_col, cum_row, tq=t.attn)
    oc = retention(proj, ret_gn[None, :], chunk=t.ret_chunk)
    n = b * s
    return oa.reshape(n, -1), ob.reshape(n, -1), oc.reshape(n, -1)


def _route_plan(meta, counts, tile, n_slots, tm):
    n = meta.shape[0]
    idx = meta[:, 0:2].astype(jnp.int32)
    rank = meta[:, 2:4].astype(jnp.int32)
    cnt = counts[0, :N_EXPERTS].astype(jnp.int32)
    padded = ((cnt + tile - 1) // tile) * tile
    ends = jnp.cumsum(padded)
    starts = ends - padded
    pos = starts[idx] + rank
    n_tiles = ends[-1] // tile
    tiles = jnp.arange(n_slots // tile, dtype=jnp.int32)
    tile_expert = jnp.sum(tiles[:, None] >= (ends // tile)[None, :], axis=1)
    last = jnp.sum(n_tiles - 1 >= ends // tile)
    tile_expert = jnp.where(tiles < n_tiles, tile_expert, last).astype(jnp.int32)
    tile_expert = jnp.clip(tile_expert, 0, N_EXPERTS - 1)
    pos = jnp.clip(pos, 0, n_slots - 1)
    pos = pos.reshape(n // tm, tm, 2).transpose(0, 2, 1)
    return pos, tile_expert, n_tiles.reshape(1).astype(jnp.int32)


def forward(x, p, norm_mix, w_in, b_forget, lambda_q1, lambda_k1, lambda_q2, lambda_k2,
            diff_subln, ret_gn, w_out, norm_ffn, dense_w_gate, dense_w_up, dense_w_down,
            router, moe_w_gate, moe_w_up, moe_w_down, ple_norm, ple_gate, ple_proj, final_norm,
            t=TILES):
    assert w_in.shape[0] == 2, "two layers: dense SwiGLU, then routed experts"
    b, s, d = x.shape
    n = b * s
    h = x.reshape(n, d)
    pf = p.reshape(p.shape[0], n, p.shape[-1])
    bf = lambda a: a.astype(BF16)
    row = lambda a: a[None, :]

    oa, ob, oc = _mixer(h, b, s, norm_mix[0], w_in[0], b_forget[0], lambda_q1[0], lambda_k1[0],
                        lambda_q2[0], lambda_k2[0], diff_subln[0], ret_gn[0], 0, t)
    h = dense_layer(h, oa, ob, oc, pf[0], bf(w_out[0]), row(norm_ffn[0]), bf(dense_w_gate[0]),
                    bf(dense_w_up[0]), bf(dense_w_down[0]), row(ple_norm[0]), bf(ple_gate[0]),
                    bf(ple_proj[0]), tm=t.rows, ff_chunk=t.ff_chunk)

    oa, ob, oc = _mixer(h, b, s, norm_mix[1], w_in[1], b_forget[1], lambda_q1[1], lambda_k1[1],
                        lambda_q2[1], lambda_k2[1], diff_subln[1], ret_gn[1], 1, t)
    wr = jnp.pad(router[0], ((0, 0), (0, LANES - N_EXPERTS)))
    h1, xpk, meta, counts = route_tokens(h, oa, ob, oc, bf(w_out[1]), row(norm_ffn[1]), wr,
                                         tm=t.rows)
    n_slots = 2 * n + N_EXPERTS * t.moe_tile
    pos, tile_expert, n_tiles = _route_plan(meta, counts, t.moe_tile, n_slots, t.rows)
    xs = dispatch(xpk, pos, n_slots, tm=t.rows)
    ys = experts(xs, tile_expert, n_tiles, bf(moe_w_gate[0]), bf(moe_w_up[0]), bf(moe_w_down[0]),
                 tile=t.moe_tile, ff_chunk=t.moe_ff_chunk, sub_chunk=t.moe_sub_chunk)
    out = combine(pos, h1, meta, pf[1], row(ple_norm[1]), bf(ple_gate[1]), bf(ple_proj[1]),
                  row(final_norm), ys, tm=t.rows)
    return out.reshape(b, s, d)


def kernel(x, p, norm_mix, w_in, b_forget, lambda_q1, lambda_k1, lambda_q2, lambda_k2,
           diff_subln, ret_gn, w_out, norm_ffn, dense_w_gate, dense_w_up, dense_w_down,
           router, moe_w_gate, moe_w_up, moe_w_down, ple_norm, ple_gate, ple_proj, final_norm):
    return forward(x, p, norm_mix, w_in, b_forget, lambda_q1, lambda_k1, lambda_q2, lambda_k2,
                   diff_subln, ret_gn, w_out, norm_ffn, dense_w_gate, dense_w_up, dense_w_down,
                   router, moe_w_gate, moe_w_up, moe_w_down, ple_norm, ple_gate, ple_proj,
                   final_norm)
```

```python
import functools
import math
from typing import NamedTuple

import numpy as np
import jax
import jax.numpy as jnp
from jax import lax
from jax.experimental import pallas as pl
from jax.experimental.pallas import tpu as pltpu

F32 = jnp.float32
BF16 = jnp.bfloat16

LANES = 128
ROW_UNROLL = 8
VMEM_LIMIT_BYTES = 56 * 1024 * 1024

HEAD_DIM = 64
DIFF_HEADS = 4
FOX_HEADS = 4
RET_HEADS = 4
RET_THETA = 10000.0
N_EXPERTS = 8
EPS = 1e-6
NEG_INF = -1e30
LOG2_E = math.log2(math.e)

QA, KA, VA = 0, 4, 8
QB, KB, VB = 12, 14, 16
QC, KC, VC, GC = 18, 20, 22, 24
N_PROJ_BLOCKS = 26


def _params(semantics):
    return pltpu.CompilerParams(dimension_semantics=semantics,
                                vmem_limit_bytes=VMEM_LIMIT_BYTES)


def _resident(shape):
    nd = len(shape)
    return pl.BlockSpec(shape, lambda *_: (0,) * nd, pipeline_mode=pl.Buffered(1))


def _rms(x, g):
    return x * lax.rsqrt(jnp.mean(x * x, axis=-1, keepdims=True) + EPS) * g


def _sigmoid(x):
    return 1.0 / (1.0 + jnp.exp(-x))


def _dot(a, b):
    return jnp.dot(a, b, preferred_element_type=F32)


def _dot_nt(a, b):
    return lax.dot_general(a, b, (((1,), (1,)), ((), ())), preferred_element_type=F32)


def _lane_masks(shape):
    lane = lax.broadcasted_iota(jnp.int32, shape, len(shape) - 1)
    return lane < HEAD_DIM


def _norm_proj_kernel(h_ref, g_ref, w_ref, wf_ref, out_ref, f_ref, *, col_chunk):
    xn = _rms(h_ref[...], g_ref[...]).astype(BF16)
    ncol = w_ref.shape[1]
    for c in range(0, ncol, col_chunk):
        w = min(col_chunk, ncol - c)
        out_ref[:, c:c + w] = _dot(xn, w_ref[:, c:c + w]).astype(out_ref.dtype)
    f_ref[...] = _dot(xn, wf_ref[...])


def norm_proj(h, g, w_main, w_f, *, tm):
    n, d = h.shape
    ncol = w_main.shape[1]
    return pl.pallas_call(
        functools.partial(_norm_proj_kernel, col_chunk=512),
        grid=(n // tm,),
        in_specs=[pl.BlockSpec((tm, d), lambda i: (i, 0)),
                  _resident((1, d)), _resident((d, ncol)), _resident((d, LANES))],
        out_specs=[pl.BlockSpec((tm, ncol), lambda i: (i, 0)),
                   pl.BlockSpec((tm, LANES), lambda i: (i, 0))],
        out_shape=[jax.ShapeDtypeStruct((n, ncol), BF16),
                   jax.ShapeDtypeStruct((n, LANES), F32)],
        compiler_params=_params(("arbitrary",)),
        name="norm_proj",
    )(h, g, w_main, w_f)


def _fox_prep_kernel(f_ref, b_ref, col_ref, row_ref, carry_ref):
    @pl.when(pl.program_id(1) == 0)
    def _():
        carry_ref[...] = jnp.zeros_like(carry_ref)

    x = f_ref[...] + b_ref[...]
    log_f = (jnp.minimum(x, 0.0) - jnp.log1p(jnp.exp(-jnp.abs(x)))) * LOG2_E
    blk = x.shape[0]
    r = lax.broadcasted_iota(jnp.int32, (blk, blk), 0)
    c = lax.broadcasted_iota(jnp.int32, (blk, blk), 1)
    tri = (c <= r).astype(F32)
    cum = jnp.dot(tri, log_f, precision=lax.Precision.HIGHEST,
                  preferred_element_type=F32) + carry_ref[...]
    col_ref[...] = cum
    row_ref[...] = cum.T[:8, :]
    carry_ref[...] = cum[blk - 1:blk, :]


def fox_prep(f, b_pad, *, blk):
    bsz, s, _ = f.shape
    return pl.pallas_call(
        _fox_prep_kernel,
        grid=(bsz, s // blk),
        in_specs=[pl.BlockSpec((None, blk, LANES), lambda b, j: (b, j, 0)),
                  pl.BlockSpec((1, LANES), lambda b, j: (0, 0))],
        out_specs=[pl.BlockSpec((None, blk, LANES), lambda b, j: (b, j, 0)),
                   pl.BlockSpec((None, 8, blk), lambda b, j: (b, 0, j))],
        out_shape=[jax.ShapeDtypeStruct((bsz, s, LANES), F32),
                   jax.ShapeDtypeStruct((bsz, 8, s), F32)],
        scratch_shapes=[pltpu.VMEM((1, LANES), F32)],
        compiler_params=_params(("arbitrary", "arbitrary")),
        name="fox_prep",
    )(f, b_pad)


def _attn_kernel(*refs, mode, tq, out_scale):
    if mode == "diff":
        lam_ref, q_ref, k_ref, vt_ref, gain_ref, o_ref = refs[:6]
        state = (refs[6:9], refs[9:12])
    else:
        q_ref, k_ref, vt_ref, ccol_ref, crow_ref, o_ref = refs[:6]
        state = (refs[6:9], refs[9:12])
        cs_scr = refs[12]
    qi = pl.program_id(2)
    pair = pl.program_id(1)

    q = q_ref[...]
    lo = _lane_masks(q.shape)
    zero = jnp.zeros_like(q)
    q_maps = (jnp.where(lo, q, zero), jnp.where(lo, zero, q))

    for m_ref, l_ref, acc_ref in state:
        m_ref[...] = jnp.full_like(m_ref, NEG_INF)
        l_ref[...] = jnp.zeros_like(l_ref)
        acc_ref[...] = jnp.zeros_like(acc_ref)

    if mode == "fox":
        @pl.when(qi == 0)
        def _():
            ccol = ccol_ref[...]
            lane = lax.broadcasted_iota(jnp.int32, ccol.shape, 1)
            for i in range(2):
                col = jnp.sum(jnp.where(lane == 2 * pair + i, ccol, 0.0), axis=-1, keepdims=True)
                cs_scr[i] = jnp.broadcast_to(col, ccol.shape)

        ct = tuple(crow_ref[pl.ds(2 * pair + i, 1), :] for i in range(2))

    def chunk(start, nk, masked):
        k = k_ref[pl.ds(start, nk), :]
        vt = vt_ref[:, pl.ds(start, nk)]
        for i in range(2):
            m_ref, l_ref, acc_ref = state[i]
            st = _dot_nt(k, q_maps[i])
            if mode == "fox":
                cs = cs_scr[i, pl.ds(start, nk), :]
                st = st + ct[i] - jnp.tile(cs, (1, tq // LANES))
            if masked:
                key = lax.broadcasted_iota(jnp.int32, st.shape, 0)
                qry = lax.broadcasted_iota(jnp.int32, st.shape, 1) + (nk - tq)
                st = jnp.where(key <= qry, st, NEG_INF)
            m_old = m_ref[...]
            m_new = jnp.maximum(m_old, jnp.max(st, axis=0, keepdims=True))
            alpha = jnp.exp2(m_old - m_new)
            p = jnp.exp2(st - m_new)
            l_ref[...] = alpha * l_ref[...] + jnp.sum(p, axis=0, keepdims=True)
            acc_ref[...] = alpha * acc_ref[...] + _dot(vt, p.astype(BF16))
            m_ref[...] = m_new

    def body(j, carry):
        chunk(pl.multiple_of(j * (2 * tq), 2 * tq), 2 * tq, False)
        return carry

    lax.fori_loop(0, qi // 2, body, 0)

    @pl.when(qi % 2 == 1)
    def _():
        chunk(pl.multiple_of((qi - 1) * tq, tq), 2 * tq, True)

    @pl.when(qi % 2 == 0)
    def _():
        chunk(pl.multiple_of(qi * tq, tq), tq, True)

    o1, o2 = (acc_ref[...] * (1.0 / l_ref[...]) for _, l_ref, acc_ref in state)
    if mode == "diff":
        o = (o1 - lam_ref[0] * o2).T
        o = _rms(o, gain_ref[...]) * out_scale
    else:
        chan = lax.broadcasted_iota(jnp.int32, o1.shape, 0)
        o = jnp.where(chan < HEAD_DIM, o1, o2).T
    o_ref[...] = o.astype(o_ref.dtype)


def _attn_scratch(tq):
    per_map = [pltpu.VMEM((1, tq), F32), pltpu.VMEM((1, tq), F32), pltpu.VMEM((LANES, tq), F32)]
    return per_map + per_map


def diff_attention(proj, vt, lam, gain, *, tq, out_scale):
    bsz, s, _ = proj.shape
    kern = functools.partial(_attn_kernel, mode="diff", tq=tq, out_scale=out_scale)
    return pl.pallas_call(
        kern,
        grid=(bsz, DIFF_HEADS, s // tq),
        in_specs=[pl.BlockSpec(memory_space=pltpu.SMEM),
                  pl.BlockSpec((None, tq, LANES), lambda b, h, i: (b, i, QA + h)),
                  pl.BlockSpec((None, s, LANES), lambda b, h, i: (b, 0, KA + h)),
                  pl.BlockSpec((None, LANES, s), lambda b, h, i: (b, h, 0)),
                  pl.BlockSpec((1, LANES), lambda b, h, i: (0, 0))],
        out_specs=pl.BlockSpec((None, tq, LANES), lambda b, h, i: (b, i, h)),
        out_shape=jax.ShapeDtypeStruct((bsz, s, DIFF_HEADS * LANES), BF16),
        scratch_shapes=_attn_scratch(tq),
        compiler_params=_params(("arbitrary", "arbitrary", "arbitrary")),
        name="diff_attention",
    )(lam, proj, proj, vt, gain)


def fox_attention(proj, vt, cum_col, cum_row, *, tq):
    bsz, s, _ = proj.shape
    npair = FOX_HEADS // 2
    kern = functools.partial(_attn_kernel, mode="fox", tq=tq, out_scale=None)
    return pl.pallas_call(
        kern,
        grid=(bsz, npair, s // tq),
        in_specs=[pl.BlockSpec((None, tq, LANES), lambda b, h, i: (b, i, QB + h)),
                  pl.BlockSpec((None, s, LANES), lambda b, h, i: (b, 0, KB + h)),
                  pl.BlockSpec((None, LANES, s), lambda b, h, i: (b, h, 0)),
                  pl.BlockSpec((None, s, LANES), lambda b, h, i: (b, 0, 0)),
                  pl.BlockSpec((None, 8, tq), lambda b, h, i: (b, 0, i))],
        out_specs=pl.BlockSpec((None, tq, LANES), lambda b, h, i: (b, i, h)),
        out_shape=jax.ShapeDtypeStruct((bsz, s, npair * LANES), BF16),
        scratch_shapes=_attn_scratch(tq) + [pltpu.VMEM((2, s, LANES), F32)],
        compiler_params=_params(("arbitrary", "arbitrary", "arbitrary")),
        name="fox_attention",
    )(proj, proj, vt, cum_col, cum_row)


def _retention_tables(s, chunk):
    half = HEAD_DIM // 2
    inv = RET_THETA ** (-np.arange(half, dtype=np.float64) / half)
    ang = np.arange(s, dtype=np.float64)[:, None] * inv[None, :]
    cos = np.tile(np.cos(ang), (1, 4))
    sin = np.tile(np.concatenate([-np.sin(ang), np.sin(ang)], axis=1), (1, 2))
    log_g = np.log1p(-np.exp2(-5.0 - np.arange(RET_HEADS, dtype=np.float64)))
    j = np.arange(chunk, dtype=np.float64)
    diff = j[:, None] - j[None, :]
    inner = np.where(diff[None] >= 0, np.exp(np.where(diff[None] >= 0, diff[None], 0.0)
                                             * log_g[:, None, None]), 0.0)
    q_dec = np.exp((j + 1.0)[None, :] * log_g[:, None])
    k_dec = np.exp((chunk - 1.0 - j)[None, :] * log_g[:, None])
    c_dec = np.exp(chunk * log_g)
    npair = RET_HEADS // 2

    def lanes(t):
        t = t.reshape(npair, 2, chunk)
        return np.repeat(t.transpose(0, 2, 1), HEAD_DIM, axis=2)

    blk = np.kron(np.eye(2), np.ones((HEAD_DIM, HEAD_DIM)))
    c_mat = np.stack([np.kron(np.diag(c_dec[2 * p:2 * p + 2]), np.ones((HEAD_DIM, HEAD_DIM)))
                      for p in range(npair)])
    f = lambda a: jnp.asarray(a, F32)
    return dict(cos=f(cos), sin=f(sin), inner=f(inner.reshape(npair, 2, chunk, chunk)),
                q_dec=f(lanes(q_dec)), k_dec=f(lanes(k_dec)), c_mat=f(c_mat), blk=f(blk))


def _retention_kernel(q_ref, k_ref, v_ref, g_ref, cos_ref, sin_ref, inner_ref, qdec_ref,
                      kdec_ref, cmat_ref, blk_ref, gain_ref, o_ref, state_ref):
    @pl.when(pl.program_id(1) == 0)
    def _():
        state_ref[...] = jnp.zeros_like(state_ref)

    cos = cos_ref[...]
    sin = sin_ref[...]
    lane = lax.broadcasted_iota(jnp.int32, cos.shape, 1)
    first_half = (lane % HEAD_DIM) < (HEAD_DIM // 2)
    lo = lane < HEAD_DIM

    def rope(x):
        x = x.astype(F32)
        swapped = jnp.where(first_half, pltpu.roll(x, LANES - HEAD_DIM // 2, 1),
                            pltpu.roll(x, HEAD_DIM // 2, 1))
        return x * cos + swapped * sin

    for p in range(RET_HEADS // 2):
        cols = slice(p * LANES, (p + 1) * LANES)
        qr = rope(q_ref[:, cols])
        kr = rope(k_ref[:, cols])
        v = v_ref[:, cols]
        qb = qr.astype(BF16)
        kb = kr.astype(BF16)
        zero = jnp.zeros_like(qb)
        a1 = (_dot_nt(jnp.where(lo, qb, zero), kb) * inner_ref[p, 0]).astype(BF16)
        a2 = (_dot_nt(jnp.where(lo, zero, qb), kb) * inner_ref[p, 1]).astype(BF16)
        state = state_ref[p]
        o = jnp.where(lo, _dot(a1, v), _dot(a2, v))
        o = o + _dot((qr * qdec_ref[p]).astype(BF16), state.astype(BF16))
        kd = (kr * kdec_ref[p]).astype(BF16)
        state_ref[p] = state * cmat_ref[p] + _dot(kd.T, v) * blk_ref[...]

        inv_n = 1.0 / HEAD_DIM
        sum_lo = jnp.sum(jnp.where(lo, o, 0.0), axis=-1, keepdims=True)
        sum_hi = jnp.sum(jnp.where(lo, 0.0, o), axis=-1, keepdims=True)
        d = o - jnp.where(lo, sum_lo, sum_hi) * inv_n
        d2 = d * d
        var_lo = jnp.sum(jnp.where(lo, d2, 0.0), axis=-1, keepdims=True)
        var_hi = jnp.sum(jnp.where(lo, 0.0, d2), axis=-1, keepdims=True)
        y = d * lax.rsqrt(jnp.where(lo, var_lo, var_hi) * inv_n + EPS)
        g = g_ref[:, cols].astype(F32)
        o_ref[:, cols] = (g * _sigmoid(g) * (y * gain_ref[:, cols])).astype(o_ref.dtype)


def retention(proj, gain, *, chunk):
    bsz, s, _ = proj.shape
    npair = RET_HEADS // 2
    t = _retention_tables(s, chunk)

    width = npair * LANES

    def act(off):
        return pl.BlockSpec((None, chunk, width), lambda b, c: (b, c, off // npair))

    pos = pl.BlockSpec((chunk, LANES), lambda b, c: (c, 0))
    return pl.pallas_call(
        _retention_kernel,
        grid=(bsz, s // chunk),
        in_specs=[act(QC), act(KC), act(VC), act(GC), pos, pos,
                  _resident((npair, 2, chunk, chunk)), _resident((npair, chunk, LANES)),
                  _resident((npair, chunk, LANES)), _resident((npair, LANES, LANES)),
                  _resident((LANES, LANES)), _resident((1, width))],
        out_specs=pl.BlockSpec((None, chunk, width), lambda b, c: (b, c, 0)),
        out_shape=jax.ShapeDtypeStruct((bsz, s, width), BF16),
        scratch_shapes=[pltpu.VMEM((npair, LANES, LANES), F32)],
        compiler_params=_params(("arbitrary", "arbitrary")),
        name="retention",
    )(proj, proj, proj, proj, t["cos"], t["sin"], t["inner"], t["q_dec"], t["k_dec"],
      t["c_mat"], t["blk"], gain)


def _mix_residual(h_ref, oa_ref, ob_ref, oc_ref, wo_ref):
    wa = oa_ref.shape[1]
    wb = ob_ref.shape[1]
    return (h_ref[...] + _dot(oa_ref[...], wo_ref[0:wa, :])
            + _dot(ob_ref[...], wo_ref[wa:wa + wb, :])
            + _dot(oc_ref[...], wo_ref[wa + wb:, :]))


def _swiglu_accumulate(xn, wg_ref, wu_ref, wd_ref, acc_ref, chunk):
    def body(i, carry):
        c = pl.multiple_of(i * chunk, chunk)
        g = _dot(xn, wg_ref[:, pl.ds(c, chunk)])
        u = _dot(xn, wu_ref[:, pl.ds(c, chunk)])
        a = (g * _sigmoid(g) * u).astype(BF16)
        acc_ref[...] += _dot(a, wd_ref[pl.ds(c, chunk), :])
        return carry

    lax.fori_loop(0, wg_ref.shape[1] // chunk, body, 0)


def _ple(h, p_ref, pn_ref, pg_ref, pp_ref):
    gate = _sigmoid(_dot(_rms(h, pn_ref[...]).astype(BF16), pg_ref[...]))
    return h + gate * _dot(p_ref[...].astype(BF16), pp_ref[...])


def _dense_layer_kernel(h_ref, oa_ref, ob_ref, oc_ref, p_ref, wo_ref, gn_ref, wg_ref, wu_ref,
                        wd_ref, pn_ref, pg_ref, pp_ref, out_ref, acc_ref, *, ff_chunk):
    h1 = _mix_residual(h_ref, oa_ref, ob_ref, oc_ref, wo_ref)
    xn = _rms(h1, gn_ref[...]).astype(BF16)
    acc_ref[...] = h1
    _swiglu_accumulate(xn, wg_ref, wu_ref, wd_ref, acc_ref, ff_chunk)
    out_ref[...] = _ple(acc_ref[...], p_ref, pn_ref, pg_ref, pp_ref)


def dense_layer(h, oa, ob, oc, p, wo, gn, wg, wu, wd, pn, pg, pp, *, tm, ff_chunk):
    n, d = h.shape
    row = lambda w: pl.BlockSpec((tm, w), lambda i: (i, 0))
    weights = (wo, gn, wg, wu, wd, pn, pg, pp)
    return pl.pallas_call(
        functools.partial(_dense_layer_kernel, ff_chunk=ff_chunk),
        grid=(n // tm,),
        in_specs=[row(d), row(oa.shape[1]), row(ob.shape[1]), row(oc.shape[1]), row(p.shape[1])]
                 + [_resident(w.shape) for w in weights],
        out_specs=row(d),
        out_shape=jax.ShapeDtypeStruct((n, d), F32),
        scratch_shapes=[pltpu.VMEM((tm, d), F32)],
        compiler_params=_params(("arbitrary",)),
        name="dense_layer",
    )(h, oa, ob, oc, p, *weights)


def _pack_halves(x):
    half = x.shape[1] // 2
    lo = pltpu.bitcast(x[:, :half].astype(BF16).astype(F32), jnp.uint32)
    hi = pltpu.bitcast(x[:, half:].astype(BF16).astype(F32), jnp.uint32)
    return (lo >> 16) | (hi & jnp.uint32(0xFFFF0000))


def _unpack_halves(w):
    lo = pltpu.bitcast(w << 16, F32).astype(BF16)
    hi = pltpu.bitcast(w & jnp.uint32(0xFFFF0000), F32).astype(BF16)
    return lo, hi


def _router_kernel(h_ref, oa_ref, ob_ref, oc_ref, wo_ref, gn_ref, wr_ref,
                   h1_ref, xpk_ref, meta_ref, cnt_ref):
    @pl.when(pl.program_id(0) == 0)
    def _():
        cnt_ref[...] = jnp.zeros_like(cnt_ref)

    h1 = _mix_residual(h_ref, oa_ref, ob_ref, oc_ref, wo_ref)
    h1_ref[...] = h1
    xn = _rms(h1, gn_ref[...])
    xpk_ref[...] = _pack_halves(xn)

    logits = jnp.dot(xn, wr_ref[...], precision=lax.Precision.HIGHEST,
                     preferred_element_type=F32)
    tm = logits.shape[0]
    lane = lax.broadcasted_iota(jnp.int32, logits.shape, 1)
    logits = jnp.where(lane < N_EXPERTS, logits, -jnp.inf)

    def top(vals):
        best = jnp.max(vals, axis=-1, keepdims=True)
        idx = jnp.min(jnp.where(vals == best, lane, LANES), axis=-1, keepdims=True)
        return best, idx

    v1, i1 = top(logits)
    v2, i2 = top(jnp.where(lane == i1, -jnp.inf, logits))
    e = jnp.exp(v2 - v1)
    g1 = 1.0 / (1.0 + e)
    g2 = e / (1.0 + e)

    oh1 = (lane == i1).astype(F32)
    oh2 = (lane == i2).astype(F32)
    both = oh1 + oh2
    r = lax.broadcasted_iota(jnp.int32, (tm, tm), 0)
    c = lax.broadcasted_iota(jnp.int32, (tm, tm), 1)
    before = _dot((c < r).astype(BF16), both.astype(BF16)) + cnt_ref[...]
    rank1 = jnp.sum(before * oh1, axis=-1, keepdims=True)
    rank2 = jnp.sum(before * oh2, axis=-1, keepdims=True)
    cnt_ref[...] += jnp.sum(both, axis=0, keepdims=True)

    meta = jnp.zeros(logits.shape, F32)
    for k, val in enumerate((i1.astype(F32), i2.astype(F32), rank1, rank2, g1, g2)):
        meta = jnp.where(lane == k, val, meta)
    meta_ref[...] = meta


def route_tokens(h, oa, ob, oc, wo, gn, wr, *, tm):
    n, d = h.shape
    row = lambda w: pl.BlockSpec((tm, w), lambda i: (i, 0))
    weights = (wo, gn, wr)
    return pl.pallas_call(
        _router_kernel,
        grid=(n // tm,),
        in_specs=[row(d), row(oa.shape[1]), row(ob.shape[1]), row(oc.shape[1])]
                 + [_resident(w.shape) for w in weights],
        out_specs=[row(d), row(d // 2), row(LANES), pl.BlockSpec((1, LANES), lambda i: (0, 0))],
        out_shape=[jax.ShapeDtypeStruct((n, d), F32),
                   jax.ShapeDtypeStruct((n, d // 2), jnp.uint32),
                   jax.ShapeDtypeStruct((n, LANES), F32),
                   jax.ShapeDtypeStruct((1, LANES), F32)],
        compiler_params=_params(("arbitrary",)),
        name="router",
    )(h, oa, ob, oc, *weights)


def _dispatch_kernel(pos_ref, x_ref, zero_ref, xs_ref, sem):
    del zero_ref
    tm = x_ref.shape[0]

    def row_copy(r, k):
        return pltpu.make_async_copy(x_ref.at[pl.ds(r, 1), :],
                                     xs_ref.at[pl.ds(pos_ref[0, k, r], 1), :], sem)

    _start_rows(tm, row_copy)
    for k in range(2):
        pltpu.make_async_copy(x_ref, xs_ref.at[pl.ds(0, tm), :], sem).wait()


def _start_rows(n_rows, row_copy):
    def body(g, carry):
        base = pl.multiple_of(g * ROW_UNROLL, ROW_UNROLL)
        for j in range(ROW_UNROLL):
            row_copy(base + j, 0).start(priority=0)
            row_copy(base + j, 1).start(priority=1)
        return carry

    lax.fori_loop(0, n_rows // ROW_UNROLL, body, 0)


def dispatch(xpk, pos, n_slots, *, tm):
    n, w = xpk.shape
    zeros = jnp.zeros((n_slots, w), jnp.uint32)
    return pl.pallas_call(
        _dispatch_kernel,
        grid=(n // tm,),
        in_specs=[pl.BlockSpec((1, 2, tm), lambda i: (i, 0, 0), memory_space=pltpu.SMEM),
                  pl.BlockSpec((tm, w), lambda i: (i, 0)),
                  pl.BlockSpec(memory_space=pl.ANY)],
        out_specs=pl.BlockSpec(memory_space=pl.ANY),
        out_shape=jax.ShapeDtypeStruct((n_slots, w), jnp.uint32),
        scratch_shapes=[pltpu.SemaphoreType.DMA(())],
        input_output_aliases={2: 0},
        compiler_params=_params(("arbitrary",)),
        name="moe_dispatch",
    )(pos, xpk, zeros)


def _experts_kernel(te_ref, nt_ref, x_ref, wg_ref, wu_ref, wd_ref, o_ref, x_scr, *, sub_chunk):
    del te_ref
    t = pl.program_id(0)
    c = pl.program_id(1)

    @pl.when(c == 0)
    def _():
        o_ref[...] = jnp.zeros_like(o_ref)

    @pl.when(t < nt_ref[0])
    def _():
        @pl.when(c == 0)
        def _():
            lo, hi = _unpack_halves(x_ref[...])
            half = lo.shape[1]
            x_scr[:, :half] = lo
            x_scr[:, half:] = hi

        _swiglu_accumulate(x_scr[...], wg_ref, wu_ref, wd_ref, o_ref, sub_chunk)


def experts(xs, tile_expert, n_tiles, wg, wu, wd, *, tile, ff_chunk, sub_chunk):
    n_slots, w = xs.shape
    d = 2 * w
    d_ff = wg.shape[2]
    nc = d_ff // ff_chunk
    grid_tiles = n_slots // tile

    def tile_idx(t, c, te, nt):
        return (jnp.maximum(jnp.minimum(t, nt[0] - 1), 0), 0)

    def chunk_idx(t, c, nt):
        return jnp.where(t < nt[0], c, nc - 1)

    grid_spec = pltpu.PrefetchScalarGridSpec(
        num_scalar_prefetch=2,
        grid=(grid_tiles, nc),
        in_specs=[pl.BlockSpec((tile, w), tile_idx),
                  pl.BlockSpec((None, d, ff_chunk), lambda t, c, te, nt: (te[t], 0, chunk_idx(t, c, nt))),
                  pl.BlockSpec((None, d, ff_chunk), lambda t, c, te, nt: (te[t], 0, chunk_idx(t, c, nt))),
                  pl.BlockSpec((None, ff_chunk, d), lambda t, c, te, nt: (te[t], chunk_idx(t, c, nt), 0))],
        out_specs=pl.BlockSpec((tile, d), lambda t, c, te, nt: (t, 0)),
        scratch_shapes=[pltpu.VMEM((tile, d), BF16)],
    )
    return pl.pallas_call(
        functools.partial(_experts_kernel, sub_chunk=sub_chunk),
        grid_spec=grid_spec,
        out_shape=jax.ShapeDtypeStruct((n_slots, d), F32),
        compiler_params=_params(("arbitrary", "arbitrary")),
        name="moe_experts",
    )(tile_expert, n_tiles, xs, wg, wu, wd)


def _combine_kernel(pos_ref, h_ref, meta_ref, p_ref, pn_ref, pg_ref, pp_ref, fn_ref, ys_ref,
                    out_ref, y1_scr, y2_scr, sem):
    tm = h_ref.shape[0]
    bufs = (y1_scr, y2_scr)

    def row_copy(r, k):
        return pltpu.make_async_copy(ys_ref.at[pl.ds(pos_ref[0, k, r], 1), :],
                                     bufs[k].at[pl.ds(r, 1), :], sem)

    _start_rows(tm, row_copy)
    for buf in bufs:
        pltpu.make_async_copy(ys_ref.at[pl.ds(0, tm), :], buf, sem).wait()

    meta = meta_ref[...]
    lane = lax.broadcasted_iota(jnp.int32, meta.shape, 1)
    g1 = jnp.sum(jnp.where(lane == 4, meta, 0.0), axis=-1, keepdims=True)
    g2 = jnp.sum(jnp.where(lane == 5, meta, 0.0), axis=-1, keepdims=True)
    h2 = h_ref[...] + g1 * y1_scr[...] + g2 * y2_scr[...]
    h3 = _ple(h2, p_ref, pn_ref, pg_ref, pp_ref)
    out_ref[...] = _rms(h3, fn_ref[...])


def combine(pos, h1, meta, p, pn, pg, pp, fn, ys, *, tm):
    n, d = h1.shape
    row = lambda w: pl.BlockSpec((tm, w), lambda i: (i, 0))
    weights = (pn, pg, pp, fn)
    return pl.pallas_call(
        _combine_kernel,
        grid=(n // tm,),
        in_specs=[pl.BlockSpec((1, 2, tm), lambda i: (i, 0, 0), memory_space=pltpu.SMEM),
                  row(d), row(LANES), row(p.shape[1])]
                 + [_resident(w.shape) for w in weights]
                 + [pl.BlockSpec(memory_space=pl.ANY)],
        out_specs=row(d),
        out_shape=jax.ShapeDtypeStruct((n, d), F32),
        scratch_shapes=[pltpu.VMEM((tm, d), F32), pltpu.VMEM((tm, d), F32),
                        pltpu.SemaphoreType.DMA(())],
        compiler_params=_params(("arbitrary",)),
        name="moe_combine",
    )(pos, h1, meta, p, *weights, ys)


def _prep_w_in(w):
    scale = HEAD_DIM ** -0.5
    qa, ka, va = w[:, 0:512] * (scale * LOG2_E), w[:, 512:1024], w[:, 1024:1536]
    qb, kb, vb = w[:, 1536:1792] * (scale * LOG2_E), w[:, 1792:2048], w[:, 2048:2304]
    fb = w[:, 2304:2308]
    qc, kc, vc, gc = w[:, 2308:2564], w[:, 2564:2820] * scale, w[:, 2820:3076], w[:, 3076:3332]
    main = jnp.concatenate([qa, ka, va, qb, kb, vb, qc, kc, vc, gc], axis=1).astype(BF16)
    f = jnp.pad(fb, ((0, 0), (0, LANES - fb.shape[1]))).astype(BF16)
    return main, f


def _pad_lanes(v):
    return jnp.pad(v, (0, LANES - v.shape[0]))[None, :]


class Tiles(NamedTuple):
    rows: int
    attn: int
    scan: int
    ret_chunk: int
    ff_chunk: int
    moe_tile: int
    moe_ff_chunk: int
    moe_sub_chunk: int


TILES = Tiles(rows=512, attn=512, scan=512, ret_chunk=256, ff_chunk=256,
              moe_tile=1024, moe_ff_chunk=1792, moe_sub_chunk=256)


def _mixer(h, b, s, norm_g, w_in, b_forget, lq1, lk1, lq2, lk2, subln, ret_gn, layer_idx, t):
    w_main, w_f = _prep_w_in(w_in)
    proj, f = norm_proj(h, norm_g[None, :], w_main, w_f, tm=t.rows)
    proj = proj.reshape(b, s, -1)
    cum_col, cum_row = fox_prep(f.reshape(b, s, LANES), _pad_lanes(b_forget), blk=t.scan)
    lam_init = 0.8 - 0.6 * math.exp(-0.3 * layer_idx)
    lam = jnp.exp(jnp.sum(lq1 * lk1)) - jnp.exp(jnp.sum(lq2 * lk2)) + lam_init
    vt_a = proj[:, :, VA * LANES:QB * LANES].transpose(0, 2, 1)
    vt_b = proj[:, :, VB * LANES:QC * LANES].transpose(0, 2, 1)
    oa = diff_attention(proj, vt_a, lam.reshape(1), subln[None, :], tq=t.attn,
                        out_scale=1.0 - lam_init)
    ob = fox_attention(proj, vt_b, cum_col, cum_row, tq=t.attn)
    oc = retention(proj, ret_gn[None, :], chunk=t.ret_chunk)
    n = b * s
    return oa.reshape(n, -1), ob.reshape(n, -1), oc.reshape(n, -1)


def _route_plan(meta, counts, tile, n_slots, tm):
    n = meta.shape[0]
    idx = meta[:, 0:2].astype(jnp.int32)
    rank = meta[:, 2:4].astype(jnp.int32)
    cnt = counts[0, :N_EXPERTS].astype(jnp.int32)
    padded = ((cnt + tile - 1) // tile) * tile
    ends = jnp.cumsum(padded)
    starts = ends - padded
    pos = starts[idx] + rank
    n_tiles = ends[-1] // tile
    tiles = jnp.arange(n_slots // tile, dtype=jnp.int32)
    tile_expert = jnp.sum(tiles[:, None] >= (ends // tile)[None, :], axis=1)
    last = jnp.sum(n_tiles - 1 >= ends // tile)
    tile_expert = jnp.where(tiles < n_tiles, tile_expert, last).astype(jnp.int32)
    tile_expert = jnp.clip(tile_expert, 0, N_EXPERTS - 1)
    pos = jnp.clip(pos, 0, n_slots - 1)
    pos = pos.reshape(n // tm, tm, 2).transpose(0, 2, 1)
    return pos, tile_expert, n_tiles.reshape(1).astype(jnp.int32)


def forward(x, p, norm_mix, w_in, b_forget, lambda_q1, lambda_k1, lambda_q2, lambda_k2,
            diff_subln, ret_gn, w_out, norm_ffn, dense_w_gate, dense_w_up, dense_w_down,
            router, moe_w_gate, moe_w_up, moe_w_down, ple_norm, ple_gate, ple_proj, final_norm,
            t=TILES):
    assert w_in.shape[0] == 2, "two layers: dense SwiGLU, then routed experts"
    b, s, d = x.shape
    n = b * s
    h = x.reshape(n, d)
    pf = p.reshape(p.shape[0], n, p.shape[-1])
    bf = lambda a: a.astype(BF16)
    row = lambda a: a[None, :]

    oa, ob, oc = _mixer(h, b, s, norm_mix[0], w_in[0], b_forget[0], lambda_q1[0], lambda_k1[0],
                        lambda_q2[0], lambda_k2[0], diff_subln[0], ret_gn[0], 0, t)
    h = dense_layer(h, oa, ob, oc, pf[0], bf(w_out[0]), row(norm_ffn[0]), bf(dense_w_gate[0]),
                    bf(dense_w_up[0]), bf(dense_w_down[0]), row(ple_norm[0]), bf(ple_gate[0]),
                    bf(ple_proj[0]), tm=t.rows, ff_chunk=t.ff_chunk)

    oa, ob, oc = _mixer(h, b, s, norm_mix[1], w_in[1], b_forget[1], lambda_q1[1], lambda_k1[1],
                        lambda_q2[1], lambda_k2[1], diff_subln[1], ret_gn[1], 1, t)
    wr = jnp.pad(router[0], ((0, 0), (0, LANES - N_EXPERTS)))
    h1, xpk, meta, counts = route_tokens(h, oa, ob, oc, bf(w_out[1]), row(norm_ffn[1]), wr,
                                         tm=t.rows)
    n_slots = 2 * n + N_EXPERTS * t.moe_tile
    pos, tile_expert, n_tiles = _route_plan(meta, counts, t.moe_tile, n_slots, t.rows)
    xs = dispatch(xpk, pos, n_slots, tm=t.rows)
    ys = experts(xs, tile_expert, n_tiles, bf(moe_w_gate[0]), bf(moe_w_up[0]), bf(moe_w_down[0]),
                 tile=t.moe_tile, ff_chunk=t.moe_ff_chunk, sub_chunk=t.moe_sub_chunk)
    out = combine(pos, h1, meta, pf[1], row(ple_norm[1]), bf(ple_gate[1]), bf(ple_proj[1]),
                  row(final_norm), ys, tm=t.rows)
    return out.reshape(b, s, d)


def kernel(x, p, norm_mix, w_in, b_forget, lambda_q1, lambda_k1, lambda_q2, lambda_k2,
           diff_subln, ret_gn, w_out, norm_ffn, dense_w_gate, dense_w_up, dense_w_down,
           router, moe_w_gate, moe_w_up, moe_w_down, ple_norm, ple_gate, ple_proj, final_norm):
    return forward(x, p, norm_mix, w_in, b_forget, lambda_q1, lambda_k1, lambda_q2, lambda_k2,
                   diff_subln, ret_gn, w_out, norm_ffn, dense_w_gate, dense_w_up, dense_w_down,
                   router, moe_w_gate, moe_w_up, moe_w_down, ple_norm, ple_gate, ple_proj,
                   final_norm)
```

```python
import functools
import math
from typing import NamedTuple

import numpy as np
import jax
import jax.numpy as jnp
from jax import lax
from jax.experimental import pallas as pl
from jax.experimental.pallas import tpu as pltpu

F32 = jnp.float32
BF16 = jnp.bfloat16

LANES = 128
ROW_UNROLL = 8
VMEM_LIMIT_BYTES = 56 * 1024 * 1024

HEAD_DIM = 64
DIFF_HEADS = 4
FOX_HEADS = 4
RET_HEADS = 4
RET_THETA = 10000.0
N_EXPERTS = 8
EPS = 1e-6
NEG_INF = -1e30
LOG2_E = math.log2(math.e)

QA, KA, VA = 0, 4, 8
QB, KB, VB = 12, 14, 16
QC, KC, VC, GC = 18, 20, 22, 24
N_PROJ_BLOCKS = 26


def _params(semantics):
    return pltpu.CompilerParams(dimension_semantics=semantics,
                                vmem_limit_bytes=VMEM_LIMIT_BYTES)


def _resident(shape):
    nd = len(shape)
    return pl.BlockSpec(shape, lambda *_: (0,) * nd, pipeline_mode=pl.Buffered(1))


def _rms(x, g):
    return x * lax.rsqrt(jnp.mean(x * x, axis=-1, keepdims=True) + EPS) * g


def _sigmoid(x):
    return 1.0 / (1.0 + jnp.exp(-x))


def _dot(a, b):
    return jnp.dot(a, b, preferred_element_type=F32)


def _dot_nt(a, b):
    return lax.dot_general(a, b, (((1,), (1,)), ((), ())), preferred_element_type=F32)


def _lane_masks(shape):
    lane = lax.broadcasted_iota(jnp.int32, shape, len(shape) - 1)
    return lane < HEAD_DIM


def _norm_proj_kernel(h_ref, g_ref, w_ref, wf_ref, out_ref, f_ref, *, col_chunk):
    xn = _rms(h_ref[...], g_ref[...]).astype(BF16)
    ncol = w_ref.shape[1]
    for c in range(0, ncol, col_chunk):
        w = min(col_chunk, ncol - c)
        out_ref[:, c:c + w] = _dot(xn, w_ref[:, c:c + w]).astype(out_ref.dtype)
    f_ref[...] = _dot(xn, wf_ref[...])


def norm_proj(h, g, w_main, w_f, *, tm):
    n, d = h.shape
    ncol = w_main.shape[1]
    return pl.pallas_call(
        functools.partial(_norm_proj_kernel, col_chunk=512),
        grid=(n // tm,),
        in_specs=[pl.BlockSpec((tm, d), lambda i: (i, 0)),
                  _resident((1, d)), _resident((d, ncol)), _resident((d, LANES))],
        out_specs=[pl.BlockSpec((tm, ncol), lambda i: (i, 0)),
                   pl.BlockSpec((tm, LANES), lambda i: (i, 0))],
        out_shape=[jax.ShapeDtypeStruct((n, ncol), BF16),
                   jax.ShapeDtypeStruct((n, LANES), F32)],
        compiler_params=_params(("arbitrary",)),
        name="norm_proj",
    )(h, g, w_main, w_f)


def _fox_prep_kernel(f_ref, b_ref, col_ref, row_ref, carry_ref):
    @pl.when(pl.program_id(1) == 0)
    def _():
        carry_ref[...] = jnp.zeros_like(carry_ref)

    x = f_ref[...] + b_ref[...]
    log_f = (jnp.minimum(x, 0.0) - jnp.log1p(jnp.exp(-jnp.abs(x)))) * LOG2_E
    blk = x.shape[0]
    r = lax.broadcasted_iota(jnp.int32, (blk, blk), 0)
    c = lax.broadcasted_iota(jnp.int32, (blk, blk), 1)
    tri = (c <= r).astype(F32)
    cum = jnp.dot(tri, log_f, precision=lax.Precision.HIGHEST,
                  preferred_element_type=F32) + carry_ref[...]
    col_ref[...] = cum
    row_ref[...] = cum.T[:8, :]
    carry_ref[...] = cum[blk - 1:blk, :]


def fox_prep(f, b_pad, *, blk):
    bsz, s, _ = f.shape
    return pl.pallas_call(
        _fox_prep_kernel,
        grid=(bsz, s // blk),
        in_specs=[pl.BlockSpec((None, blk, LANES), lambda b, j: (b, j, 0)),
                  pl.BlockSpec((1, LANES), lambda b, j: (0, 0))],
        out_specs=[pl.BlockSpec((None, blk, LANES), lambda b, j: (b, j, 0)),
                   pl.BlockSpec((None, 8, blk), lambda b, j: (b, 0, j))],
        out_shape=[jax.ShapeDtypeStruct((bsz, s, LANES), F32),
                   jax.ShapeDtypeStruct((bsz, 8, s), F32)],
        scratch_shapes=[pltpu.VMEM((1, LANES), F32)],
        compiler_params=_params(("arbitrary", "arbitrary")),
        name="fox_prep",
    )(f, b_pad)


def _attn_kernel(*refs, mode, tq, out_scale):
    if mode == "diff":
        lam_ref, q_ref, k_ref, vt_ref, gain_ref, o_ref = refs[:6]
    else:
        q_ref, k_ref, vt_ref, ccol_ref, crow_ref, o_ref = refs[:6]
        cs_scr = refs[16]
    state = (refs[6:9], refs[9:12])
    st_a, st_b = refs[12:14], refs[14:16]
    qi = pl.program_id(2)
    pair = pl.program_id(1)

    q = q_ref[...]
    lo = _lane_masks(q.shape)
    zero = jnp.zeros_like(q)
    q_maps = (jnp.where(lo, q, zero), jnp.where(lo, zero, q))

    for m_ref, l_ref, acc_ref in state:
        m_ref[...] = jnp.full_like(m_ref, NEG_INF)
        l_ref[...] = jnp.zeros_like(l_ref)
        acc_ref[...] = jnp.zeros_like(acc_ref)

    if mode == "fox":
        @pl.when(qi == 0)
        def _():
            ccol = ccol_ref[...]
            lane = lax.broadcasted_iota(jnp.int32, ccol.shape, 1)
            for i in range(2):
                col = jnp.sum(jnp.where(lane == 2 * pair + i, ccol, 0.0), axis=-1, keepdims=True)
                cs_scr[i] = jnp.broadcast_to(col, ccol.shape)

        ct = tuple(crow_ref[pl.ds(2 * pair + i, 1), :] for i in range(2))

    def scores(c, bufs):
        k = k_ref[pl.ds(pl.multiple_of(c * tq, tq), tq), :]
        for i in range(2):
            bufs[i][...] = _dot_nt(k, q_maps[i])

    def consume(c, bufs, masked):
        start = pl.multiple_of(c * tq, tq)
        vt = vt_ref[:, pl.ds(start, tq)]
        for i in range(2):
            m_ref, l_ref, acc_ref = state[i]
            st = bufs[i][...]
            if mode == "fox":
                cs = cs_scr[i, pl.ds(start, tq), :]
                st = st + ct[i] - jnp.tile(cs, (1, tq // LANES))
            if masked:
                key = lax.broadcasted_iota(jnp.int32, st.shape, 0)
                qry = lax.broadcasted_iota(jnp.int32, st.shape, 1)
                st = jnp.where(key <= qry, st, NEG_INF)
            m_old = m_ref[...]
            m_new = jnp.maximum(m_old, jnp.max(st, axis=0, keepdims=True))
            alpha = jnp.exp2(m_old - m_new)
            p = jnp.exp2(st - m_new)
            l_ref[...] = alpha * l_ref[...] + jnp.sum(p, axis=0, keepdims=True)
            acc_ref[...] = alpha * acc_ref[...] + _dot(vt, p.astype(BF16))
            m_ref[...] = m_new

    def body(j, carry):
        scores(2 * j + 1, st_b)
        consume(2 * j, st_a, False)
        scores(2 * j + 2, st_a)
        consume(2 * j + 1, st_b, False)
        return carry

    scores(0, st_a)
    lax.fori_loop(0, qi // 2, body, 0)

    @pl.when(qi % 2 == 0)
    def _():
        consume(qi, st_a, True)

    @pl.when(qi % 2 == 1)
    def _():
        scores(qi, st_b)
        consume(qi - 1, st_a, False)
        consume(qi, st_b, True)

    o1, o2 = (acc_ref[...] * (1.0 / l_ref[...]) for _, l_ref, acc_ref in state)
    if mode == "diff":
        o = (o1 - lam_ref[0] * o2).T
        o = _rms(o, gain_ref[...]) * out_scale
    else:
        chan = lax.broadcasted_iota(jnp.int32, o1.shape, 0)
        o = jnp.where(chan < HEAD_DIM, o1, o2).T
    o_ref[...] = o.astype(o_ref.dtype)


def _attn_scratch(tq):
    per_map = [pltpu.VMEM((1, tq), F32), pltpu.VMEM((1, tq), F32), pltpu.VMEM((LANES, tq), F32)]
    scores = [pltpu.VMEM((tq, tq), F32)] * 4
    return per_map + per_map + scores


def diff_attention(proj, vt, lam, gain, *, tq, out_scale):
    bsz, s, _ = proj.shape
    kern = functools.partial(_attn_kernel, mode="diff", tq=tq, out_scale=out_scale)
    return pl.pallas_call(
        kern,
        grid=(bsz, DIFF_HEADS, s // tq),
        in_specs=[pl.BlockSpec(memory_space=pltpu.SMEM),
                  pl.BlockSpec((None, tq, LANES), lambda b, h, i: (b, i, QA + h)),
                  pl.BlockSpec((None, s, LANES), lambda b, h, i: (b, 0, KA + h)),
                  pl.BlockSpec((None, LANES, s), lambda b, h, i: (b, h, 0)),
                  pl.BlockSpec((1, LANES), lambda b, h, i: (0, 0))],
        out_specs=pl.BlockSpec((None, tq, LANES), lambda b, h, i: (b, i, h)),
        out_shape=jax.ShapeDtypeStruct((bsz, s, DIFF_HEADS * LANES), BF16),
        scratch_shapes=_attn_scratch(tq),
        compiler_params=_params(("arbitrary", "arbitrary", "arbitrary")),
        name="diff_attention",
    )(lam, proj, proj, vt, gain)


def fox_attention(proj, vt, cum_col, cum_row, *, tq):
    bsz, s, _ = proj.shape
    npair = FOX_HEADS // 2
    kern = functools.partial(_attn_kernel, mode="fox", tq=tq, out_scale=None)
    return pl.pallas_call(
        kern,
        grid=(bsz, npair, s // tq),
        in_specs=[pl.BlockSpec((None, tq, LANES), lambda b, h, i: (b, i, QB + h)),
                  pl.BlockSpec((None, s, LANES), lambda b, h, i: (b, 0, KB + h)),
                  pl.BlockSpec((None, LANES, s), lambda b, h, i: (b, h, 0)),
                  pl.BlockSpec((None, s, LANES), lambda b, h, i: (b, 0, 0)),
                  pl.BlockSpec((None, 8, tq), lambda b, h, i: (b, 0, i))],
        out_specs=pl.BlockSpec((None, tq, LANES), lambda b, h, i: (b, i, h)),
        out_shape=jax.ShapeDtypeStruct((bsz, s, npair * LANES), BF16),
        scratch_shapes=_attn_scratch(tq) + [pltpu.VMEM((2, s, LANES), F32)],
        compiler_params=_params(("arbitrary", "arbitrary", "arbitrary")),
        name="fox_attention",
    )(proj, proj, vt, cum_col, cum_row)


def _retention_tables(s, chunk):
    half = HEAD_DIM // 2
    inv = RET_THETA ** (-np.arange(half, dtype=np.float64) / half)
    ang = np.arange(s, dtype=np.float64)[:, None] * inv[None, :]
    cos = np.tile(np.cos(ang), (1, 4))
    sin = np.tile(np.concatenate([-np.sin(ang), np.sin(ang)], axis=1), (1, 2))
    log_g = np.log1p(-np.exp2(-5.0 - np.arange(RET_HEADS, dtype=np.float64)))
    j = np.arange(chunk, dtype=np.float64)
    diff = j[:, None] - j[None, :]
    inner = np.where(diff[None] >= 0, np.exp(np.where(diff[None] >= 0, diff[None], 0.0)
                                             * log_g[:, None, None]), 0.0)
    q_dec = np.exp((j + 1.0)[None, :] * log_g[:, None])
    k_dec = np.exp((chunk - 1.0 - j)[None, :] * log_g[:, None])
    c_dec = np.exp(chunk * log_g)
    npair = RET_HEADS // 2

    def lanes(t):
        t = t.reshape(npair, 2, chunk)
        return np.repeat(t.transpose(0, 2, 1), HEAD_DIM, axis=2)

    blk = np.kron(np.eye(2), np.ones((HEAD_DIM, HEAD_DIM)))
    c_mat = np.stack([np.kron(np.diag(c_dec[2 * p:2 * p + 2]), np.ones((HEAD_DIM, HEAD_DIM)))
                      for p in range(npair)])
    f = lambda a: jnp.asarray(a, F32)
    return dict(cos=f(cos), sin=f(sin), inner=f(inner.reshape(npair, 2, chunk, chunk)),
                q_dec=f(lanes(q_dec)), k_dec=f(lanes(k_dec)), c_mat=f(c_mat), blk=f(blk))


def _retention_kernel(q_ref, k_ref, v_ref, g_ref, cos_ref, sin_ref, inner_ref, qdec_ref,
                      kdec_ref, cmat_ref, blk_ref, gain_ref, o_ref, state_ref):
    @pl.when(pl.program_id(1) == 0)
    def _():
        state_ref[...] = jnp.zeros_like(state_ref)

    cos = cos_ref[...]
    sin = sin_ref[...]
    lane = lax.broadcasted_iota(jnp.int32, cos.shape, 1)
    first_half = (lane % HEAD_DIM) < (HEAD_DIM // 2)
    lo = lane < HEAD_DIM

    def rope(x):
        x = x.astype(F32)
        swapped = jnp.where(first_half, pltpu.roll(x, LANES - HEAD_DIM // 2, 1),
                            pltpu.roll(x, HEAD_DIM // 2, 1))
        return x * cos + swapped * sin

    for p in range(RET_HEADS // 2):
        cols = slice(p * LANES, (p + 1) * LANES)
        qr = rope(q_ref[:, cols])
        kr = rope(k_ref[:, cols])
        v = v_ref[:, cols]
        qb = qr.astype(BF16)
        kb = kr.astype(BF16)
        zero = jnp.zeros_like(qb)
        a1 = (_dot_nt(jnp.where(lo, qb, zero), kb) * inner_ref[p, 0]).astype(BF16)
        a2 = (_dot_nt(jnp.where(lo, zero, qb), kb) * inner_ref[p, 1]).astype(BF16)
        state = state_ref[p]
        o = jnp.where(lo, _dot(a1, v), _dot(a2, v))
        o = o + _dot((qr * qdec_ref[p]).astype(BF16), state.astype(BF16))
        kd = (kr * kdec_ref[p]).astype(BF16)
        state_ref[p] = state * cmat_ref[p] + _dot(kd.T, v) * blk_ref[...]

        inv_n = 1.0 / HEAD_DIM
        sum_lo = jnp.sum(jnp.where(lo, o, 0.0), axis=-1, keepdims=True)
        sum_hi = jnp.sum(jnp.where(lo, 0.0, o), axis=-1, keepdims=True)
        d = o - jnp.where(lo, sum_lo, sum_hi) * inv_n
        d2 = d * d
        var_lo = jnp.sum(jnp.where(lo, d2, 0.0), axis=-1, keepdims=True)
        var_hi = jnp.sum(jnp.where(lo, 0.0, d2), axis=-1, keepdims=True)
        y = d * lax.rsqrt(jnp.where(lo, var_lo, var_hi) * inv_n + EPS)
        g = g_ref[:, cols].astype(F32)
        o_ref[:, cols] = (g * _sigmoid(g) * (y * gain_ref[:, cols])).astype(o_ref.dtype)


def retention(proj, gain, *, chunk):
    bsz, s, _ = proj.shape
    npair = RET_HEADS // 2
    t = _retention_tables(s, chunk)

    width = npair * LANES

    def act(off):
        return pl.BlockSpec((None, chunk, width), lambda b, c: (b, c, off // npair))

    pos = pl.BlockSpec((chunk, LANES), lambda b, c: (c, 0))
    return pl.pallas_call(
        _retention_kernel,
        grid=(bsz, s // chunk),
        in_specs=[act(QC), act(KC), act(VC), act(GC), pos, pos,
                  _resident((npair, 2, chunk, chunk)), _resident((npair, chunk, LANES)),
                  _resident((npair, chunk, LANES)), _resident((npair, LANES, LANES)),
                  _resident((LANES, LANES)), _resident((1, width))],
        out_specs=pl.BlockSpec((None, chunk, width), lambda b, c: (b, c, 0)),
        out_shape=jax.ShapeDtypeStruct((bsz, s, width), BF16),
        scratch_shapes=[pltpu.VMEM((npair, LANES, LANES), F32)],
        compiler_params=_params(("arbitrary", "arbitrary")),
        name="retention",
    )(proj, proj, proj, proj, t["cos"], t["sin"], t["inner"], t["q_dec"], t["k_dec"],
      t["c_mat"], t["blk"], gain)


def _mix_residual(h_ref, oa_ref, ob_ref, oc_ref, wo_ref):
    wa = oa_ref.shape[1]
    wb = ob_ref.shape[1]
    return (h_ref[...] + _dot(oa_ref[...], wo_ref[0:wa, :])
            + _dot(ob_ref[...], wo_ref[wa:wa + wb, :])
            + _dot(oc_ref[...], wo_ref[wa + wb:, :]))


def _swiglu_accumulate(xn, wg_ref, wu_ref, wd_ref, acc_ref, chunk):
    def body(i, carry):
        c = pl.multiple_of(i * chunk, chunk)
        g = _dot(xn, wg_ref[:, pl.ds(c, chunk)])
        u = _dot(xn, wu_ref[:, pl.ds(c, chunk)])
        a = (g * _sigmoid(g) * u).astype(BF16)
        acc_ref[...] += _dot(a, wd_ref[pl.ds(c, chunk), :])
        return carry

    lax.fori_loop(0, wg_ref.shape[1] // chunk, body, 0)


def _ple(h, p_ref, pn_ref, pg_ref, pp_ref):
    gate = _sigmoid(_dot(_rms(h, pn_ref[...]).astype(BF16), pg_ref[...]))
    return h + gate * _dot(p_ref[...].astype(BF16), pp_ref[...])


def _dense_layer_kernel(h_ref, oa_ref, ob_ref, oc_ref, p_ref, wo_ref, gn_ref, wg_ref, wu_ref,
                        wd_ref, pn_ref, pg_ref, pp_ref, out_ref, acc_ref, *, ff_chunk):
    h1 = _mix_residual(h_ref, oa_ref, ob_ref, oc_ref, wo_ref)
    xn = _rms(h1, gn_ref[...]).astype(BF16)
    acc_ref[...] = h1
    _swiglu_accumulate(xn, wg_ref, wu_ref, wd_ref, acc_ref, ff_chunk)
    out_ref[...] = _ple(acc_ref[...], p_ref, pn_ref, pg_ref, pp_ref)


def dense_layer(h, oa, ob, oc, p, wo, gn, wg, wu, wd, pn, pg, pp, *, tm, ff_chunk):
    n, d = h.shape
    row = lambda w: pl.BlockSpec((tm, w), lambda i: (i, 0))
    weights = (wo, gn, wg, wu, wd, pn, pg, pp)
    return pl.pallas_call(
        functools.partial(_dense_layer_kernel, ff_chunk=ff_chunk),
        grid=(n // tm,),
        in_specs=[row(d), row(oa.shape[1]), row(ob.shape[1]), row(oc.shape[1]), row(p.shape[1])]
                 + [_resident(w.shape) for w in weights],
        out_specs=row(d),
        out_shape=jax.ShapeDtypeStruct((n, d), F32),
        scratch_shapes=[pltpu.VMEM((tm, d), F32)],
        compiler_params=_params(("arbitrary",)),
        name="dense_layer",
    )(h, oa, ob, oc, p, *weights)


def _router_kernel(h_ref, oa_ref, ob_ref, oc_ref, wo_ref, gn_ref, wr_ref,
                   h1_ref, xn_ref, meta_ref, cnt_ref):
    @pl.when(pl.program_id(0) == 0)
    def _():
        cnt_ref[...] = jnp.zeros_like(cnt_ref)

    h1 = _mix_residual(h_ref, oa_ref, ob_ref, oc_ref, wo_ref)
    h1_ref[...] = h1
    xn = _rms(h1, gn_ref[...])
    xn_ref[...] = xn

    logits = jnp.dot(xn, wr_ref[...], precision=lax.Precision.HIGHEST,
                     preferred_element_type=F32)
    tm = logits.shape[0]
    lane = lax.broadcasted_iota(jnp.int32, logits.shape, 1)
    logits = jnp.where(lane < N_EXPERTS, logits, -jnp.inf)

    def top(vals):
        best = jnp.max(vals, axis=-1, keepdims=True)
        idx = jnp.min(jnp.where(vals == best, lane, LANES), axis=-1, keepdims=True)
        return best, idx

    v1, i1 = top(logits)
    v2, i2 = top(jnp.where(lane == i1, -jnp.inf, logits))
    e = jnp.exp(v2 - v1)
    g1 = 1.0 / (1.0 + e)
    g2 = e / (1.0 + e)

    oh1 = (lane == i1).astype(F32)
    oh2 = (lane == i2).astype(F32)
    both = oh1 + oh2
    r = lax.broadcasted_iota(jnp.int32, (tm, tm), 0)
    c = lax.broadcasted_iota(jnp.int32, (tm, tm), 1)
    before = _dot((c < r).astype(BF16), both.astype(BF16)) + cnt_ref[...]
    rank1 = jnp.sum(before * oh1, axis=-1, keepdims=True)
    rank2 = jnp.sum(before * oh2, axis=-1, keepdims=True)
    cnt_ref[...] += jnp.sum(both, axis=0, keepdims=True)

    meta = jnp.zeros(logits.shape, F32)
    for k, val in enumerate((i1.astype(F32), i2.astype(F32), rank1, rank2, g1, g2)):
        meta = jnp.where(lane == k, val, meta)
    meta_ref[...] = meta


def route_tokens(h, oa, ob, oc, wo, gn, wr, *, tm):
    n, d = h.shape
    row = lambda w: pl.BlockSpec((tm, w), lambda i: (i, 0))
    weights = (wo, gn, wr)
    return pl.pallas_call(
        _router_kernel,
        grid=(n // tm,),
        in_specs=[row(d), row(oa.shape[1]), row(ob.shape[1]), row(oc.shape[1])]
                 + [_resident(w.shape) for w in weights],
        out_specs=[row(d), row(d), row(LANES), pl.BlockSpec((1, LANES), lambda i: (0, 0))],
        out_shape=[jax.ShapeDtypeStruct((n, d), F32),
                   jax.ShapeDtypeStruct((n, d), F32),
                   jax.ShapeDtypeStruct((n, LANES), F32),
                   jax.ShapeDtypeStruct((1, LANES), F32)],
        compiler_params=_params(("arbitrary",)),
        name="router",
    )(h, oa, ob, oc, *weights)


def _dispatch_kernel(pos_ref, x_ref, zero_ref, xs_ref, sem):
    del zero_ref
    tm = x_ref.shape[0]

    def row_copy(r, k):
        return pltpu.make_async_copy(x_ref.at[pl.ds(r, 1), :],
                                     xs_ref.at[pl.ds(pos_ref[0, k, r], 1), :], sem)

    _start_rows(tm, row_copy)
    for k in range(2):
        pltpu.make_async_copy(x_ref, xs_ref.at[pl.ds(0, tm), :], sem).wait()


def _start_rows(n_rows, row_copy):
    def body(g, carry):
        base = pl.multiple_of(g * ROW_UNROLL, ROW_UNROLL)
        for j in range(ROW_UNROLL):
            row_copy(base + j, 0).start(priority=0)
            row_copy(base + j, 1).start(priority=1)
        return carry

    lax.fori_loop(0, n_rows // ROW_UNROLL, body, 0)


def dispatch(xn, pos, n_slots, *, tm):
    n, w = xn.shape
    zeros = jnp.zeros((n_slots, w), xn.dtype)
    return pl.pallas_call(
        _dispatch_kernel,
        grid=(n // tm,),
        in_specs=[pl.BlockSpec((1, 2, tm), lambda i: (i, 0, 0), memory_space=pltpu.SMEM),
                  pl.BlockSpec((tm, w), lambda i: (i, 0)),
                  pl.BlockSpec(memory_space=pl.ANY)],
        out_specs=pl.BlockSpec(memory_space=pl.ANY),
        out_shape=jax.ShapeDtypeStruct((n_slots, w), xn.dtype),
        scratch_shapes=[pltpu.SemaphoreType.DMA(())],
        input_output_aliases={2: 0},
        compiler_params=_params(("arbitrary",)),
        name="moe_dispatch",
    )(pos, xn, zeros)


def _experts_kernel(te_ref, nt_ref, x_ref, wg_ref, wu_ref, wd_ref, o_ref, x_scr, *, sub_chunk):
    del te_ref
    t = pl.program_id(0)
    c = pl.program_id(1)

    @pl.when(c == 0)
    def _():
        o_ref[...] = jnp.zeros_like(o_ref)

    @pl.when(t < nt_ref[0])
    def _():
        @pl.when(c == 0)
        def _():
            x_scr[...] = x_ref[...].astype(BF16)

        _swiglu_accumulate(x_scr[...], wg_ref, wu_ref, wd_ref, o_ref, sub_chunk)


def experts(xs, tile_expert, n_tiles, wg, wu, wd, *, tile, ff_chunk, sub_chunk):
    n_slots, d = xs.shape
    d_ff = wg.shape[2]
    nc = d_ff // ff_chunk
    grid_tiles = n_slots // tile

    def tile_idx(t, c, te, nt):
        return (jnp.maximum(jnp.minimum(t, nt[0] - 1), 0), 0)

    def chunk_idx(t, c, nt):
        return jnp.where(t < nt[0], c, nc - 1)

    grid_spec = pltpu.PrefetchScalarGridSpec(
        num_scalar_prefetch=2,
        grid=(grid_tiles, nc),
        in_specs=[pl.BlockSpec((tile, d), tile_idx),
                  pl.BlockSpec((None, d, ff_chunk), lambda t, c, te, nt: (te[t], 0, chunk_idx(t, c, nt))),
                  pl.BlockSpec((None, d, ff_chunk), lambda t, c, te, nt: (te[t], 0, chunk_idx(t, c, nt))),
                  pl.BlockSpec((None, ff_chunk, d), lambda t, c, te, nt: (te[t], chunk_idx(t, c, nt), 0))],
        out_specs=pl.BlockSpec((tile, d), lambda t, c, te, nt: (t, 0)),
        scratch_shapes=[pltpu.VMEM((tile, d), BF16)],
    )
    return pl.pallas_call(
        functools.partial(_experts_kernel, sub_chunk=sub_chunk),
        grid_spec=grid_spec,
        out_shape=jax.ShapeDtypeStruct((n_slots, d), F32),
        compiler_params=_params(("arbitrary", "arbitrary")),
        name="moe_experts",
    )(tile_expert, n_tiles, xs, wg, wu, wd)


def _combine_kernel(pos_ref, h_ref, meta_ref, p_ref, pn_ref, pg_ref, pp_ref, fn_ref, ys_ref,
                    out_ref, y1_scr, y2_scr, sem):
    tm = h_ref.shape[0]
    bufs = (y1_scr, y2_scr)

    def row_copy(r, k):
        return pltpu.make_async_copy(ys_ref.at[pl.ds(pos_ref[0, k, r], 1), :],
                                     bufs[k].at[pl.ds(r, 1), :], sem)

    _start_rows(tm, row_copy)
    for buf in bufs:
        pltpu.make_async_copy(ys_ref.at[pl.ds(0, tm), :], buf, sem).wait()

    meta = meta_ref[...]
    lane = lax.broadcasted_iota(jnp.int32, meta.shape, 1)
    g1 = jnp.sum(jnp.where(lane == 4, meta, 0.0), axis=-1, keepdims=True)
    g2 = jnp.sum(jnp.where(lane == 5, meta, 0.0), axis=-1, keepdims=True)
    h2 = h_ref[...] + g1 * y1_scr[...] + g2 * y2_scr[...]
    h3 = _ple(h2, p_ref, pn_ref, pg_ref, pp_ref)
    out_ref[...] = _rms(h3, fn_ref[...])


def combine(pos, h1, meta, p, pn, pg, pp, fn, ys, *, tm):
    n, d = h1.shape
    row = lambda w: pl.BlockSpec((tm, w), lambda i: (i, 0))
    weights = (pn, pg, pp, fn)
    return pl.pallas_call(
        _combine_kernel,
        grid=(n // tm,),
        in_specs=[pl.BlockSpec((1, 2, tm), lambda i: (i, 0, 0), memory_space=pltpu.SMEM),
                  row(d), row(LANES), row(p.shape[1])]
                 + [_resident(w.shape) for w in weights]
                 + [pl.BlockSpec(memory_space=pl.ANY)],
        out_specs=row(d),
        out_shape=jax.ShapeDtypeStruct((n, d), F32),
        scratch_shapes=[pltpu.VMEM((tm, d), F32), pltpu.VMEM((tm, d), F32),
                        pltpu.SemaphoreType.DMA(())],
        compiler_params=_params(("arbitrary",)),
        name="moe_combine",
    )(pos, h1, meta, p, *weights, ys)


def _prep_w_in(w):
    scale = HEAD_DIM ** -0.5
    qa, ka, va = w[:, 0:512] * (scale * LOG2_E), w[:, 512:1024], w[:, 1024:1536]
    qb, kb, vb = w[:, 1536:1792] * (scale * LOG2_E), w[:, 1792:2048], w[:, 2048:2304]
    fb = w[:, 2304:2308]
    qc, kc, vc, gc = w[:, 2308:2564], w[:, 2564:2820] * scale, w[:, 2820:3076], w[:, 3076:3332]
    main = jnp.concatenate([qa, ka, va, qb, kb, vb, qc, kc, vc, gc], axis=1).astype(BF16)
    f = jnp.pad(fb, ((0, 0), (0, LANES - fb.shape[1]))).astype(BF16)
    return main, f


def _pad_lanes(v):
    return jnp.pad(v, (0, LANES - v.shape[0]))[None, :]


class Tiles(NamedTuple):
    rows: int
    attn: int
    scan: int
    ret_chunk: int
    ff_chunk: int
    moe_tile: int
    moe_ff_chunk: int
    moe_sub_chunk: int


TILES = Tiles(rows=512, attn=512, scan=512, ret_chunk=256, ff_chunk=256,
              moe_tile=1024, moe_ff_chunk=1792, moe_sub_chunk=256)


def _mixer(h, b, s, norm_g, w_in, b_forget, lq1, lk1, lq2, lk2, subln, ret_gn, layer_idx, t):
    w_main, w_f = _prep_w_in(w_in)
    proj, f = norm_proj(h, norm_g[None, :], w_main, w_f, tm=t.rows)
    proj = proj.reshape(b, s, -1)
    cum_col, cum_row = fox_prep(f.reshape(b, s, LANES), _pad_lanes(b_forget), blk=t.scan)
    lam_init = 0.8 - 0.6 * math.exp(-0.3 * layer_idx)
    lam = jnp.exp(jnp.sum(lq1 * lk1)) - jnp.exp(jnp.sum(lq2 * lk2)) + lam_init
    vt_a = proj[:, :, VA * LANES:QB * LANES].transpose(0, 2, 1)
    vt_b = proj[:, :, VB * LANES:QC * LANES].transpose(0, 2, 1)
    oa = diff_attention(proj, vt_a, lam.reshape(1), subln[None, :], tq=t.attn,
                        out_scale=1.0 - lam_init)
    ob = fox_attention(proj, vt_b, cum_col, cum_row, tq=t.attn)
    oc = retention(proj, ret_gn[None, :], chunk=t.ret_chunk)
    n = b * s
    return oa.reshape(n, -1), ob.reshape(n, -1), oc.reshape(n, -1)


def _route_plan(meta, counts, tile, n_slots, tm):
    n = meta.shape[0]
    idx = meta[:, 0:2].astype(jnp.int32)
    rank = meta[:, 2:4].astype(jnp.int32)
    cnt = counts[0, :N_EXPERTS].astype(jnp.int32)
    padded = ((cnt + tile - 1) // tile) * tile
    ends = jnp.cumsum(padded)
    starts = ends - padded
    pos = starts[idx] + rank
    n_tiles = ends[-1] // tile
    tiles = jnp.arange(n_slots // tile, dtype=jnp.int32)
    tile_expert = jnp.sum(tiles[:, None] >= (ends // tile)[None, :], axis=1)
    last = jnp.sum(n_tiles - 1 >= ends // tile)
    tile_expert = jnp.where(tiles < n_tiles, tile_expert, last).astype(jnp.int32)
    tile_expert = jnp.clip(tile_expert, 0, N_EXPERTS - 1)
    pos = jnp.clip(pos, 0, n_slots - 1)
    pos = pos.reshape(n // tm, tm, 2).transpose(0, 2, 1)
    return pos, tile_expert, n_tiles.reshape(1).astype(jnp.int32)


def forward(x, p, norm_mix, w_in, b_forget, lambda_q1, lambda_k1, lambda_q2, lambda_k2,
            diff_subln, ret_gn, w_out, norm_ffn, dense_w_gate, dense_w_up, dense_w_down,
            router, moe_w_gate, moe_w_up, moe_w_down, ple_norm, ple_gate, ple_proj, final_norm,
            t=TILES):
    assert w_in.shape[0] == 2, "two layers: dense SwiGLU, then routed experts"
    b, s, d = x.shape
    n = b * s
    h = x.reshape(n, d)
    pf = p.reshape(p.shape[0], n, p.shape[-1])
    bf = lambda a: a.astype(BF16)
    row = lambda a: a[None, :]

    oa, ob, oc = _mixer(h, b, s, norm_mix[0], w_in[0], b_forget[0], lambda_q1[0], lambda_k1[0],
                        lambda_q2[0], lambda_k2[0], diff_subln[0], ret_gn[0], 0, t)
    h = dense_layer(h, oa, ob, oc, pf[0], bf(w_out[0]), row(norm_ffn[0]), bf(dense_w_gate[0]),
                    bf(dense_w_up[0]), bf(dense_w_down[0]), row(ple_norm[0]), bf(ple_gate[0]),
                    bf(ple_proj[0]), tm=t.rows, ff_chunk=t.ff_chunk)

    oa, ob, oc = _mixer(h, b, s, norm_mix[1], w_in[1], b_forget[1], lambda_q1[1], lambda_k1[1],
                        lambda_q2[1], lambda_k2[1], diff_subln[1], ret_gn[1], 1, t)
    wr = jnp.pad(router[0], ((0, 0), (0, LANES - N_EXPERTS)))
    h1, xn, meta, counts = route_tokens(h, oa, ob, oc, bf(w_out[1]), row(norm_ffn[1]), wr,
                                         tm=t.rows)
    n_slots = 2 * n + N_EXPERTS * t.moe_tile
    pos, tile_expert, n_tiles = _route_plan(meta, counts, t.moe_tile, n_slots, t.rows)
    xs = dispatch(xn, pos, n_slots, tm=t.rows)
    ys = experts(xs, tile_expert, n_tiles, bf(moe_w_gate[0]), bf(moe_w_up[0]), bf(moe_w_down[0]),
                 tile=t.moe_tile, ff_chunk=t.moe_ff_chunk, sub_chunk=t.moe_sub_chunk)
    out = combine(pos, h1, meta, pf[1], row(ple_norm[1]), bf(ple_gate[1]), bf(ple_proj[1]),
                  row(final_norm), ys, tm=t.rows)
    return out.reshape(b, s, d)


def kernel(x, p, norm_mix, w_in, b_forget, lambda_q1, lambda_k1, lambda_q2, lambda_k2,
           diff_subln, ret_gn, w_out, norm_ffn, dense_w_gate, dense_w_up, dense_w_down,
           router, moe_w_gate, moe_w_up, moe_w_down, ple_norm, ple_gate, ple_proj, final_norm):
    return forward(x, p, norm_mix, w_in, b_forget, lambda_q1, lambda_k1, lambda_q2, lambda_k2,
                   diff_subln, ret_gn, w_out, norm_ffn, dense_w_gate, dense_w_up, dense_w_down,
                   router, moe_w_gate, moe_w_up, moe_w_down, ple_norm, ple_gate, ple_proj,
                   final_norm)
```

```python
import functools
import math
from typing import NamedTuple

import numpy as np
import jax
import jax.numpy as jnp
from jax import lax
from jax.experimental import pallas as pl
from jax.experimental.pallas import tpu as pltpu

F32 = jnp.float32
BF16 = jnp.bfloat16

LANES = 128
ROW_UNROLL = 8
VMEM_LIMIT_BYTES = 56 * 1024 * 1024

HEAD_DIM = 64
DIFF_HEADS = 4
FOX_HEADS = 4
RET_HEADS = 4
RET_THETA = 10000.0
N_EXPERTS = 8
EPS = 1e-6
NEG_INF = -1e30
LOG2_E = math.log2(math.e)

QA, KA = 0, 4
QB, KB = 8, 10
QC, KC, VC, GC = 12, 14, 16, 18
VT_A, VT_B = 0, 4


def _params(semantics):
    return pltpu.CompilerParams(dimension_semantics=semantics,
                                vmem_limit_bytes=VMEM_LIMIT_BYTES)


def _resident(shape):
    nd = len(shape)
    return pl.BlockSpec(shape, lambda *_: (0,) * nd, pipeline_mode=pl.Buffered(1))


def _rms(x, g):
    return x * lax.rsqrt(jnp.mean(x * x, axis=-1, keepdims=True) + EPS) * g


def _sigmoid(x):
    return 1.0 / (1.0 + jnp.exp(-x))


def _dot(a, b):
    return jnp.dot(a, b, preferred_element_type=F32)


def _dot_nt(a, b):
    return lax.dot_general(a, b, (((1,), (1,)), ((), ())), preferred_element_type=F32)


def _lane_masks(shape):
    lane = lax.broadcasted_iota(jnp.int32, shape, len(shape) - 1)
    return lane < HEAD_DIM


def _norm_proj_kernel(h_ref, g_ref, w_ref, wf_ref, wvt_ref, out_ref, f_ref, vt_ref, *,
                      col_chunk):
    xn = _rms(h_ref[...], g_ref[...]).astype(BF16)
    ncol = w_ref.shape[1]
    for c in range(0, ncol, col_chunk):
        w = min(col_chunk, ncol - c)
        out_ref[:, c:c + w] = _dot(xn, w_ref[:, c:c + w]).astype(out_ref.dtype)
    f_ref[...] = _dot(xn, wf_ref[...])
    vt_ref[...] = _dot_nt(wvt_ref[...], xn).astype(vt_ref.dtype)


def norm_proj(h, g, w_main, w_f, w_vt, *, tm, seq):
    n, d = h.shape
    ncol = w_main.shape[1]
    nvt = w_vt.shape[0]
    per_seq = seq // tm
    return pl.pallas_call(
        functools.partial(_norm_proj_kernel, col_chunk=512),
        grid=(n // tm,),
        in_specs=[pl.BlockSpec((tm, d), lambda i: (i, 0)),
                  _resident((1, d)), _resident((d, ncol)), _resident((d, LANES)),
                  _resident((nvt, d))],
        out_specs=[pl.BlockSpec((tm, ncol), lambda i: (i, 0)),
                   pl.BlockSpec((tm, LANES), lambda i: (i, 0)),
                   pl.BlockSpec((None, nvt, tm), lambda i: (i // per_seq, 0, i % per_seq))],
        out_shape=[jax.ShapeDtypeStruct((n, ncol), BF16),
                   jax.ShapeDtypeStruct((n, LANES), F32),
                   jax.ShapeDtypeStruct((n // seq, nvt, seq), BF16)],
        compiler_params=_params(("arbitrary",)),
        name="norm_proj",
    )(h, g, w_main, w_f, w_vt)


def _fox_prep_kernel(f_ref, b_ref, col_ref, row_ref, carry_ref):
    @pl.when(pl.program_id(1) == 0)
    def _():
        carry_ref[...] = jnp.zeros_like(carry_ref)

    x = f_ref[...] + b_ref[...]
    log_f = (jnp.minimum(x, 0.0) - jnp.log1p(jnp.exp(-jnp.abs(x)))) * LOG2_E
    blk = x.shape[0]
    r = lax.broadcasted_iota(jnp.int32, (blk, blk), 0)
    c = lax.broadcasted_iota(jnp.int32, (blk, blk), 1)
    tri = (c <= r).astype(F32)
    cum = jnp.dot(tri, log_f, precision=lax.Precision.HIGHEST,
                  preferred_element_type=F32) + carry_ref[...]
    col_ref[...] = cum
    row_ref[...] = cum.T[:8, :]
    carry_ref[...] = cum[blk - 1:blk, :]


def fox_prep(f, b_pad, *, blk):
    bsz, s, _ = f.shape
    return pl.pallas_call(
        _fox_prep_kernel,
        grid=(bsz, s // blk),
        in_specs=[pl.BlockSpec((None, blk, LANES), lambda b, j: (b, j, 0)),
                  pl.BlockSpec((1, LANES), lambda b, j: (0, 0))],
        out_specs=[pl.BlockSpec((None, blk, LANES), lambda b, j: (b, j, 0)),
                   pl.BlockSpec((None, 8, blk), lambda b, j: (b, 0, j))],
        out_shape=[jax.ShapeDtypeStruct((bsz, s, LANES), F32),
                   jax.ShapeDtypeStruct((bsz, 8, s), F32)],
        scratch_shapes=[pltpu.VMEM((1, LANES), F32)],
        compiler_params=_params(("arbitrary", "arbitrary")),
        name="fox_prep",
    )(f, b_pad)


def _attn_kernel(*refs, mode, tq, out_scale):
    if mode == "diff":
        lam_ref, q_ref, k_ref, vt_ref, gain_ref, o_ref = refs[:6]
    else:
        q_ref, k_ref, vt_ref, ccol_ref, crow_ref, o_ref = refs[:6]
        cs_scr = refs[16]
    state = (refs[6:9], refs[9:12])
    st_a, st_b = refs[12:14], refs[14:16]
    qi = pl.program_id(2)
    pair = pl.program_id(1)

    q = q_ref[...]
    lo = _lane_masks(q.shape)
    zero = jnp.zeros_like(q)
    q_maps = (jnp.where(lo, q, zero), jnp.where(lo, zero, q))

    for m_ref, l_ref, acc_ref in state:
        m_ref[...] = jnp.full_like(m_ref, NEG_INF)
        l_ref[...] = jnp.zeros_like(l_ref)
        acc_ref[...] = jnp.zeros_like(acc_ref)

    if mode == "fox":
        @pl.when(qi == 0)
        def _():
            ccol = ccol_ref[...]
            lane = lax.broadcasted_iota(jnp.int32, ccol.shape, 1)
            for i in range(2):
                col = jnp.sum(jnp.where(lane == 2 * pair + i, ccol, 0.0), axis=-1, keepdims=True)
                cs_scr[i] = jnp.broadcast_to(col, ccol.shape)

        ct = tuple(crow_ref[pl.ds(2 * pair + i, 1), :] for i in range(2))

    def scores(c, bufs):
        k = k_ref[pl.ds(pl.multiple_of(c * tq, tq), tq), :]
        for i in range(2):
            bufs[i][...] = _dot_nt(k, q_maps[i])

    def consume(c, bufs, masked):
        start = pl.multiple_of(c * tq, tq)
        vt = vt_ref[:, pl.ds(start, tq)]
        for i in range(2):
            m_ref, l_ref, acc_ref = state[i]
            st = bufs[i][...]
            if mode == "fox":
                cs = cs_scr[i, pl.ds(start, tq), :]
                st = st + ct[i] - jnp.tile(cs, (1, tq // LANES))
            if masked:
                key = lax.broadcasted_iota(jnp.int32, st.shape, 0)
                qry = lax.broadcasted_iota(jnp.int32, st.shape, 1)
                st = jnp.where(key <= qry, st, NEG_INF)
            m_old = m_ref[...]
            m_new = jnp.maximum(m_old, jnp.max(st, axis=0, keepdims=True))
            alpha = jnp.exp2(m_old - m_new)
            p = jnp.exp2(st - m_new)
            l_ref[...] = alpha * l_ref[...] + jnp.sum(p, axis=0, keepdims=True)
            acc_ref[...] = alpha * acc_ref[...] + _dot(vt, p.astype(BF16))
            m_ref[...] = m_new

    def body(j, carry):
        scores(2 * j + 1, st_b)
        consume(2 * j, st_a, False)
        scores(2 * j + 2, st_a)
        consume(2 * j + 1, st_b, False)
        return carry

    scores(0, st_a)
    lax.fori_loop(0, qi // 2, body, 0)

    @pl.when(qi % 2 == 0)
    def _():
        consume(qi, st_a, True)

    @pl.when(qi % 2 == 1)
    def _():
        scores(qi, st_b)
        consume(qi - 1, st_a, False)
        consume(qi, st_b, True)

    o1, o2 = (acc_ref[...] * (1.0 / l_ref[...]) for _, l_ref, acc_ref in state)
    if mode == "diff":
        o = (o1 - lam_ref[0] * o2).T
        o = _rms(o, gain_ref[...]) * out_scale
    else:
        chan = lax.broadcasted_iota(jnp.int32, o1.shape, 0)
        o = jnp.where(chan < HEAD_DIM, o1, o2).T
    o_ref[...] = o.astype(o_ref.dtype)


def _attn_scratch(tq):
    per_map = [pltpu.VMEM((1, tq), F32), pltpu.VMEM((1, tq), F32), pltpu.VMEM((LANES, tq), F32)]
    scores = [pltpu.VMEM((tq, tq), F32)] * 4
    return per_map + per_map + scores


def diff_attention(proj, vt, lam, gain, *, tq, out_scale):
    bsz, s, _ = proj.shape
    kern = functools.partial(_attn_kernel, mode="diff", tq=tq, out_scale=out_scale)
    return pl.pallas_call(
        kern,
        grid=(bsz, DIFF_HEADS, s // tq),
        in_specs=[pl.BlockSpec(memory_space=pltpu.SMEM),
                  pl.BlockSpec((None, tq, LANES), lambda b, h, i: (b, i, QA + h)),
                  pl.BlockSpec((None, s, LANES), lambda b, h, i: (b, 0, KA + h)),
                  pl.BlockSpec((None, LANES, s), lambda b, h, i: (b, VT_A + h, 0)),
                  pl.BlockSpec((1, LANES), lambda b, h, i: (0, 0))],
        out_specs=pl.BlockSpec((None, tq, LANES), lambda b, h, i: (b, i, h)),
        out_shape=jax.ShapeDtypeStruct((bsz, s, DIFF_HEADS * LANES), BF16),
        scratch_shapes=_attn_scratch(tq),
        compiler_params=_params(("arbitrary", "arbitrary", "arbitrary")),
        name="diff_attention",
    )(lam, proj, proj, vt, gain)


def fox_attention(proj, vt, cum_col, cum_row, *, tq):
    bsz, s, _ = proj.shape
    npair = FOX_HEADS // 2
    kern = functools.partial(_attn_kernel, mode="fox", tq=tq, out_scale=None)
    return pl.pallas_call(
        kern,
        grid=(bsz, npair, s // tq),
        in_specs=[pl.BlockSpec((None, tq, LANES), lambda b, h, i: (b, i, QB + h)),
                  pl.BlockSpec((None, s, LANES), lambda b, h, i: (b, 0, KB + h)),
                  pl.BlockSpec((None, LANES, s), lambda b, h, i: (b, VT_B + h, 0)),
                  pl.BlockSpec((None, s, LANES), lambda b, h, i: (b, 0, 0)),
                  pl.BlockSpec((None, 8, tq), lambda b, h, i: (b, 0, i))],
        out_specs=pl.BlockSpec((None, tq, LANES), lambda b, h, i: (b, i, h)),
        out_shape=jax.ShapeDtypeStruct((bsz, s, npair * LANES), BF16),
        scratch_shapes=_attn_scratch(tq) + [pltpu.VMEM((2, s, LANES), F32)],
        compiler_params=_params(("arbitrary", "arbitrary", "arbitrary")),
        name="fox_attention",
    )(proj, proj, vt, cum_col, cum_row)


def _retention_tables(s, chunk):
    half = HEAD_DIM // 2
    inv = RET_THETA ** (-np.arange(half, dtype=np.float64) / half)
    ang = np.arange(s, dtype=np.float64)[:, None] * inv[None, :]
    cos = np.tile(np.cos(ang), (1, 4))
    sin = np.tile(np.concatenate([-np.sin(ang), np.sin(ang)], axis=1), (1, 2))
    log_g = np.log1p(-np.exp2(-5.0 - np.arange(RET_HEADS, dtype=np.float64)))
    j = np.arange(chunk, dtype=np.float64)
    diff = j[:, None] - j[None, :]
    inner = np.where(diff[None] >= 0, np.exp(np.where(diff[None] >= 0, diff[None], 0.0)
                                             * log_g[:, None, None]), 0.0)
    q_dec = np.exp((j + 1.0)[None, :] * log_g[:, None])
    k_dec = np.exp((chunk - 1.0 - j)[None, :] * log_g[:, None])
    c_dec = np.exp(chunk * log_g)
    npair = RET_HEADS // 2

    def lanes(t):
        t = t.reshape(npair, 2, chunk)
        return np.repeat(t.transpose(0, 2, 1), HEAD_DIM, axis=2)

    blk = np.kron(np.eye(2), np.ones((HEAD_DIM, HEAD_DIM)))
    c_mat = np.stack([np.kron(np.diag(c_dec[2 * p:2 * p + 2]), np.ones((HEAD_DIM, HEAD_DIM)))
                      for p in range(npair)])
    f = lambda a: jnp.asarray(a, F32)
    return dict(cos=f(cos), sin=f(sin), inner=f(inner.reshape(npair, 2, chunk, chunk)),
                q_dec=f(lanes(q_dec)), k_dec=f(lanes(k_dec)), c_mat=f(c_mat), blk=f(blk))


def _retention_kernel(q_ref, k_ref, v_ref, g_ref, cos_ref, sin_ref, inner_ref, qdec_ref,
                      kdec_ref, cmat_ref, blk_ref, gain_ref, o_ref, state_ref):
    @pl.when(pl.program_id(1) == 0)
    def _():
        state_ref[...] = jnp.zeros_like(state_ref)

    cos = cos_ref[...]
    sin = sin_ref[...]
    lane = lax.broadcasted_iota(jnp.int32, cos.shape, 1)
    first_half = (lane % HEAD_DIM) < (HEAD_DIM // 2)
    lo = lane < HEAD_DIM

    def rope(x):
        x = x.astype(F32)
        swapped = jnp.where(first_half, pltpu.roll(x, LANES - HEAD_DIM // 2, 1),
                            pltpu.roll(x, HEAD_DIM // 2, 1))
        return x * cos + swapped * sin

    for p in range(RET_HEADS // 2):
        cols = slice(p * LANES, (p + 1) * LANES)
        qr = rope(q_ref[:, cols])
        kr = rope(k_ref[:, cols])
        v = v_ref[:, cols]
        qb = qr.astype(BF16)
        kb = kr.astype(BF16)
        zero = jnp.zeros_like(qb)
        a1 = (_dot_nt(jnp.where(lo, qb, zero), kb) * inner_ref[p, 0]).astype(BF16)
        a2 = (_dot_nt(jnp.where(lo, zero, qb), kb) * inner_ref[p, 1]).astype(BF16)
        state = state_ref[p]
        o = jnp.where(lo, _dot(a1, v), _dot(a2, v))
        o = o + _dot((qr * qdec_ref[p]).astype(BF16), state.astype(BF16))
        kd = (kr * kdec_ref[p]).astype(BF16)
        state_ref[p] = state * cmat_ref[p] + _dot(kd.T, v) * blk_ref[...]

        inv_n = 1.0 / HEAD_DIM
        sum_lo = jnp.sum(jnp.where(lo, o, 0.0), axis=-1, keepdims=True)
        sum_hi = jnp.sum(jnp.where(lo, 0.0, o), axis=-1, keepdims=True)
        d = o - jnp.where(lo, sum_lo, sum_hi) * inv_n
        d2 = d * d
        var_lo = jnp.sum(jnp.where(lo, d2, 0.0), axis=-1, keepdims=True)
        var_hi = jnp.sum(jnp.where(lo, 0.0, d2), axis=-1, keepdims=True)
        y = d * lax.rsqrt(jnp.where(lo, var_lo, var_hi) * inv_n + EPS)
        g = g_ref[:, cols].astype(F32)
        o_ref[:, cols] = (g * _sigmoid(g) * (y * gain_ref[:, cols])).astype(o_ref.dtype)


def retention(proj, gain, *, chunk):
    bsz, s, _ = proj.shape
    npair = RET_HEADS // 2
    t = _retention_tables(s, chunk)

    width = npair * LANES

    def act(off):
        return pl.BlockSpec((None, chunk, width), lambda b, c: (b, c, off // npair))

    pos = pl.BlockSpec((chunk, LANES), lambda b, c: (c, 0))
    return pl.pallas_call(
        _retention_kernel,
        grid=(bsz, s // chunk),
        in_specs=[act(QC), act(KC), act(VC), act(GC), pos, pos,
                  _resident((npair, 2, chunk, chunk)), _resident((npair, chunk, LANES)),
                  _resident((npair, chunk, LANES)), _resident((npair, LANES, LANES)),
                  _resident((LANES, LANES)), _resident((1, width))],
        out_specs=pl.BlockSpec((None, chunk, width), lambda b, c: (b, c, 0)),
        out_shape=jax.ShapeDtypeStruct((bsz, s, width), BF16),
        scratch_shapes=[pltpu.VMEM((npair, LANES, LANES), F32)],
        compiler_params=_params(("arbitrary", "arbitrary")),
        name="retention",
    )(proj, proj, proj, proj, t["cos"], t["sin"], t["inner"], t["q_dec"], t["k_dec"],
      t["c_mat"], t["blk"], gain)


def _mix_residual(h_ref, oa_ref, ob_ref, oc_ref, wo_ref):
    wa = oa_ref.shape[1]
    wb = ob_ref.shape[1]
    return (h_ref[...] + _dot(oa_ref[...], wo_ref[0:wa, :])
            + _dot(ob_ref[...], wo_ref[wa:wa + wb, :])
            + _dot(oc_ref[...], wo_ref[wa + wb:, :]))


def _swiglu_accumulate(xn, wg_ref, wu_ref, wd_ref, acc_ref, chunk):
    def body(i, carry):
        c = pl.multiple_of(i * chunk, chunk)
        g = _dot(xn, wg_ref[:, pl.ds(c, chunk)])
        u = _dot(xn, wu_ref[:, pl.ds(c, chunk)])
        a = (g * _sigmoid(g) * u).astype(BF16)
        acc_ref[...] += _dot(a, wd_ref[pl.ds(c, chunk), :])
        return carry

    lax.fori_loop(0, wg_ref.shape[1] // chunk, body, 0)


def _ple(h, p_ref, pn_ref, pg_ref, pp_ref):
    gate = _sigmoid(_dot(_rms(h, pn_ref[...]).astype(BF16), pg_ref[...]))
    return h + gate * _dot(p_ref[...].astype(BF16), pp_ref[...])


def _dense_layer_kernel(h_ref, oa_ref, ob_ref, oc_ref, p_ref, wo_ref, gn_ref, wg_ref, wu_ref,
                        wd_ref, pn_ref, pg_ref, pp_ref, out_ref, acc_ref, *, ff_chunk):
    h1 = _mix_residual(h_ref, oa_ref, ob_ref, oc_ref, wo_ref)
    xn = _rms(h1, gn_ref[...]).astype(BF16)
    acc_ref[...] = h1
    _swiglu_accumulate(xn, wg_ref, wu_ref, wd_ref, acc_ref, ff_chunk)
    out_ref[...] = _ple(acc_ref[...], p_ref, pn_ref, pg_ref, pp_ref)


def dense_layer(h, oa, ob, oc, p, wo, gn, wg, wu, wd, pn, pg, pp, *, tm, ff_chunk):
    n, d = h.shape
    row = lambda w: pl.BlockSpec((tm, w), lambda i: (i, 0))
    weights = (wo, gn, wg, wu, wd, pn, pg, pp)
    return pl.pallas_call(
        functools.partial(_dense_layer_kernel, ff_chunk=ff_chunk),
        grid=(n // tm,),
        in_specs=[row(d), row(oa.shape[1]), row(ob.shape[1]), row(oc.shape[1]), row(p.shape[1])]
                 + [_resident(w.shape) for w in weights],
        out_specs=row(d),
        out_shape=jax.ShapeDtypeStruct((n, d), F32),
        scratch_shapes=[pltpu.VMEM((tm, d), F32)],
        compiler_params=_params(("arbitrary",)),
        name="dense_layer",
    )(h, oa, ob, oc, p, *weights)


def _router_kernel(h_ref, oa_ref, ob_ref, oc_ref, wo_ref, gn_ref, wr_ref,
                   h1_ref, xn_ref, meta_ref, cnt_ref):
    @pl.when(pl.program_id(0) == 0)
    def _():
        cnt_ref[...] = jnp.zeros_like(cnt_ref)

    h1 = _mix_residual(h_ref, oa_ref, ob_ref, oc_ref, wo_ref)
    h1_ref[...] = h1
    xn = _rms(h1, gn_ref[...])
    xn_ref[...] = xn

    logits = jnp.dot(xn, wr_ref[...], precision=lax.Precision.HIGHEST,
                     preferred_element_type=F32)
    tm = logits.shape[0]
    lane = lax.broadcasted_iota(jnp.int32, logits.shape, 1)
    logits = jnp.where(lane < N_EXPERTS, logits, -jnp.inf)

    def top(vals):
        best = jnp.max(vals, axis=-1, keepdims=True)
        idx = jnp.min(jnp.where(vals == best, lane, LANES), axis=-1, keepdims=True)
        return best, idx

    v1, i1 = top(logits)
    v2, i2 = top(jnp.where(lane == i1, -jnp.inf, logits))
    e = jnp.exp(v2 - v1)
    g1 = 1.0 / (1.0 + e)
    g2 = e / (1.0 + e)

    oh1 = (lane == i1).astype(F32)
    oh2 = (lane == i2).astype(F32)
    both = oh1 + oh2
    r = lax.broadcasted_iota(jnp.int32, (tm, tm), 0)
    c = lax.broadcasted_iota(jnp.int32, (tm, tm), 1)
    before = _dot((c < r).astype(BF16), both.astype(BF16)) + cnt_ref[...]
    rank1 = jnp.sum(before * oh1, axis=-1, keepdims=True)
    rank2 = jnp.sum(before * oh2, axis=-1, keepdims=True)
    cnt_ref[...] += jnp.sum(both, axis=0, keepdims=True)

    meta = jnp.zeros(logits.shape, F32)
    for k, val in enumerate((i1.astype(F32), i2.astype(F32), rank1, rank2, g1, g2)):
        meta = jnp.where(lane == k, val, meta)
    meta_ref[...] = meta


def route_tokens(h, oa, ob, oc, wo, gn, wr, *, tm):
    n, d = h.shape
    row = lambda w: pl.BlockSpec((tm, w), lambda i: (i, 0))
    weights = (wo, gn, wr)
    return pl.pallas_call(
        _router_kernel,
        grid=(n // tm,),
        in_specs=[row(d), row(oa.shape[1]), row(ob.shape[1]), row(oc.shape[1])]
                 + [_resident(w.shape) for w in weights],
        out_specs=[row(d), row(d), row(LANES), pl.BlockSpec((1, LANES), lambda i: (0, 0))],
        out_shape=[jax.ShapeDtypeStruct((n, d), F32),
                   jax.ShapeDtypeStruct((n, d), F32),
                   jax.ShapeDtypeStruct((n, LANES), F32),
                   jax.ShapeDtypeStruct((1, LANES), F32)],
        compiler_params=_params(("arbitrary",)),
        name="router",
    )(h, oa, ob, oc, *weights)


def _dispatch_kernel(last_ref, pos_ref, x_ref, xs_ref, zero_scr, sem, zero_sem, *, tile):
    tm = x_ref.shape[0]
    zrows = zero_scr.shape[0]

    @pl.when(pl.program_id(0) == 0)
    def _():
        zero_scr[...] = jnp.zeros_like(zero_scr)

        def zero_copy(e, q):
            start = pl.multiple_of(last_ref[e] + q * zrows, zrows)
            return pltpu.make_async_copy(zero_scr, xs_ref.at[pl.ds(start, zrows), :], zero_sem)

        for e in range(N_EXPERTS):
            for q in range(tile // zrows):
                zero_copy(e, q).start()
        for e in range(N_EXPERTS):
            for q in range(tile // zrows):
                zero_copy(e, q).wait()

        n_trailing = (xs_ref.shape[0] - last_ref[N_EXPERTS]) // zrows

        def trailing(q, carry, wait):
            cp = zero_copy(N_EXPERTS, q)
            cp.wait() if wait else cp.start()
            return carry

        lax.fori_loop(0, n_trailing, functools.partial(trailing, wait=False), 0)
        lax.fori_loop(0, n_trailing, functools.partial(trailing, wait=True), 0)

    def row_copy(r, k):
        return pltpu.make_async_copy(x_ref.at[pl.ds(r, 1), :],
                                     xs_ref.at[pl.ds(pos_ref[0, k, r], 1), :], sem)

    _start_rows(tm, row_copy)
    for k in range(2):
        pltpu.make_async_copy(x_ref, xs_ref.at[pl.ds(0, tm), :], sem).wait()


def _start_rows(n_rows, row_copy):
    def body(g, carry):
        base = pl.multiple_of(g * ROW_UNROLL, ROW_UNROLL)
        for j in range(ROW_UNROLL):
            row_copy(base + j, 0).start(priority=0)
            row_copy(base + j, 1).start(priority=1)
        return carry

    lax.fori_loop(0, n_rows // ROW_UNROLL, body, 0)


def dispatch(xn, pos, last_tile_start, n_slots, *, tm, tile):
    n, w = xn.shape
    zrows = min(tile, 256)
    grid_spec = pltpu.PrefetchScalarGridSpec(
        num_scalar_prefetch=1,
        grid=(n // tm,),
        in_specs=[pl.BlockSpec((1, 2, tm), lambda i, last: (i, 0, 0), memory_space=pltpu.SMEM),
                  pl.BlockSpec((tm, w), lambda i, last: (i, 0))],
        out_specs=pl.BlockSpec(memory_space=pl.ANY),
        scratch_shapes=[pltpu.VMEM((zrows, w), xn.dtype), pltpu.SemaphoreType.DMA(()),
                        pltpu.SemaphoreType.DMA(())],
    )
    return pl.pallas_call(
        functools.partial(_dispatch_kernel, tile=tile),
        grid_spec=grid_spec,
        out_shape=jax.ShapeDtypeStruct((n_slots, w), xn.dtype),
        compiler_params=_params(("arbitrary",)),
        name="moe_dispatch",
    )(last_tile_start, pos, xn)


def _experts_kernel(te_ref, nt_ref, x_ref, wg_ref, wu_ref, wd_ref, o_ref, x_scr, *, sub_chunk):
    del te_ref
    t = pl.program_id(0)
    c = pl.program_id(1)

    @pl.when(c == 0)
    def _():
        o_ref[...] = jnp.zeros_like(o_ref)

    @pl.when(t < nt_ref[0])
    def _():
        @pl.when(c == 0)
        def _():
            x_scr[...] = x_ref[...].astype(BF16)

        _swiglu_accumulate(x_scr[...], wg_ref, wu_ref, wd_ref, o_ref, sub_chunk)


def experts(xs, tile_expert, n_tiles, wg, wu, wd, *, tile, ff_chunk, sub_chunk):
    n_slots, d = xs.shape
    d_ff = wg.shape[2]
    nc = d_ff // ff_chunk
    grid_tiles = n_slots // tile

    def tile_idx(t, c, te, nt):
        return (jnp.maximum(jnp.minimum(t, nt[0] - 1), 0), 0)

    def chunk_idx(t, c, nt):
        return jnp.where(t < nt[0], c, nc - 1)

    grid_spec = pltpu.PrefetchScalarGridSpec(
        num_scalar_prefetch=2,
        grid=(grid_tiles, nc),
        in_specs=[pl.BlockSpec((tile, d), tile_idx),
                  pl.BlockSpec((None, d, ff_chunk), lambda t, c, te, nt: (te[t], 0, chunk_idx(t, c, nt))),
                  pl.BlockSpec((None, d, ff_chunk), lambda t, c, te, nt: (te[t], 0, chunk_idx(t, c, nt))),
                  pl.BlockSpec((None, ff_chunk, d), lambda t, c, te, nt: (te[t], chunk_idx(t, c, nt), 0))],
        out_specs=pl.BlockSpec((tile, d), lambda t, c, te, nt: (t, 0)),
        scratch_shapes=[pltpu.VMEM((tile, d), BF16)],
    )
    return pl.pallas_call(
        functools.partial(_experts_kernel, sub_chunk=sub_chunk),
        grid_spec=grid_spec,
        out_shape=jax.ShapeDtypeStruct((n_slots, d), F32),
        compiler_params=_params(("arbitrary", "arbitrary")),
        name="moe_experts",
    )(tile_expert, n_tiles, xs, wg, wu, wd)


def _combine_kernel(pos_ref, h_ref, meta_ref, p_ref, pn_ref, pg_ref, pp_ref, fn_ref, ys_ref,
                    out_ref, y1_scr, y2_scr, sem):
    tm = h_ref.shape[0]
    bufs = (y1_scr, y2_scr)

    def row_copy(r, k):
        return pltpu.make_async_copy(ys_ref.at[pl.ds(pos_ref[0, k, r], 1), :],
                                     bufs[k].at[pl.ds(r, 1), :], sem)

    _start_rows(tm, row_copy)
    for buf in bufs:
        pltpu.make_async_copy(ys_ref.at[pl.ds(0, tm), :], buf, sem).wait()

    meta = meta_ref[...]
    lane = lax.broadcasted_iota(jnp.int32, meta.shape, 1)
    g1 = jnp.sum(jnp.where(lane == 4, meta, 0.0), axis=-1, keepdims=True)
    g2 = jnp.sum(jnp.where(lane == 5, meta, 0.0), axis=-1, keepdims=True)
    h2 = h_ref[...] + g1 * y1_scr[...] + g2 * y2_scr[...]
    h3 = _ple(h2, p_ref, pn_ref, pg_ref, pp_ref)
    out_ref[...] = _rms(h3, fn_ref[...])


def combine(pos, h1, meta, p, pn, pg, pp, fn, ys, *, tm):
    n, d = h1.shape
    row = lambda w: pl.BlockSpec((tm, w), lambda i: (i, 0))
    weights = (pn, pg, pp, fn)
    return pl.pallas_call(
        _combine_kernel,
        grid=(n // tm,),
        in_specs=[pl.BlockSpec((1, 2, tm), lambda i: (i, 0, 0), memory_space=pltpu.SMEM),
                  row(d), row(LANES), row(p.shape[1])]
                 + [_resident(w.shape) for w in weights]
                 + [pl.BlockSpec(memory_space=pl.ANY)],
        out_specs=row(d),
        out_shape=jax.ShapeDtypeStruct((n, d), F32),
        scratch_shapes=[pltpu.VMEM((tm, d), F32), pltpu.VMEM((tm, d), F32),
                        pltpu.SemaphoreType.DMA(())],
        compiler_params=_params(("arbitrary",)),
        name="moe_combine",
    )(pos, h1, meta, p, *weights, ys)


def _prep_w_in(w):
    scale = HEAD_DIM ** -0.5
    qa, ka, va = w[:, 0:512] * (scale * LOG2_E), w[:, 512:1024], w[:, 1024:1536]
    qb, kb, vb = w[:, 1536:1792] * (scale * LOG2_E), w[:, 1792:2048], w[:, 2048:2304]
    fb = w[:, 2304:2308]
    qc, kc, vc, gc = w[:, 2308:2564], w[:, 2564:2820] * scale, w[:, 2820:3076], w[:, 3076:3332]
    main = jnp.concatenate([qa, ka, qb, kb, qc, kc, vc, gc], axis=1).astype(BF16)
    f = jnp.pad(fb, ((0, 0), (0, LANES - fb.shape[1]))).astype(BF16)
    vt = jnp.concatenate([va, vb], axis=1).T.astype(BF16)
    return main, f, vt


def _pad_lanes(v):
    return jnp.pad(v, (0, LANES - v.shape[0]))[None, :]


class Tiles(NamedTuple):
    rows: int
    attn: int
    scan: int
    ret_chunk: int
    ff_chunk: int
    moe_tile: int
    moe_ff_chunk: int
    moe_sub_chunk: int


TILES = Tiles(rows=512, attn=512, scan=512, ret_chunk=256, ff_chunk=256,
              moe_tile=1024, moe_ff_chunk=1792, moe_sub_chunk=256)


def _mixer(h, b, s, norm_g, w_in, b_forget, lq1, lk1, lq2, lk2, subln, ret_gn, layer_idx, t):
    w_main, w_f, w_vt = _prep_w_in(w_in)
    proj, f, vt = norm_proj(h, norm_g[None, :], w_main, w_f, w_vt, tm=t.rows, seq=s)
    proj = proj.reshape(b, s, -1)
    cum_col, cum_row = fox_prep(f.reshape(b, s, LANES), _pad_lanes(b_forget), blk=t.scan)
    lam_init = 0.8 - 0.6 * math.exp(-0.3 * layer_idx)
    lam = jnp.exp(jnp.sum(lq1 * lk1)) - jnp.exp(jnp.sum(lq2 * lk2)) + lam_init
    oa = diff_attention(proj, vt, lam.reshape(1), subln[None, :], tq=t.attn,
                        out_scale=1.0 - lam_init)
    ob = fox_attention(proj, vt, cum_col, cum_row, tq=t.attn)
    oc = retention(proj, ret_gn[None, :], chunk=t.ret_chunk)
    n = b * s
    return oa.reshape(n, -1), ob.reshape(n, -1), oc.reshape(n, -1)


def _route_plan(meta, counts, tile, n_slots, tm):
    n = meta.shape[0]
    idx = meta[:, 0:2].astype(jnp.int32)
    rank = meta[:, 2:4].astype(jnp.int32)
    cnt = counts[0, :N_EXPERTS].astype(jnp.int32)
    padded = ((cnt + tile - 1) // tile) * tile
    ends = jnp.cumsum(padded)
    starts = ends - padded
    pos = starts[idx] + rank
    n_tiles = ends[-1] // tile
    tiles = jnp.arange(n_slots // tile, dtype=jnp.int32)
    tile_expert = jnp.sum(tiles[:, None] >= (ends // tile)[None, :], axis=1)
    last = jnp.sum(n_tiles - 1 >= ends // tile)
    tile_expert = jnp.where(tiles < n_tiles, tile_expert, last).astype(jnp.int32)
    tile_expert = jnp.clip(tile_expert, 0, N_EXPERTS - 1)
    pos = jnp.clip(pos, 0, n_slots - 1)
    pos = pos.reshape(n // tm, tm, 2).transpose(0, 2, 1)
    last_tile_start = jnp.concatenate([jnp.clip(ends - tile, 0, n_slots - tile),
                                       jnp.clip(ends[-1:], 0, n_slots)]).astype(jnp.int32)
    return pos, tile_expert, n_tiles.reshape(1).astype(jnp.int32), last_tile_start


def forward(x, p, norm_mix, w_in, b_forget, lambda_q1, lambda_k1, lambda_q2, lambda_k2,
            diff_subln, ret_gn, w_out, norm_ffn, dense_w_gate, dense_w_up, dense_w_down,
            router, moe_w_gate, moe_w_up, moe_w_down, ple_norm, ple_gate, ple_proj, final_norm,
            t=TILES):
    assert w_in.shape[0] == 2, "two layers: dense SwiGLU, then routed experts"
    b, s, d = x.shape
    n = b * s
    h = x.reshape(n, d)
    pf = p.reshape(p.shape[0], n, p.shape[-1])
    bf = lambda a: a.astype(BF16)
    row = lambda a: a[None, :]

    oa, ob, oc = _mixer(h, b, s, norm_mix[0], w_in[0], b_forget[0], lambda_q1[0], lambda_k1[0],
                        lambda_q2[0], lambda_k2[0], diff_subln[0], ret_gn[0], 0, t)
    h = dense_layer(h, oa, ob, oc, pf[0], bf(w_out[0]), row(norm_ffn[0]), bf(dense_w_gate[0]),
                    bf(dense_w_up[0]), bf(dense_w_down[0]), row(ple_norm[0]), bf(ple_gate[0]),
                    bf(ple_proj[0]), tm=t.rows, ff_chunk=t.ff_chunk)

    oa, ob, oc = _mixer(h, b, s, norm_mix[1], w_in[1], b_forget[1], lambda_q1[1], lambda_k1[1],
                        lambda_q2[1], lambda_k2[1], diff_subln[1], ret_gn[1], 1, t)
    wr = jnp.pad(router[0], ((0, 0), (0, LANES - N_EXPERTS)))
    h1, xn, meta, counts = route_tokens(h, oa, ob, oc, bf(w_out[1]), row(norm_ffn[1]), wr,
                                         tm=t.rows)
    n_slots = 2 * n + N_EXPERTS * t.moe_tile
    pos, tile_expert, n_tiles, last_tile_start = _route_plan(meta, counts, t.moe_tile, n_slots,
                                                             t.rows)
    xs = dispatch(xn, pos, last_tile_start, n_slots, tm=t.rows, tile=t.moe_tile)
    ys = experts(xs, tile_expert, n_tiles, bf(moe_w_gate[0]), bf(moe_w_up[0]), bf(moe_w_down[0]),
                 tile=t.moe_tile, ff_chunk=t.moe_ff_chunk, sub_chunk=t.moe_sub_chunk)
    out = combine(pos, h1, meta, pf[1], row(ple_norm[1]), bf(ple_gate[1]), bf(ple_proj[1]),
                  row(final_norm), ys, tm=t.rows)
    return out.reshape(b, s, d)


def kernel(x, p, norm_mix, w_in, b_forget, lambda_q1, lambda_k1, lambda_q2, lambda_k2,
           diff_subln, ret_gn, w_out, norm_ffn, dense_w_gate, dense_w_up, dense_w_down,
           router, moe_w_gate, moe_w_up, moe_w_down, ple_norm, ple_gate, ple_proj, final_norm):
    return forward(x, p, norm_mix, w_in, b_forget, lambda_q1, lambda_k1, lambda_q2, lambda_k2,
                   diff_subln, ret_gn, w_out, norm_ffn, dense_w_gate, dense_w_up, dense_w_down,
                   router, moe_w_gate, moe_w_up, moe_w_down, ple_norm, ple_gate, ple_proj,
                   final_norm)
```

```python
import functools
import math
from typing import NamedTuple

import numpy as np
import jax
import jax.numpy as jnp
from jax import lax
from jax.experimental import pallas as pl
from jax.experimental.pallas import tpu as pltpu

F32 = jnp.float32
BF16 = jnp.bfloat16

LANES = 128
ROW_UNROLL = 8
VMEM_LIMIT_BYTES = 56 * 1024 * 1024

HEAD_DIM = 64
DIFF_HEADS = 4
FOX_HEADS = 4
RET_HEADS = 4
RET_THETA = 10000.0
N_EXPERTS = 8
EPS = 1e-6
NEG_INF = -1e30
LOG2_E = math.log2(math.e)

QA, KA = 0, 4
QB, KB = 8, 10
QC, KC, VC, GC = 12, 14, 16, 18
VT_A, VT_B = 0, 4


def _params(semantics):
    return pltpu.CompilerParams(dimension_semantics=semantics,
                                vmem_limit_bytes=VMEM_LIMIT_BYTES)


def _resident(shape):
    nd = len(shape)
    return pl.BlockSpec(shape, lambda *_: (0,) * nd, pipeline_mode=pl.Buffered(1))


def _rms(x, g):
    return x * lax.rsqrt(jnp.mean(x * x, axis=-1, keepdims=True) + EPS) * g


def _sigmoid(x):
    return 1.0 / (1.0 + jnp.exp(-x))


def _dot(a, b):
    return jnp.dot(a, b, preferred_element_type=F32)


def _dot_nt(a, b):
    return lax.dot_general(a, b, (((1,), (1,)), ((), ())), preferred_element_type=F32)


def _lane_masks(shape):
    lane = lax.broadcasted_iota(jnp.int32, shape, len(shape) - 1)
    return lane < HEAD_DIM


def _norm_proj_kernel(h_ref, g_ref, w_ref, wf_ref, wvt_ref, out_ref, f_ref, vt_ref, *,
                      col_chunk):
    xn = _rms(h_ref[...], g_ref[...]).astype(BF16)
    ncol = w_ref.shape[1]
    for c in range(0, ncol, col_chunk):
        w = min(col_chunk, ncol - c)
        out_ref[:, c:c + w] = _dot(xn, w_ref[:, c:c + w]).astype(out_ref.dtype)
    f_ref[...] = _dot(xn, wf_ref[...])
    vt_ref[...] = _dot_nt(wvt_ref[...], xn).astype(vt_ref.dtype)


def norm_proj(h, g, w_main, w_f, w_vt, *, tm, seq):
    n, d = h.shape
    ncol = w_main.shape[1]
    nvt = w_vt.shape[0]
    per_seq = seq // tm
    return pl.pallas_call(
        functools.partial(_norm_proj_kernel, col_chunk=512),
        grid=(n // tm,),
        in_specs=[pl.BlockSpec((tm, d), lambda i: (i, 0)),
                  _resident((1, d)), _resident((d, ncol)), _resident((d, LANES)),
                  _resident((nvt, d))],
        out_specs=[pl.BlockSpec((tm, ncol), lambda i: (i, 0)),
                   pl.BlockSpec((tm, LANES), lambda i: (i, 0)),
                   pl.BlockSpec((None, nvt, tm), lambda i: (i // per_seq, 0, i % per_seq))],
        out_shape=[jax.ShapeDtypeStruct((n, ncol), BF16),
                   jax.ShapeDtypeStruct((n, LANES), F32),
                   jax.ShapeDtypeStruct((n // seq, nvt, seq), BF16)],
        compiler_params=_params(("arbitrary",)),
        name="norm_proj",
    )(h, g, w_main, w_f, w_vt)


def _fox_prep_kernel(f_ref, b_ref, col_ref, row_ref, carry_ref):
    @pl.when(pl.program_id(1) == 0)
    def _():
        carry_ref[...] = jnp.zeros_like(carry_ref)

    x = f_ref[...] + b_ref[...]
    log_f = (jnp.minimum(x, 0.0) - jnp.log1p(jnp.exp(-jnp.abs(x)))) * LOG2_E
    blk = x.shape[0]
    r = lax.broadcasted_iota(jnp.int32, (blk, blk), 0)
    c = lax.broadcasted_iota(jnp.int32, (blk, blk), 1)
    tri = (c <= r).astype(F32)
    cum = jnp.dot(tri, log_f, precision=lax.Precision.HIGHEST,
                  preferred_element_type=F32) + carry_ref[...]
    col_ref[...] = cum
    row_ref[...] = cum.T[:8, :]
    carry_ref[...] = cum[blk - 1:blk, :]


def fox_prep(f, b_pad, *, blk):
    bsz, s, _ = f.shape
    return pl.pallas_call(
        _fox_prep_kernel,
        grid=(bsz, s // blk),
        in_specs=[pl.BlockSpec((None, blk, LANES), lambda b, j: (b, j, 0)),
                  pl.BlockSpec((1, LANES), lambda b, j: (0, 0))],
        out_specs=[pl.BlockSpec((None, blk, LANES), lambda b, j: (b, j, 0)),
                   pl.BlockSpec((None, 8, blk), lambda b, j: (b, 0, j))],
        out_shape=[jax.ShapeDtypeStruct((bsz, s, LANES), F32),
                   jax.ShapeDtypeStruct((bsz, 8, s), F32)],
        scratch_shapes=[pltpu.VMEM((1, LANES), F32)],
        compiler_params=_params(("arbitrary", "arbitrary")),
        name="fox_prep",
    )(f, b_pad)


def _attn_kernel(*refs, mode, tq, out_scale):
    if mode == "diff":
        lam_ref, q_ref, k_ref, vt_ref, gain_ref, o_ref = refs[:6]
    else:
        q_ref, k_ref, vt_ref, ccol_ref, crow_ref, o_ref = refs[:6]
        cs_scr = refs[16]
    state = (refs[6:9], refs[9:12])
    st_a, st_b = refs[12:14], refs[14:16]
    qi = pl.program_id(2)
    pair = pl.program_id(1)

    q = q_ref[...]
    lo = _lane_masks(q.shape)
    zero = jnp.zeros_like(q)
    q_maps = (jnp.where(lo, q, zero), jnp.where(lo, zero, q))

    for m_ref, l_ref, acc_ref in state:
        m_ref[...] = jnp.full_like(m_ref, NEG_INF)
        l_ref[...] = jnp.zeros_like(l_ref)
        acc_ref[...] = jnp.zeros_like(acc_ref)

    if mode == "fox":
        @pl.when(qi == 0)
        def _():
            ccol = ccol_ref[...]
            lane = lax.broadcasted_iota(jnp.int32, ccol.shape, 1)
            for i in range(2):
                col = jnp.sum(jnp.where(lane == 2 * pair + i, ccol, 0.0), axis=-1, keepdims=True)
                cs_scr[i] = jnp.broadcast_to(col, ccol.shape)

        ct = tuple(crow_ref[pl.ds(2 * pair + i, 1), :] for i in range(2))

    def scores(c, bufs):
        k = k_ref[pl.ds(pl.multiple_of(c * tq, tq), tq), :]
        for i in range(2):
            bufs[i][...] = _dot_nt(k, q_maps[i])

    def consume(c, bufs, masked):
        start = pl.multiple_of(c * tq, tq)
        vt = vt_ref[:, pl.ds(start, tq)]
        for i in range(2):
            m_ref, l_ref, acc_ref = state[i]
            st = bufs[i][...]
            if mode == "fox":
                cs = cs_scr[i, pl.ds(start, tq), :]
                st = st - jnp.tile(cs, (1, tq // LANES))
            if masked:
                key = lax.broadcasted_iota(jnp.int32, st.shape, 0)
                qry = lax.broadcasted_iota(jnp.int32, st.shape, 1)
                st = jnp.where(key <= qry, st, NEG_INF)
            m_old = m_ref[...]
            m_chunk = jnp.max(st, axis=0, keepdims=True)
            if mode == "fox":
                m_new = jnp.maximum(m_old, m_chunk + ct[i])
                p = jnp.exp2(st + (ct[i] - m_new))
            else:
                m_new = jnp.maximum(m_old, m_chunk)
                p = jnp.exp2(st - m_new)
            alpha = jnp.exp2(m_old - m_new)
            l_ref[...] = alpha * l_ref[...] + jnp.sum(p, axis=0, keepdims=True)
            acc_ref[...] = alpha * acc_ref[...] + _dot(vt, p.astype(BF16))
            m_ref[...] = m_new

    def body(j, carry):
        scores(2 * j + 1, st_b)
        consume(2 * j, st_a, False)
        scores(2 * j + 2, st_a)
        consume(2 * j + 1, st_b, False)
        return carry

    scores(0, st_a)
    lax.fori_loop(0, qi // 2, body, 0)

    @pl.when(qi % 2 == 0)
    def _():
        consume(qi, st_a, True)

    @pl.when(qi % 2 == 1)
    def _():
        scores(qi, st_b)
        consume(qi - 1, st_a, False)
        consume(qi, st_b, True)

    o1, o2 = (acc_ref[...] * (1.0 / l_ref[...]) for _, l_ref, acc_ref in state)
    if mode == "diff":
        o = (o1 - lam_ref[0] * o2).T
        o = _rms(o, gain_ref[...]) * out_scale
    else:
        chan = lax.broadcasted_iota(jnp.int32, o1.shape, 0)
        o = jnp.where(chan < HEAD_DIM, o1, o2).T
    o_ref[...] = o.astype(o_ref.dtype)


def _attn_scratch(tq):
    per_map = [pltpu.VMEM((1, tq), F32), pltpu.VMEM((1, tq), F32), pltpu.VMEM((LANES, tq), F32)]
    scores = [pltpu.VMEM((tq, tq), F32)] * 4
    return per_map + per_map + scores


def diff_attention(proj, vt, lam, gain, *, tq, out_scale):
    bsz, s, _ = proj.shape
    kern = functools.partial(_attn_kernel, mode="diff", tq=tq, out_scale=out_scale)
    return pl.pallas_call(
        kern,
        grid=(bsz, DIFF_HEADS, s // tq),
        in_specs=[pl.BlockSpec(memory_space=pltpu.SMEM),
                  pl.BlockSpec((None, tq, LANES), lambda b, h, i: (b, i, QA + h)),
                  pl.BlockSpec((None, s, LANES), lambda b, h, i: (b, 0, KA + h)),
                  pl.BlockSpec((None, LANES, s), lambda b, h, i: (b, VT_A + h, 0)),
                  pl.BlockSpec((1, LANES), lambda b, h, i: (0, 0))],
        out_specs=pl.BlockSpec((None, tq, LANES), lambda b, h, i: (b, i, h)),
        out_shape=jax.ShapeDtypeStruct((bsz, s, DIFF_HEADS * LANES), BF16),
        scratch_shapes=_attn_scratch(tq),
        compiler_params=_params(("arbitrary", "arbitrary", "arbitrary")),
        name="diff_attention",
    )(lam, proj, proj, vt, gain)


def fox_attention(proj, vt, cum_col, cum_row, *, tq):
    bsz, s, _ = proj.shape
    npair = FOX_HEADS // 2
    kern = functools.partial(_attn_kernel, mode="fox", tq=tq, out_scale=None)
    return pl.pallas_call(
        kern,
        grid=(bsz, npair, s // tq),
        in_specs=[pl.BlockSpec((None, tq, LANES), lambda b, h, i: (b, i, QB + h)),
                  pl.BlockSpec((None, s, LANES), lambda b, h, i: (b, 0, KB + h)),
                  pl.BlockSpec((None, LANES, s), lambda b, h, i: (b, VT_B + h, 0)),
                  pl.BlockSpec((None, s, LANES), lambda b, h, i: (b, 0, 0)),
                  pl.BlockSpec((None, 8, tq), lambda b, h, i: (b, 0, i))],
        out_specs=pl.BlockSpec((None, tq, LANES), lambda b, h, i: (b, i, h)),
        out_shape=jax.ShapeDtypeStruct((bsz, s, npair * LANES), BF16),
        scratch_shapes=_attn_scratch(tq) + [pltpu.VMEM((2, s, LANES), F32)],
        compiler_params=_params(("arbitrary", "arbitrary", "arbitrary")),
        name="fox_attention",
    )(proj, proj, vt, cum_col, cum_row)


def _retention_tables(s, chunk):
    half = HEAD_DIM // 2
    inv = RET_THETA ** (-np.arange(half, dtype=np.float64) / half)
    ang = np.arange(s, dtype=np.float64)[:, None] * inv[None, :]
    cos = np.tile(np.cos(ang), (1, 4))
    sin = np.tile(np.concatenate([-np.sin(ang), np.sin(ang)], axis=1), (1, 2))
    log_g = np.log1p(-np.exp2(-5.0 - np.arange(RET_HEADS, dtype=np.float64)))
    j = np.arange(chunk, dtype=np.float64)
    diff = j[:, None] - j[None, :]
    inner = np.where(diff[None] >= 0, np.exp(np.where(diff[None] >= 0, diff[None], 0.0)
                                             * log_g[:, None, None]), 0.0)
    q_dec = np.exp((j + 1.0)[None, :] * log_g[:, None])
    k_dec = np.exp((chunk - 1.0 - j)[None, :] * log_g[:, None])
    c_dec = np.exp(chunk * log_g)
    npair = RET_HEADS // 2

    def lanes(t):
        t = t.reshape(npair, 2, chunk)
        return np.repeat(t.transpose(0, 2, 1), HEAD_DIM, axis=2)

    blk = np.kron(np.eye(2), np.ones((HEAD_DIM, HEAD_DIM)))
    c_mat = np.stack([np.kron(np.diag(c_dec[2 * p:2 * p + 2]), np.ones((HEAD_DIM, HEAD_DIM)))
                      for p in range(npair)])
    f = lambda a: jnp.asarray(a, F32)
    return dict(cos=f(cos), sin=f(sin), inner=f(inner.reshape(npair, 2, chunk, chunk)),
                q_dec=f(lanes(q_dec)), k_dec=f(lanes(k_dec)), c_mat=f(c_mat), blk=f(blk))


def _retention_kernel(q_ref, k_ref, v_ref, g_ref, cos_ref, sin_ref, inner_ref, qdec_ref,
                      kdec_ref, cmat_ref, blk_ref, gain_ref, o_ref, state_ref):
    @pl.when(pl.program_id(1) == 0)
    def _():
        state_ref[...] = jnp.zeros_like(state_ref)

    cos = cos_ref[...]
    sin = sin_ref[...]
    lane = lax.broadcasted_iota(jnp.int32, cos.shape, 1)
    first_half = (lane % HEAD_DIM) < (HEAD_DIM // 2)
    lo = lane < HEAD_DIM

    def rope(x):
        x = x.astype(F32)
        swapped = jnp.where(first_half, pltpu.roll(x, LANES - HEAD_DIM // 2, 1),
                            pltpu.roll(x, HEAD_DIM // 2, 1))
        return x * cos + swapped * sin

    for p in range(RET_HEADS // 2):
        cols = slice(p * LANES, (p + 1) * LANES)
        qr = rope(q_ref[:, cols])
        kr = rope(k_ref[:, cols])
        v = v_ref[:, cols]
        qb = qr.astype(BF16)
        kb = kr.astype(BF16)
        zero = jnp.zeros_like(qb)
        a1 = (_dot_nt(jnp.where(lo, qb, zero), kb) * inner_ref[p, 0]).astype(BF16)
        a2 = (_dot_nt(jnp.where(lo, zero, qb), kb) * inner_ref[p, 1]).astype(BF16)
        state = state_ref[p]
        o = jnp.where(lo, _dot(a1, v), _dot(a2, v))
        o = o + _dot((qr * qdec_ref[p]).astype(BF16), state.astype(BF16))
        kd = (kr * kdec_ref[p]).astype(BF16)
        state_ref[p] = state * cmat_ref[p] + _dot(kd.T, v) * blk_ref[...]

        inv_n = 1.0 / HEAD_DIM
        sum_lo = jnp.sum(jnp.where(lo, o, 0.0), axis=-1, keepdims=True)
        sum_hi = jnp.sum(jnp.where(lo, 0.0, o), axis=-1, keepdims=True)
        d = o - jnp.where(lo, sum_lo, sum_hi) * inv_n
        d2 = d * d
        var_lo = jnp.sum(jnp.where(lo, d2, 0.0), axis=-1, keepdims=True)
        var_hi = jnp.sum(jnp.where(lo, 0.0, d2), axis=-1, keepdims=True)
        y = d * lax.rsqrt(jnp.where(lo, var_lo, var_hi) * inv_n + EPS)
        g = g_ref[:, cols].astype(F32)
        o_ref[:, cols] = (g * _sigmoid(g) * (y * gain_ref[:, cols])).astype(o_ref.dtype)


def retention(proj, gain, *, chunk):
    bsz, s, _ = proj.shape
    npair = RET_HEADS // 2
    t = _retention_tables(s, chunk)

    width = npair * LANES

    def act(off):
        return pl.BlockSpec((None, chunk, width), lambda b, c: (b, c, off // npair))

    pos = pl.BlockSpec((chunk, LANES), lambda b, c: (c, 0))
    return pl.pallas_call(
        _retention_kernel,
        grid=(bsz, s // chunk),
        in_specs=[act(QC), act(KC), act(VC), act(GC), pos, pos,
                  _resident((npair, 2, chunk, chunk)), _resident((npair, chunk, LANES)),
                  _resident((npair, chunk, LANES)), _resident((npair, LANES, LANES)),
                  _resident((LANES, LANES)), _resident((1, width))],
        out_specs=pl.BlockSpec((None, chunk, width), lambda b, c: (b, c, 0)),
        out_shape=jax.ShapeDtypeStruct((bsz, s, width), BF16),
        scratch_shapes=[pltpu.VMEM((npair, LANES, LANES), F32)],
        compiler_params=_params(("arbitrary", "arbitrary")),
        name="retention",
    )(proj, proj, proj, proj, t["cos"], t["sin"], t["inner"], t["q_dec"], t["k_dec"],
      t["c_mat"], t["blk"], gain)


def _mix_residual(h_ref, oa_ref, ob_ref, oc_ref, wo_ref):
    wa = oa_ref.shape[1]
    wb = ob_ref.shape[1]
    return (h_ref[...] + _dot(oa_ref[...], wo_ref[0:wa, :])
            + _dot(ob_ref[...], wo_ref[wa:wa + wb, :])
            + _dot(oc_ref[...], wo_ref[wa + wb:, :]))


def _swiglu_accumulate(xn, wg_ref, wu_ref, wd_ref, acc_ref, chunk):
    def body(i, carry):
        c = pl.multiple_of(i * chunk, chunk)
        g = _dot(xn, wg_ref[:, pl.ds(c, chunk)])
        u = _dot(xn, wu_ref[:, pl.ds(c, chunk)])
        a = (g * _sigmoid(g) * u).astype(BF16)
        acc_ref[...] += _dot(a, wd_ref[pl.ds(c, chunk), :])
        return carry

    lax.fori_loop(0, wg_ref.shape[1] // chunk, body, 0)


def _ple(h, p_ref, pn_ref, pg_ref, pp_ref):
    gate = _sigmoid(_dot(_rms(h, pn_ref[...]).astype(BF16), pg_ref[...]))
    return h + gate * _dot(p_ref[...].astype(BF16), pp_ref[...])


def _dense_layer_kernel(h_ref, oa_ref, ob_ref, oc_ref, p_ref, wo_ref, gn_ref, wg_ref, wu_ref,
                        wd_ref, pn_ref, pg_ref, pp_ref, out_ref, acc_ref, *, ff_chunk):
    h1 = _mix_residual(h_ref, oa_ref, ob_ref, oc_ref, wo_ref)
    xn = _rms(h1, gn_ref[...]).astype(BF16)
    acc_ref[...] = h1
    _swiglu_accumulate(xn, wg_ref, wu_ref, wd_ref, acc_ref, ff_chunk)
    out_ref[...] = _ple(acc_ref[...], p_ref, pn_ref, pg_ref, pp_ref)


def dense_layer(h, oa, ob, oc, p, wo, gn, wg, wu, wd, pn, pg, pp, *, tm, ff_chunk):
    n, d = h.shape
    row = lambda w: pl.BlockSpec((tm, w), lambda i: (i, 0))
    weights = (wo, gn, wg, wu, wd, pn, pg, pp)
    return pl.pallas_call(
        functools.partial(_dense_layer_kernel, ff_chunk=ff_chunk),
        grid=(n // tm,),
        in_specs=[row(d), row(oa.shape[1]), row(ob.shape[1]), row(oc.shape[1]), row(p.shape[1])]
                 + [_resident(w.shape) for w in weights],
        out_specs=row(d),
        out_shape=jax.ShapeDtypeStruct((n, d), F32),
        scratch_shapes=[pltpu.VMEM((tm, d), F32)],
        compiler_params=_params(("arbitrary",)),
        name="dense_layer",
    )(h, oa, ob, oc, p, *weights)


def _router_kernel(h_ref, oa_ref, ob_ref, oc_ref, wo_ref, gn_ref, wr_ref,
                   h1_ref, xn_ref, meta_ref, cnt_ref):
    @pl.when(pl.program_id(0) == 0)
    def _():
        cnt_ref[...] = jnp.zeros_like(cnt_ref)

    h1 = _mix_residual(h_ref, oa_ref, ob_ref, oc_ref, wo_ref)
    h1_ref[...] = h1
    xn = _rms(h1, gn_ref[...])
    xn_ref[...] = xn

    logits = jnp.dot(xn, wr_ref[...], precision=lax.Precision.HIGHEST,
                     preferred_element_type=F32)
    tm = logits.shape[0]
    lane = lax.broadcasted_iota(jnp.int32, logits.shape, 1)
    logits = jnp.where(lane < N_EXPERTS, logits, -jnp.inf)

    def top(vals):
        best = jnp.max(vals, axis=-1, keepdims=True)
        idx = jnp.min(jnp.where(vals == best, lane, LANES), axis=-1, keepdims=True)
        return best, idx

    v1, i1 = top(logits)
    v2, i2 = top(jnp.where(lane == i1, -jnp.inf, logits))
    e = jnp.exp(v2 - v1)
    g1 = 1.0 / (1.0 + e)
    g2 = e / (1.0 + e)

    oh1 = (lane == i1).astype(F32)
    oh2 = (lane == i2).astype(F32)
    both = oh1 + oh2
    r = lax.broadcasted_iota(jnp.int32, (tm, tm), 0)
    c = lax.broadcasted_iota(jnp.int32, (tm, tm), 1)
    before = _dot((c < r).astype(BF16), both.astype(BF16)) + cnt_ref[...]
    rank1 = jnp.sum(before * oh1, axis=-1, keepdims=True)
    rank2 = jnp.sum(before * oh2, axis=-1, keepdims=True)
    cnt_ref[...] += jnp.sum(both, axis=0, keepdims=True)

    meta = jnp.zeros(logits.shape, F32)
    for k, val in enumerate((i1.astype(F32), i2.astype(F32), rank1, rank2, g1, g2)):
        meta = jnp.where(lane == k, val, meta)
    meta_ref[...] = meta


def route_tokens(h, oa, ob, oc, wo, gn, wr, *, tm):
    n, d = h.shape
    row = lambda w: pl.BlockSpec((tm, w), lambda i: (i, 0))
    weights = (wo, gn, wr)
    return pl.pallas_call(
        _router_kernel,
        grid=(n // tm,),
        in_specs=[row(d), row(oa.shape[1]), row(ob.shape[1]), row(oc.shape[1])]
                 + [_resident(w.shape) for w in weights],
        out_specs=[row(d), row(d), row(LANES), pl.BlockSpec((1, LANES), lambda i: (0, 0))],
        out_shape=[jax.ShapeDtypeStruct((n, d), F32),
                   jax.ShapeDtypeStruct((n, d), F32),
                   jax.ShapeDtypeStruct((n, LANES), F32),
                   jax.ShapeDtypeStruct((1, LANES), F32)],
        compiler_params=_params(("arbitrary",)),
        name="router",
    )(h, oa, ob, oc, *weights)


def _dispatch_kernel(last_ref, pos_ref, x_ref, xs_ref, zero_scr, sem, zero_sem, *, tile):
    tm = x_ref.shape[0]
    zrows = zero_scr.shape[0]

    @pl.when(pl.program_id(0) == 0)
    def _():
        zero_scr[...] = jnp.zeros_like(zero_scr)

        def zero_copy(e, q):
            start = pl.multiple_of(last_ref[e] + q * zrows, zrows)
            return pltpu.make_async_copy(zero_scr, xs_ref.at[pl.ds(start, zrows), :], zero_sem)

        for e in range(N_EXPERTS):
            for q in range(tile // zrows):
                zero_copy(e, q).start()
        for e in range(N_EXPERTS):
            for q in range(tile // zrows):
                zero_copy(e, q).wait()

        n_trailing = (xs_ref.shape[0] - last_ref[N_EXPERTS]) // zrows

        def trailing(q, carry, wait):
            cp = zero_copy(N_EXPERTS, q)
            cp.wait() if wait else cp.start()
            return carry

        lax.fori_loop(0, n_trailing, functools.partial(trailing, wait=False), 0)
        lax.fori_loop(0, n_trailing, functools.partial(trailing, wait=True), 0)

    def row_copy(r, k):
        return pltpu.make_async_copy(x_ref.at[pl.ds(r, 1), :],
                                     xs_ref.at[pl.ds(pos_ref[0, k, r], 1), :], sem)

    _start_rows(tm, row_copy)
    for k in range(2):
        pltpu.make_async_copy(x_ref, xs_ref.at[pl.ds(0, tm), :], sem).wait()


def _start_rows(n_rows, row_copy):
    def body(g, carry):
        base = pl.multiple_of(g * ROW_UNROLL, ROW_UNROLL)
        for j in range(ROW_UNROLL):
            row_copy(base + j, 0).start(priority=0)
            row_copy(base + j, 1).start(priority=1)
        return carry

    lax.fori_loop(0, n_rows // ROW_UNROLL, body, 0)


def dispatch(xn, pos, last_tile_start, n_slots, *, tm, tile):
    n, w = xn.shape
    zrows = min(tile, 256)
    grid_spec = pltpu.PrefetchScalarGridSpec(
        num_scalar_prefetch=1,
        grid=(n // tm,),
        in_specs=[pl.BlockSpec((1, 2, tm), lambda i, last: (i, 0, 0), memory_space=pltpu.SMEM),
                  pl.BlockSpec((tm, w), lambda i, last: (i, 0))],
        out_specs=pl.BlockSpec(memory_space=pl.ANY),
        scratch_shapes=[pltpu.VMEM((zrows, w), xn.dtype), pltpu.SemaphoreType.DMA(()),
                        pltpu.SemaphoreType.DMA(())],
    )
    return pl.pallas_call(
        functools.partial(_dispatch_kernel, tile=tile),
        grid_spec=grid_spec,
        out_shape=jax.ShapeDtypeStruct((n_slots, w), xn.dtype),
        compiler_params=_params(("arbitrary",)),
        name="moe_dispatch",
    )(last_tile_start, pos, xn)


def _experts_kernel(te_ref, nt_ref, x_ref, wg_ref, wu_ref, wd_ref, o_ref, x_scr, *, sub_chunk):
    del te_ref
    t = pl.program_id(0)
    c = pl.program_id(1)

    @pl.when(c == 0)
    def _():
        o_ref[...] = jnp.zeros_like(o_ref)

    @pl.when(t < nt_ref[0])
    def _():
        @pl.when(c == 0)
        def _():
            x_scr[...] = x_ref[...].astype(BF16)

        _swiglu_accumulate(x_scr[...], wg_ref, wu_ref, wd_ref, o_ref, sub_chunk)


def experts(xs, tile_expert, n_tiles, wg, wu, wd, *, tile, ff_chunk, sub_chunk):
    n_slots, d = xs.shape
    d_ff = wg.shape[2]
    nc = d_ff // ff_chunk
    grid_tiles = n_slots // tile

    def tile_idx(t, c, te, nt):
        return (jnp.maximum(jnp.minimum(t, nt[0] - 1), 0), 0)

    def chunk_idx(t, c, nt):
        return jnp.where(t < nt[0], c, nc - 1)

    grid_spec = pltpu.PrefetchScalarGridSpec(
        num_scalar_prefetch=2,
        grid=(grid_tiles, nc),
        in_specs=[pl.BlockSpec((tile, d), tile_idx),
                  pl.BlockSpec((None, d, ff_chunk), lambda t, c, te, nt: (te[t], 0, chunk_idx(t, c, nt))),
                  pl.BlockSpec((None, d, ff_chunk), lambda t, c, te, nt: (te[t], 0, chunk_idx(t, c, nt))),
                  pl.BlockSpec((None, ff_chunk, d), lambda t, c, te, nt: (te[t], chunk_idx(t, c, nt), 0))],
        out_specs=pl.BlockSpec((tile, d), lambda t, c, te, nt: (t, 0)),
        scratch_shapes=[pltpu.VMEM((tile, d), BF16)],
    )
    return pl.pallas_call(
        functools.partial(_experts_kernel, sub_chunk=sub_chunk),
        grid_spec=grid_spec,
        out_shape=jax.ShapeDtypeStruct((n_slots, d), F32),
        compiler_params=_params(("arbitrary", "arbitrary")),
        name="moe_experts",
    )(tile_expert, n_tiles, xs, wg, wu, wd)


def _combine_kernel(pos_ref, h_ref, meta_ref, p_ref, pn_ref, pg_ref, pp_ref, fn_ref, ys_ref,
                    out_ref, y1_scr, y2_scr, sem):
    tm = h_ref.shape[0]
    bufs = (y1_scr, y2_scr)

    def row_copy(r, k):
        return pltpu.make_async_copy(ys_ref.at[pl.ds(pos_ref[0, k, r], 1), :],
                                     bufs[k].at[pl.ds(r, 1), :], sem)

    _start_rows(tm, row_copy)
    for buf in bufs:
        pltpu.make_async_copy(ys_ref.at[pl.ds(0, tm), :], buf, sem).wait()

    meta = meta_ref[...]
    lane = lax.broadcasted_iota(jnp.int32, meta.shape, 1)
    g1 = jnp.sum(jnp.where(lane == 4, meta, 0.0), axis=-1, keepdims=True)
    g2 = jnp.sum(jnp.where(lane == 5, meta, 0.0), axis=-1, keepdims=True)
    h2 = h_ref[...] + g1 * y1_scr[...] + g2 * y2_scr[...]
    h3 = _ple(h2, p_ref, pn_ref, pg_ref, pp_ref)
    out_ref[...] = _rms(h3, fn_ref[...])


def combine(pos, h1, meta, p, pn, pg, pp, fn, ys, *, tm):
    n, d = h1.shape
    row = lambda w: pl.BlockSpec((tm, w), lambda i: (i, 0))
    weights = (pn, pg, pp, fn)
    return pl.pallas_call(
        _combine_kernel,
        grid=(n // tm,),
        in_specs=[pl.BlockSpec((1, 2, tm), lambda i: (i, 0, 0), memory_space=pltpu.SMEM),
                  row(d), row(LANES), row(p.shape[1])]
                 + [_resident(w.shape) for w in weights]
                 + [pl.BlockSpec(memory_space=pl.ANY)],
        out_specs=row(d),
        out_shape=jax.ShapeDtypeStruct((n, d), F32),
        scratch_shapes=[pltpu.VMEM((tm, d), F32), pltpu.VMEM((tm, d), F32),
                        pltpu.SemaphoreType.DMA(())],
        compiler_params=_params(("arbitrary",)),
        name="moe_combine",
    )(pos, h1, meta, p, *weights, ys)


def _prep_w_in(w):
    scale = HEAD_DIM ** -0.5
    qa, ka, va = w[:, 0:512] * (scale * LOG2_E), w[:, 512:1024], w[:, 1024:1536]
    qb, kb, vb = w[:, 1536:1792] * (scale * LOG2_E), w[:, 1792:2048], w[:, 2048:2304]
    fb = w[:, 2304:2308]
    qc, kc, vc, gc = w[:, 2308:2564], w[:, 2564:2820] * scale, w[:, 2820:3076], w[:, 3076:3332]
    main = jnp.concatenate([qa, ka, qb, kb, qc, kc, vc, gc], axis=1).astype(BF16)
    f = jnp.pad(fb, ((0, 0), (0, LANES - fb.shape[1]))).astype(BF16)
    vt = jnp.concatenate([va, vb], axis=1).T.astype(BF16)
    return main, f, vt


def _pad_lanes(v):
    return jnp.pad(v, (0, LANES - v.shape[0]))[None, :]


class Tiles(NamedTuple):
    rows: int
    attn: int
    scan: int
    ret_chunk: int
    ff_chunk: int
    moe_tile: int
    moe_ff_chunk: int
    moe_sub_chunk: int


TILES = Tiles(rows=1024, attn=512, scan=512, ret_chunk=256, ff_chunk=256,
              moe_tile=1024, moe_ff_chunk=1792, moe_sub_chunk=256)


def _mixer(h, b, s, norm_g, w_in, b_forget, lq1, lk1, lq2, lk2, subln, ret_gn, layer_idx, t):
    w_main, w_f, w_vt = _prep_w_in(w_in)
    proj, f, vt = norm_proj(h, norm_g[None, :], w_main, w_f, w_vt, tm=t.rows, seq=s)
    proj = proj.reshape(b, s, -1)
    cum_col, cum_row = fox_prep(f.reshape(b, s, LANES), _pad_lanes(b_forget), blk=t.scan)
    lam_init = 0.8 - 0.6 * math.exp(-0.3 * layer_idx)
    lam = jnp.exp(jnp.sum(lq1 * lk1)) - jnp.exp(jnp.sum(lq2 * lk2)) + lam_init
    oa = diff_attention(proj, vt, lam.reshape(1), subln[None, :], tq=t.attn,
                        out_scale=1.0 - lam_init)
    ob = fox_attention(proj, vt, cum_col, cum_row, tq=t.attn)
    oc = retention(proj, ret_gn[None, :], chunk=t.ret_chunk)
    n = b * s
    return oa.reshape(n, -1), ob.reshape(n, -1), oc.reshape(n, -1)


def _route_plan(meta, counts, tile, n_slots, tm):
    n = meta.shape[0]
    idx = meta[:, 0:2].astype(jnp.int32)
    rank = meta[:, 2:4].astype(jnp.int32)
    cnt = counts[0, :N_EXPERTS].astype(jnp.int32)
    padded = ((cnt + tile - 1) // tile) * tile
    ends = jnp.cumsum(padded)
    starts = ends - padded
    pos = starts[idx] + rank
    n_tiles = ends[-1] // tile
    tiles = jnp.arange(n_slots // tile, dtype=jnp.int32)
    tile_expert = jnp.sum(tiles[:, None] >= (ends // tile)[None, :], axis=1)
    last = jnp.sum(n_tiles - 1 >= ends // tile)
    tile_expert = jnp.where(tiles < n_tiles, tile_expert, last).astype(jnp.int32)
    tile_expert = jnp.clip(tile_expert, 0, N_EXPERTS - 1)
    pos = jnp.clip(pos, 0, n_slots - 1)
    pos = pos.reshape(n // tm, tm, 2).transpose(0, 2, 1)
    last_tile_start = jnp.concatenate([jnp.clip(ends - tile, 0, n_slots - tile),
                                       jnp.clip(ends[-1:], 0, n_slots)]).astype(jnp.int32)
    return pos, tile_expert, n_tiles.reshape(1).astype(jnp.int32), last_tile_start


def forward(x, p, norm_mix, w_in, b_forget, lambda_q1, lambda_k1, lambda_q2, lambda_k2,
            diff_subln, ret_gn, w_out, norm_ffn, dense_w_gate, dense_w_up, dense_w_down,
            router, moe_w_gate, moe_w_up, moe_w_down, ple_norm, ple_gate, ple_proj, final_norm,
            t=TILES):
    assert w_in.shape[0] == 2, "two layers: dense SwiGLU, then routed experts"
    b, s, d = x.shape
    n = b * s
    h = x.reshape(n, d)
    pf = p.reshape(p.shape[0], n, p.shape[-1])
    bf = lambda a: a.astype(BF16)
    row = lambda a: a[None, :]

    oa, ob, oc = _mixer(h, b, s, norm_mix[0], w_in[0], b_forget[0], lambda_q1[0], lambda_k1[0],
                        lambda_q2[0], lambda_k2[0], diff_subln[0], ret_gn[0], 0, t)
    h = dense_layer(h, oa, ob, oc, pf[0], bf(w_out[0]), row(norm_ffn[0]), bf(dense_w_gate[0]),
                    bf(dense_w_up[0]), bf(dense_w_down[0]), row(ple_norm[0]), bf(ple_gate[0]),
                    bf(ple_proj[0]), tm=t.rows, ff_chunk=t.ff_chunk)

    oa, ob, oc = _mixer(h, b, s, norm_mix[1], w_in[1], b_forget[1], lambda_q1[1], lambda_k1[1],
                        lambda_q2[1], lambda_k2[1], diff_subln[1], ret_gn[1], 1, t)
    wr = jnp.pad(router[0], ((0, 0), (0, LANES - N_EXPERTS)))
    h1, xn, meta, counts = route_tokens(h, oa, ob, oc, bf(w_out[1]), row(norm_ffn[1]), wr,
                                         tm=t.rows)
    n_slots = 2 * n + N_EXPERTS * t.moe_tile
    pos, tile_expert, n_tiles, last_tile_start = _route_plan(meta, counts, t.moe_tile, n_slots,
                                                             t.rows)
    xs = dispatch(xn, pos, last_tile_start, n_slots, tm=t.rows, tile=t.moe_tile)
    ys = experts(xs, tile_expert, n_tiles, bf(moe_w_gate[0]), bf(moe_w_up[0]), bf(moe_w_down[0]),
                 tile=t.moe_tile, ff_chunk=t.moe_ff_chunk, sub_chunk=t.moe_sub_chunk)
    out = combine(pos, h1, meta, pf[1], row(ple_norm[1]), bf(ple_gate[1]), bf(ple_proj[1]),
                  row(final_norm), ys, tm=t.rows)
    return out.reshape(b, s, d)


def kernel(x, p, norm_mix, w_in, b_forget, lambda_q1, lambda_k1, lambda_q2, lambda_k2,
           diff_subln, ret_gn, w_out, norm_ffn, dense_w_gate, dense_w_up, dense_w_down,
           router, moe_w_gate, moe_w_up, moe_w_down, ple_norm, ple_gate, ple_proj, final_norm):
    return forward(x, p, norm_mix, w_in, b_forget, lambda_q1, lambda_k1, lambda_q2, lambda_k2,
                   diff_subln, ret_gn, w_out, norm_ffn, dense_w_gate, dense_w_up, dense_w_down,
                   router, moe_w_gate, moe_w_up, moe_w_down, ple_norm, ple_gate, ple_proj,
                   final_norm)
```

```python
import functools
import math
from typing import NamedTuple

import numpy as np
import jax
import jax.numpy as jnp
from jax import lax
from jax.experimental import pallas as pl
from jax.experimental.pallas import tpu as pltpu

F32 = jnp.float32
BF16 = jnp.bfloat16

LANES = 128
ROW_UNROLL = 8
VMEM_LIMIT_BYTES = 56 * 1024 * 1024

HEAD_DIM = 64
DIFF_HEADS = 4
FOX_HEADS = 4
RET_HEADS = 4
RET_THETA = 10000.0
N_EXPERTS = 8
EPS = 1e-6
NEG_INF = -1e30
LOG2_E = math.log2(math.e)

QA, KA = 0, 4
QB, KB = 8, 10
QC, KC, VC, GC = 12, 14, 16, 18
VT_A, VT_B = 0, 4


def _params(semantics):
    return pltpu.CompilerParams(dimension_semantics=semantics,
                                vmem_limit_bytes=VMEM_LIMIT_BYTES)


def _resident(shape):
    nd = len(shape)
    return pl.BlockSpec(shape, lambda *_: (0,) * nd, pipeline_mode=pl.Buffered(1))


def _rms(x, g):
    return x * lax.rsqrt(jnp.mean(x * x, axis=-1, keepdims=True) + EPS) * g


def _sigmoid(x):
    return 1.0 / (1.0 + jnp.exp(-x))


def _dot(a, b):
    return jnp.dot(a, b, preferred_element_type=F32)


def _dot_nt(a, b):
    return lax.dot_general(a, b, (((1,), (1,)), ((), ())), preferred_element_type=F32)


def _lane_masks(shape):
    lane = lax.broadcasted_iota(jnp.int32, shape, len(shape) - 1)
    return lane < HEAD_DIM


def _norm_proj_kernel(h_ref, g_ref, w_ref, wf_ref, wvt_ref, out_ref, f_ref, vt_ref, *,
                      col_chunk):
    xn = _rms(h_ref[...], g_ref[...]).astype(BF16)
    ncol = w_ref.shape[1]
    for c in range(0, ncol, col_chunk):
        w = min(col_chunk, ncol - c)
        out_ref[:, c:c + w] = _dot(xn, w_ref[:, c:c + w]).astype(out_ref.dtype)
    f_ref[...] = _dot(xn, wf_ref[...])
    vt_ref[...] = _dot_nt(wvt_ref[...], xn).astype(vt_ref.dtype)


def norm_proj(h, g, w_main, w_f, w_vt, *, tm, seq):
    n, d = h.shape
    ncol = w_main.shape[1]
    nvt = w_vt.shape[0]
    per_seq = seq // tm
    return pl.pallas_call(
        functools.partial(_norm_proj_kernel, col_chunk=512),
        grid=(n // tm,),
        in_specs=[pl.BlockSpec((tm, d), lambda i: (i, 0)),
                  _resident((1, d)), _resident((d, ncol)), _resident((d, LANES)),
                  _resident((nvt, d))],
        out_specs=[pl.BlockSpec((tm, ncol), lambda i: (i, 0)),
                   pl.BlockSpec((tm, LANES), lambda i: (i, 0)),
                   pl.BlockSpec((None, nvt, tm), lambda i: (i // per_seq, 0, i % per_seq))],
        out_shape=[jax.ShapeDtypeStruct((n, ncol), BF16),
                   jax.ShapeDtypeStruct((n, LANES), F32),
                   jax.ShapeDtypeStruct((n // seq, nvt, seq), BF16)],
        compiler_params=_params(("arbitrary",)),
        name="norm_proj",
    )(h, g, w_main, w_f, w_vt)


def _fox_prep_kernel(f_ref, b_ref, col_ref, row_ref, carry_ref):
    @pl.when(pl.program_id(1) == 0)
    def _():
        carry_ref[...] = jnp.zeros_like(carry_ref)

    x = f_ref[...] + b_ref[...]
    log_f = (jnp.minimum(x, 0.0) - jnp.log1p(jnp.exp(-jnp.abs(x)))) * LOG2_E
    blk = x.shape[0]
    r = lax.broadcasted_iota(jnp.int32, (blk, blk), 0)
    c = lax.broadcasted_iota(jnp.int32, (blk, blk), 1)
    tri = (c <= r).astype(F32)
    cum = jnp.dot(tri, log_f, precision=lax.Precision.HIGHEST,
                  preferred_element_type=F32) + carry_ref[...]
    col_ref[...] = cum
    row_ref[...] = cum.T[:8, :]
    carry_ref[...] = cum[blk - 1:blk, :]


def fox_prep(f, b_pad, *, blk):
    bsz, s, _ = f.shape
    return pl.pallas_call(
        _fox_prep_kernel,
        grid=(bsz, s // blk),
        in_specs=[pl.BlockSpec((None, blk, LANES), lambda b, j: (b, j, 0)),
                  pl.BlockSpec((1, LANES), lambda b, j: (0, 0))],
        out_specs=[pl.BlockSpec((None, blk, LANES), lambda b, j: (b, j, 0)),
                   pl.BlockSpec((None, 8, blk), lambda b, j: (b, 0, j))],
        out_shape=[jax.ShapeDtypeStruct((bsz, s, LANES), F32),
                   jax.ShapeDtypeStruct((bsz, 8, s), F32)],
        scratch_shapes=[pltpu.VMEM((1, LANES), F32)],
        compiler_params=_params(("arbitrary", "arbitrary")),
        name="fox_prep",
    )(f, b_pad)


def _attn_kernel(*refs, mode, tq, out_scale):
    if mode == "diff":
        lam_ref, q_ref, k_ref, vt_ref, gain_ref, o_ref = refs[:6]
    else:
        q_ref, k_ref, vt_ref, ccol_ref, crow_ref, o_ref = refs[:6]
        cs_scr = refs[16]
    state = (refs[6:9], refs[9:12])
    st_a, st_b = refs[12:14], refs[14:16]
    qi = pl.program_id(2)
    pair = pl.program_id(1)

    q = q_ref[...]
    lo = _lane_masks(q.shape)
    zero = jnp.zeros_like(q)
    q_maps = (jnp.where(lo, q, zero), jnp.where(lo, zero, q))

    for m_ref, l_ref, acc_ref in state:
        m_ref[...] = jnp.full_like(m_ref, NEG_INF)
        l_ref[...] = jnp.zeros_like(l_ref)
        acc_ref[...] = jnp.zeros_like(acc_ref)

    if mode == "fox":
        @pl.when(qi == 0)
        def _():
            ccol = ccol_ref[...]
            lane = lax.broadcasted_iota(jnp.int32, ccol.shape, 1)
            for i in range(2):
                col = jnp.sum(jnp.where(lane == 2 * pair + i, ccol, 0.0), axis=-1, keepdims=True)
                cs_scr[i] = jnp.broadcast_to(col, ccol.shape)

        ct = tuple(crow_ref[pl.ds(2 * pair + i, 1), :] for i in range(2))

    def scores(c, bufs):
        k = k_ref[pl.ds(pl.multiple_of(c * tq, tq), tq), :]
        for i in range(2):
            bufs[i][...] = _dot_nt(k, q_maps[i])

    def consume(c, bufs, masked):
        start = pl.multiple_of(c * tq, tq)
        vt = vt_ref[:, pl.ds(start, tq)]
        for i in range(2):
            m_ref, l_ref, acc_ref = state[i]
            st = bufs[i][...]
            if mode == "fox":
                cs = cs_scr[i, pl.ds(start, tq), :]
                st = st - jnp.tile(cs, (1, tq // LANES))
            if masked:
                key = lax.broadcasted_iota(jnp.int32, st.shape, 0)
                qry = lax.broadcasted_iota(jnp.int32, st.shape, 1)
                st = jnp.where(key <= qry, st, NEG_INF)
            m_old = m_ref[...]
            m_chunk = jnp.max(st, axis=0, keepdims=True)
            if mode == "fox":
                m_new = jnp.maximum(m_old, m_chunk + ct[i])
                p = jnp.exp2(st + (ct[i] - m_new))
            else:
                m_new = jnp.maximum(m_old, m_chunk)
                p = jnp.exp2(st - m_new)
            alpha = jnp.exp2(m_old - m_new)
            l_ref[...] = alpha * l_ref[...] + jnp.sum(p, axis=0, keepdims=True)
            acc_ref[...] = alpha * acc_ref[...] + _dot(vt, p.astype(BF16))
            m_ref[...] = m_new

    def body(j, carry):
        scores(2 * j + 1, st_b)
        consume(2 * j, st_a, False)
        scores(2 * j + 2, st_a)
        consume(2 * j + 1, st_b, False)
        return carry

    scores(0, st_a)
    lax.fori_loop(0, qi // 2, body, 0)

    @pl.when(qi % 2 == 0)
    def _():
        consume(qi, st_a, True)

    @pl.when(qi % 2 == 1)
    def _():
        scores(qi, st_b)
        consume(qi - 1, st_a, False)
        consume(qi, st_b, True)

    o1, o2 = (acc_ref[...] * (1.0 / l_ref[...]) for _, l_ref, acc_ref in state)
    if mode == "diff":
        o = (o1 - lam_ref[0] * o2).T
        o = _rms(o, gain_ref[...]) * out_scale
    else:
        chan = lax.broadcasted_iota(jnp.int32, o1.shape, 0)
        o = jnp.where(chan < HEAD_DIM, o1, o2).T
    o_ref[...] = o.astype(o_ref.dtype)


def _attn_scratch(tq):
    per_map = [pltpu.VMEM((1, tq), F32), pltpu.VMEM((1, tq), F32), pltpu.VMEM((LANES, tq), F32)]
    scores = [pltpu.VMEM((tq, tq), F32)] * 4
    return per_map + per_map + scores


def diff_attention(proj, vt, lam, gain, *, tq, out_scale):
    bsz, s, _ = proj.shape
    kern = functools.partial(_attn_kernel, mode="diff", tq=tq, out_scale=out_scale)
    return pl.pallas_call(
        kern,
        grid=(bsz, DIFF_HEADS, s // tq),
        in_specs=[pl.BlockSpec(memory_space=pltpu.SMEM),
                  pl.BlockSpec((None, tq, LANES), lambda b, h, i: (b, i, QA + h)),
                  pl.BlockSpec((None, s, LANES), lambda b, h, i: (b, 0, KA + h)),
                  pl.BlockSpec((None, LANES, s), lambda b, h, i: (b, VT_A + h, 0)),
                  pl.BlockSpec((1, LANES), lambda b, h, i: (0, 0))],
        out_specs=pl.BlockSpec((None, tq, LANES), lambda b, h, i: (b, i, h)),
        out_shape=jax.ShapeDtypeStruct((bsz, s, DIFF_HEADS * LANES), BF16),
        scratch_shapes=_attn_scratch(tq),
        compiler_params=_params(("arbitrary", "arbitrary", "arbitrary")),
        name="diff_attention",
    )(lam, proj, proj, vt, gain)


def fox_attention(proj, vt, cum_col, cum_row, *, tq):
    bsz, s, _ = proj.shape
    npair = FOX_HEADS // 2
    kern = functools.partial(_attn_kernel, mode="fox", tq=tq, out_scale=None)
    return pl.pallas_call(
        kern,
        grid=(bsz, npair, s // tq),
        in_specs=[pl.BlockSpec((None, tq, LANES), lambda b, h, i: (b, i, QB + h)),
                  pl.BlockSpec((None, s, LANES), lambda b, h, i: (b, 0, KB + h)),
                  pl.BlockSpec((None, LANES, s), lambda b, h, i: (b, VT_B + h, 0)),
                  pl.BlockSpec((None, s, LANES), lambda b, h, i: (b, 0, 0)),
                  pl.BlockSpec((None, 8, tq), lambda b, h, i: (b, 0, i))],
        out_specs=pl.BlockSpec((None, tq, LANES), lambda b, h, i: (b, i, h)),
        out_shape=jax.ShapeDtypeStruct((bsz, s, npair * LANES), BF16),
        scratch_shapes=_attn_scratch(tq) + [pltpu.VMEM((2, s, LANES), F32)],
        compiler_params=_params(("arbitrary", "arbitrary", "arbitrary")),
        name="fox_attention",
    )(proj, proj, vt, cum_col, cum_row)


def _retention_tables(s, chunk):
    half = HEAD_DIM // 2
    inv = RET_THETA ** (-np.arange(half, dtype=np.float64) / half)
    ang = np.arange(s, dtype=np.float64)[:, None] * inv[None, :]
    cos = np.tile(np.cos(ang), (1, 4))
    sin = np.tile(np.concatenate([-np.sin(ang), np.sin(ang)], axis=1), (1, 2))
    log_g = np.log1p(-np.exp2(-5.0 - np.arange(RET_HEADS, dtype=np.float64)))
    j = np.arange(chunk, dtype=np.float64)
    diff = j[:, None] - j[None, :]
    inner = np.where(diff[None] >= 0, np.exp(np.where(diff[None] >= 0, diff[None], 0.0)
                                             * log_g[:, None, None]), 0.0)
    q_dec = np.exp((j + 1.0)[None, :] * log_g[:, None])
    k_dec = np.exp((chunk - 1.0 - j)[None, :] * log_g[:, None])
    c_dec = np.exp(chunk * log_g)
    npair = RET_HEADS // 2

    def lanes(t):
        t = t.reshape(npair, 2, chunk)
        return np.repeat(t.transpose(0, 2, 1), HEAD_DIM, axis=2)

    blk = np.kron(np.eye(2), np.ones((HEAD_DIM, HEAD_DIM)))
    c_mat = np.stack([np.kron(np.diag(c_dec[2 * p:2 * p + 2]), np.ones((HEAD_DIM, HEAD_DIM)))
                      for p in range(npair)])
    f = lambda a: jnp.asarray(a, F32)
    return dict(cos=f(cos), sin=f(sin), inner=f(inner.reshape(npair, 2, chunk, chunk)),
                q_dec=f(lanes(q_dec)), k_dec=f(lanes(k_dec)), c_mat=f(c_mat), blk=f(blk))


def _retention_kernel(q_ref, k_ref, v_ref, g_ref, cos_ref, sin_ref, inner_ref, qdec_ref,
                      kdec_ref, cmat_ref, blk_ref, gain_ref, o_ref, state_ref):
    @pl.when(pl.program_id(1) == 0)
    def _():
        state_ref[...] = jnp.zeros_like(state_ref)

    cos = cos_ref[...]
    sin = sin_ref[...]
    lane = lax.broadcasted_iota(jnp.int32, cos.shape, 1)
    first_half = (lane % HEAD_DIM) < (HEAD_DIM // 2)
    lo = lane < HEAD_DIM

    def rope(x):
        x = x.astype(F32)
        swapped = jnp.where(first_half, pltpu.roll(x, LANES - HEAD_DIM // 2, 1),
                            pltpu.roll(x, HEAD_DIM // 2, 1))
        return x * cos + swapped * sin

    for p in range(RET_HEADS // 2):
        cols = slice(p * LANES, (p + 1) * LANES)
        qr = rope(q_ref[:, cols])
        kr = rope(k_ref[:, cols])
        v = v_ref[:, cols]
        qb = qr.astype(BF16)
        kb = kr.astype(BF16)
        zero = jnp.zeros_like(qb)
        a1 = (_dot_nt(jnp.where(lo, qb, zero), kb) * inner_ref[p, 0]).astype(BF16)
        a2 = (_dot_nt(jnp.where(lo, zero, qb), kb) * inner_ref[p, 1]).astype(BF16)
        state = state_ref[p]
        o = jnp.where(lo, _dot(a1, v), _dot(a2, v))
        o = o + _dot((qr * qdec_ref[p]).astype(BF16), state.astype(BF16))
        kd = (kr * kdec_ref[p]).astype(BF16)
        state_ref[p] = state * cmat_ref[p] + _dot(kd.T, v) * blk_ref[...]

        inv_n = 1.0 / HEAD_DIM
        sum_lo = jnp.sum(jnp.where(lo, o, 0.0), axis=-1, keepdims=True)
        sum_hi = jnp.sum(jnp.where(lo, 0.0, o), axis=-1, keepdims=True)
        d = o - jnp.where(lo, sum_lo, sum_hi) * inv_n
        d2 = d * d
        var_lo = jnp.sum(jnp.where(lo, d2, 0.0), axis=-1, keepdims=True)
        var_hi = jnp.sum(jnp.where(lo, 0.0, d2), axis=-1, keepdims=True)
        y = d * lax.rsqrt(jnp.where(lo, var_lo, var_hi) * inv_n + EPS)
        g = g_ref[:, cols].astype(F32)
        o_ref[:, cols] = (g * _sigmoid(g) * (y * gain_ref[:, cols])).astype(o_ref.dtype)


def retention(proj, gain, *, chunk):
    bsz, s, _ = proj.shape
    npair = RET_HEADS // 2
    t = _retention_tables(s, chunk)

    width = npair * LANES

    def act(off):
        return pl.BlockSpec((None, chunk, width), lambda b, c: (b, c, off // npair))

    pos = pl.BlockSpec((chunk, LANES), lambda b, c: (c, 0))
    return pl.pallas_call(
        _retention_kernel,
        grid=(bsz, s // chunk),
        in_specs=[act(QC), act(KC), act(VC), act(GC), pos, pos,
                  _resident((npair, 2, chunk, chunk)), _resident((npair, chunk, LANES)),
                  _resident((npair, chunk, LANES)), _resident((npair, LANES, LANES)),
                  _resident((LANES, LANES)), _resident((1, width))],
        out_specs=pl.BlockSpec((None, chunk, width), lambda b, c: (b, c, 0)),
        out_shape=jax.ShapeDtypeStruct((bsz, s, width), BF16),
        scratch_shapes=[pltpu.VMEM((npair, LANES, LANES), F32)],
        compiler_params=_params(("arbitrary", "arbitrary")),
        name="retention",
    )(proj, proj, proj, proj, t["cos"], t["sin"], t["inner"], t["q_dec"], t["k_dec"],
      t["c_mat"], t["blk"], gain)


def _mix_residual(h_ref, oa_ref, ob_ref, oc_ref, wo_ref):
    wa = oa_ref.shape[1]
    wb = ob_ref.shape[1]
    return (h_ref[...] + _dot(oa_ref[...], wo_ref[0:wa, :])
            + _dot(ob_ref[...], wo_ref[wa:wa + wb, :])
            + _dot(oc_ref[...], wo_ref[wa + wb:, :]))


def _swiglu_accumulate(xn, wg_ref, wu_ref, wd_ref, acc_ref, chunk):
    def body(i, carry):
        c = pl.multiple_of(i * chunk, chunk)
        g = _dot(xn, wg_ref[:, pl.ds(c, chunk)])
        u = _dot(xn, wu_ref[:, pl.ds(c, chunk)])
        a = (g * _sigmoid(g) * u).astype(BF16)
        acc_ref[...] += _dot(a, wd_ref[pl.ds(c, chunk), :])
        return carry

    lax.fori_loop(0, wg_ref.shape[1] // chunk, body, 0)


def _ple(h, p_ref, pn_ref, pg_ref, pp_ref):
    gate = _sigmoid(_dot(_rms(h, pn_ref[...]).astype(BF16), pg_ref[...]))
    return h + gate * _dot(p_ref[...].astype(BF16), pp_ref[...])


def _dense_layer_kernel(h_ref, oa_ref, ob_ref, oc_ref, p_ref, wo_ref, gn_ref, wg_ref, wu_ref,
                        wd_ref, pn_ref, pg_ref, pp_ref, out_ref, acc_ref, *, ff_chunk):
    h1 = _mix_residual(h_ref, oa_ref, ob_ref, oc_ref, wo_ref)
    xn = _rms(h1, gn_ref[...]).astype(BF16)
    acc_ref[...] = h1
    _swiglu_accumulate(xn, wg_ref, wu_ref, wd_ref, acc_ref, ff_chunk)
    out_ref[...] = _ple(acc_ref[...], p_ref, pn_ref, pg_ref, pp_ref)


def dense_layer(h, oa, ob, oc, p, wo, gn, wg, wu, wd, pn, pg, pp, *, tm, ff_chunk):
    n, d = h.shape
    row = lambda w: pl.BlockSpec((tm, w), lambda i: (i, 0))
    weights = (wo, gn, wg, wu, wd, pn, pg, pp)
    return pl.pallas_call(
        functools.partial(_dense_layer_kernel, ff_chunk=ff_chunk),
        grid=(n // tm,),
        in_specs=[row(d), row(oa.shape[1]), row(ob.shape[1]), row(oc.shape[1]), row(p.shape[1])]
                 + [_resident(w.shape) for w in weights],
        out_specs=row(d),
        out_shape=jax.ShapeDtypeStruct((n, d), F32),
        scratch_shapes=[pltpu.VMEM((tm, d), F32)],
        compiler_params=_params(("arbitrary",)),
        name="dense_layer",
    )(h, oa, ob, oc, p, *weights)


def _store_slabs(ref, x):
    rows, d = x.shape
    nl = d // LANES
    for c in range(nl):
        ref[pl.ds(c, rows, stride=nl), :] = x[:, c * LANES:(c + 1) * LANES].astype(ref.dtype)


def _load_slabs(ref, nl, dtype=None):
    parts = [ref[pl.ds(c, ref.shape[0] // nl, stride=nl), :] for c in range(nl)]
    if dtype is not None:
        parts = [p.astype(dtype) for p in parts]
    return jnp.concatenate(parts, axis=1)


def _slab_at(ref, row, nl):
    return ref.at[pl.ds(pl.multiple_of(row * nl, nl), nl), :]


def _router_kernel(h_ref, oa_ref, ob_ref, oc_ref, wo_ref, gn_ref, wr_ref,
                   h1_ref, xn_ref, meta_ref, cnt_ref):
    @pl.when(pl.program_id(0) == 0)
    def _():
        cnt_ref[...] = jnp.zeros_like(cnt_ref)

    h1 = _mix_residual(h_ref, oa_ref, ob_ref, oc_ref, wo_ref)
    h1_ref[...] = h1
    xn = _rms(h1, gn_ref[...])
    _store_slabs(xn_ref, xn)

    logits = jnp.dot(xn, wr_ref[...], precision=lax.Precision.HIGHEST,
                     preferred_element_type=F32)
    tm = logits.shape[0]
    lane = lax.broadcasted_iota(jnp.int32, logits.shape, 1)
    logits = jnp.where(lane < N_EXPERTS, logits, -jnp.inf)

    def top(vals):
        best = jnp.max(vals, axis=-1, keepdims=True)
        idx = jnp.min(jnp.where(vals == best, lane, LANES), axis=-1, keepdims=True)
        return best, idx

    v1, i1 = top(logits)
    v2, i2 = top(jnp.where(lane == i1, -jnp.inf, logits))
    e = jnp.exp(v2 - v1)
    g1 = 1.0 / (1.0 + e)
    g2 = e / (1.0 + e)

    oh1 = (lane == i1).astype(F32)
    oh2 = (lane == i2).astype(F32)
    both = oh1 + oh2
    r = lax.broadcasted_iota(jnp.int32, (tm, tm), 0)
    c = lax.broadcasted_iota(jnp.int32, (tm, tm), 1)
    before = _dot((c < r).astype(BF16), both.astype(BF16)) + cnt_ref[...]
    rank1 = jnp.sum(before * oh1, axis=-1, keepdims=True)
    rank2 = jnp.sum(before * oh2, axis=-1, keepdims=True)
    cnt_ref[...] += jnp.sum(both, axis=0, keepdims=True)

    meta = jnp.zeros(logits.shape, F32)
    for k, val in enumerate((i1.astype(F32), i2.astype(F32), rank1, rank2, g1, g2)):
        meta = jnp.where(lane == k, val, meta)
    meta_ref[...] = meta


def route_tokens(h, oa, ob, oc, wo, gn, wr, *, tm):
    n, d = h.shape
    row = lambda w: pl.BlockSpec((tm, w), lambda i: (i, 0))
    weights = (wo, gn, wr)
    return pl.pallas_call(
        _router_kernel,
        grid=(n // tm,),
        in_specs=[row(d), row(oa.shape[1]), row(ob.shape[1]), row(oc.shape[1])]
                 + [_resident(w.shape) for w in weights],
        out_specs=[row(d), pl.BlockSpec((tm * (d // LANES), LANES), lambda i: (i, 0)),
                   row(LANES), pl.BlockSpec((1, LANES), lambda i: (0, 0))],
        out_shape=[jax.ShapeDtypeStruct((n, d), F32),
                   jax.ShapeDtypeStruct((n * (d // LANES), LANES), F32),
                   jax.ShapeDtypeStruct((n, LANES), F32),
                   jax.ShapeDtypeStruct((1, LANES), F32)],
        compiler_params=_params(("arbitrary",)),
        name="router",
    )(h, oa, ob, oc, *weights)


def _dispatch_kernel(last_ref, pos_ref, x_ref, xs_ref, zero_scr, sem, zero_sem, *, tile, nl):
    tm = x_ref.shape[0] // nl
    zrows = zero_scr.shape[0] // nl

    @pl.when(pl.program_id(0) == 0)
    def _():
        zero_scr[...] = jnp.zeros_like(zero_scr)

        def zero_copy(e, q):
            start = pl.multiple_of((last_ref[e] + q * zrows) * nl, zrows * nl)
            return pltpu.make_async_copy(zero_scr, xs_ref.at[pl.ds(start, zrows * nl), :],
                                         zero_sem)

        for e in range(N_EXPERTS):
            for q in range(tile // zrows):
                zero_copy(e, q).start()
        for e in range(N_EXPERTS):
            for q in range(tile // zrows):
                zero_copy(e, q).wait()

        n_trailing = (xs_ref.shape[0] // nl - last_ref[N_EXPERTS]) // zrows

        def trailing(q, carry, wait):
            cp = zero_copy(N_EXPERTS, q)
            cp.wait() if wait else cp.start()
            return carry

        lax.fori_loop(0, n_trailing, functools.partial(trailing, wait=False), 0)
        lax.fori_loop(0, n_trailing, functools.partial(trailing, wait=True), 0)

    def row_copy(r, k):
        return pltpu.make_async_copy(_slab_at(x_ref, r, nl),
                                     _slab_at(xs_ref, pos_ref[0, k, r], nl), sem)

    _start_rows(tm, row_copy)
    for k in range(2):
        pltpu.make_async_copy(x_ref, xs_ref.at[pl.ds(0, tm * nl), :], sem).wait()


def _start_rows(n_rows, row_copy):
    def body(g, carry):
        base = pl.multiple_of(g * ROW_UNROLL, ROW_UNROLL)
        for j in range(ROW_UNROLL):
            row_copy(base + j, 0).start(priority=0)
            row_copy(base + j, 1).start(priority=1)
        return carry

    lax.fori_loop(0, n_rows // ROW_UNROLL, body, 0)


def dispatch(xn, pos, last_tile_start, n_slots, *, tm, tile, nl):
    n = xn.shape[0] // nl
    zrows = min(tile, 256)
    grid_spec = pltpu.PrefetchScalarGridSpec(
        num_scalar_prefetch=1,
        grid=(n // tm,),
        in_specs=[pl.BlockSpec((1, 2, tm), lambda i, last: (i, 0, 0), memory_space=pltpu.SMEM),
                  pl.BlockSpec((tm * nl, LANES), lambda i, last: (i, 0))],
        out_specs=pl.BlockSpec(memory_space=pl.ANY),
        scratch_shapes=[pltpu.VMEM((zrows * nl, LANES), xn.dtype),
                        pltpu.SemaphoreType.DMA(()), pltpu.SemaphoreType.DMA(())],
    )
    return pl.pallas_call(
        functools.partial(_dispatch_kernel, tile=tile, nl=nl),
        grid_spec=grid_spec,
        out_shape=jax.ShapeDtypeStruct((n_slots * nl, LANES), xn.dtype),
        compiler_params=_params(("arbitrary",)),
        name="moe_dispatch",
    )(last_tile_start, pos, xn)


def _experts_kernel(te_ref, nt_ref, x_ref, wg_ref, wu_ref, wd_ref, o_ref, x_scr, acc_ref, *,
                    sub_chunk):
    del te_ref
    t = pl.program_id(0)
    c = pl.program_id(1)
    last = pl.num_programs(1) - 1

    @pl.when(c == 0)
    def _():
        acc_ref[...] = jnp.zeros_like(acc_ref)

    @pl.when(t < nt_ref[0])
    def _():
        @pl.when(c == 0)
        def _():
            x_scr[...] = _load_slabs(x_ref, x_scr.shape[1] // LANES, BF16)

        _swiglu_accumulate(x_scr[...], wg_ref, wu_ref, wd_ref, acc_ref, sub_chunk)

    @pl.when(c == last)
    def _():
        _store_slabs(o_ref, acc_ref[...])


def experts(xs, tile_expert, n_tiles, wg, wu, wd, *, tile, ff_chunk, sub_chunk):
    d = wg.shape[1]
    nl = d // LANES
    n_slots = xs.shape[0] // nl
    d_ff = wg.shape[2]
    nc = d_ff // ff_chunk
    grid_tiles = n_slots // tile

    def tile_idx(t, c, te, nt):
        return (jnp.maximum(jnp.minimum(t, nt[0] - 1), 0), 0)

    def chunk_idx(t, c, nt):
        return jnp.where(t < nt[0], c, nc - 1)

    grid_spec = pltpu.PrefetchScalarGridSpec(
        num_scalar_prefetch=2,
        grid=(grid_tiles, nc),
        in_specs=[pl.BlockSpec((tile * nl, LANES), tile_idx),
                  pl.BlockSpec((None, d, ff_chunk), lambda t, c, te, nt: (te[t], 0, chunk_idx(t, c, nt))),
                  pl.BlockSpec((None, d, ff_chunk), lambda t, c, te, nt: (te[t], 0, chunk_idx(t, c, nt))),
                  pl.BlockSpec((None, ff_chunk, d), lambda t, c, te, nt: (te[t], chunk_idx(t, c, nt), 0))],
        out_specs=pl.BlockSpec((tile * nl, LANES), lambda t, c, te, nt: (t, 0)),
        scratch_shapes=[pltpu.VMEM((tile, d), BF16), pltpu.VMEM((tile, d), F32)],
    )
    return pl.pallas_call(
        functools.partial(_experts_kernel, sub_chunk=sub_chunk),
        grid_spec=grid_spec,
        out_shape=jax.ShapeDtypeStruct((n_slots * nl, LANES), F32),
        compiler_params=_params(("arbitrary", "arbitrary")),
        name="moe_experts",
    )(tile_expert, n_tiles, xs, wg, wu, wd)


def _combine_kernel(pos_ref, h_ref, meta_ref, p_ref, pn_ref, pg_ref, pp_ref, fn_ref, ys_ref,
                    out_ref, y1_scr, y2_scr, sem):
    tm = h_ref.shape[0]
    bufs = (y1_scr, y2_scr)

    nl = h_ref.shape[1] // LANES

    def row_copy(r, k):
        return pltpu.make_async_copy(_slab_at(ys_ref, pos_ref[0, k, r], nl),
                                     _slab_at(bufs[k], r, nl), sem)

    _start_rows(tm, row_copy)
    for buf in bufs:
        pltpu.make_async_copy(ys_ref.at[pl.ds(0, tm * nl), :], buf, sem).wait()

    meta = meta_ref[...]
    lane = lax.broadcasted_iota(jnp.int32, meta.shape, 1)
    g1 = jnp.sum(jnp.where(lane == 4, meta, 0.0), axis=-1, keepdims=True)
    g2 = jnp.sum(jnp.where(lane == 5, meta, 0.0), axis=-1, keepdims=True)
    h2 = h_ref[...] + g1 * _load_slabs(y1_scr, nl) + g2 * _load_slabs(y2_scr, nl)
    h3 = _ple(h2, p_ref, pn_ref, pg_ref, pp_ref)
    out_ref[...] = _rms(h3, fn_ref[...])


def combine(pos, h1, meta, p, pn, pg, pp, fn, ys, *, tm):
    n, d = h1.shape
    row = lambda w: pl.BlockSpec((tm, w), lambda i: (i, 0))
    weights = (pn, pg, pp, fn)
    return pl.pallas_call(
        _combine_kernel,
        grid=(n // tm,),
        in_specs=[pl.BlockSpec((1, 2, tm), lambda i: (i, 0, 0), memory_space=pltpu.SMEM),
                  row(d), row(LANES), row(p.shape[1])]
                 + [_resident(w.shape) for w in weights]
                 + [pl.BlockSpec(memory_space=pl.ANY)],
        out_specs=row(d),
        out_shape=jax.ShapeDtypeStruct((n, d), F32),
        scratch_shapes=[pltpu.VMEM((tm * (d // LANES), LANES), F32),
                        pltpu.VMEM((tm * (d // LANES), LANES), F32),
                        pltpu.SemaphoreType.DMA(())],
        compiler_params=_params(("arbitrary",)),
        name="moe_combine",
    )(pos, h1, meta, p, *weights, ys)


def _prep_w_in(w):
    scale = HEAD_DIM ** -0.5
    qa, ka, va = w[:, 0:512] * (scale * LOG2_E), w[:, 512:1024], w[:, 1024:1536]
    qb, kb, vb = w[:, 1536:1792] * (scale * LOG2_E), w[:, 1792:2048], w[:, 2048:2304]
    fb = w[:, 2304:2308]
    qc, kc, vc, gc = w[:, 2308:2564], w[:, 2564:2820] * scale, w[:, 2820:3076], w[:, 3076:3332]
    main = jnp.concatenate([qa, ka, qb, kb, qc, kc, vc, gc], axis=1).astype(BF16)
    f = jnp.pad(fb, ((0, 0), (0, LANES - fb.shape[1]))).astype(BF16)
    vt = jnp.concatenate([va, vb], axis=1).T.astype(BF16)
    return main, f, vt


def _pad_lanes(v):
    return jnp.pad(v, (0, LANES - v.shape[0]))[None, :]


class Tiles(NamedTuple):
    rows: int
    attn: int
    scan: int
    ret_chunk: int
    ff_chunk: int
    moe_tile: int
    moe_ff_chunk: int
    moe_sub_chunk: int


TILES = Tiles(rows=1024, attn=512, scan=512, ret_chunk=256, ff_chunk=256,
              moe_tile=1024, moe_ff_chunk=1792, moe_sub_chunk=256)


def _mixer(h, b, s, norm_g, w_in, b_forget, lq1, lk1, lq2, lk2, subln, ret_gn, layer_idx, t):
    w_main, w_f, w_vt = _prep_w_in(w_in)
    proj, f, vt = norm_proj(h, norm_g[None, :], w_main, w_f, w_vt, tm=t.rows, seq=s)
    proj = proj.reshape(b, s, -1)
    cum_col, cum_row = fox_prep(f.reshape(b, s, LANES), _pad_lanes(b_forget), blk=t.scan)
    lam_init = 0.8 - 0.6 * math.exp(-0.3 * layer_idx)
    lam = jnp.exp(jnp.sum(lq1 * lk1)) - jnp.exp(jnp.sum(lq2 * lk2)) + lam_init
    oa = diff_attention(proj, vt, lam.reshape(1), subln[None, :], tq=t.attn,
                        out_scale=1.0 - lam_init)
    ob = fox_attention(proj, vt, cum_col, cum_row, tq=t.attn)
    oc = retention(proj, ret_gn[None, :], chunk=t.ret_chunk)
    n = b * s
    return oa.reshape(n, -1), ob.reshape(n, -1), oc.reshape(n, -1)


def _route_plan(meta, counts, tile, n_slots, tm):
    n = meta.shape[0]
    idx = meta[:, 0:2].astype(jnp.int32)
    rank = meta[:, 2:4].astype(jnp.int32)
    cnt = counts[0, :N_EXPERTS].astype(jnp.int32)
    padded = ((cnt + tile - 1) // tile) * tile
    ends = jnp.cumsum(padded)
    starts = ends - padded
    pos = starts[idx] + rank
    n_tiles = ends[-1] // tile
    tiles = jnp.arange(n_slots // tile, dtype=jnp.int32)
    tile_expert = jnp.sum(tiles[:, None] >= (ends // tile)[None, :], axis=1)
    last = jnp.sum(n_tiles - 1 >= ends // tile)
    tile_expert = jnp.where(tiles < n_tiles, tile_expert, last).astype(jnp.int32)
    tile_expert = jnp.clip(tile_expert, 0, N_EXPERTS - 1)
    pos = jnp.clip(pos, 0, n_slots - 1)
    pos = pos.reshape(n // tm, tm, 2).transpose(0, 2, 1)
    last_tile_start = jnp.concatenate([jnp.clip(ends - tile, 0, n_slots - tile),
                                       jnp.clip(ends[-1:], 0, n_slots)]).astype(jnp.int32)
    return pos, tile_expert, n_tiles.reshape(1).astype(jnp.int32), last_tile_start


def forward(x, p, norm_mix, w_in, b_forget, lambda_q1, lambda_k1, lambda_q2, lambda_k2,
            diff_subln, ret_gn, w_out, norm_ffn, dense_w_gate, dense_w_up, dense_w_down,
            router, moe_w_gate, moe_w_up, moe_w_down, ple_norm, ple_gate, ple_proj, final_norm,
            t=TILES):
    assert w_in.shape[0] == 2, "two layers: dense SwiGLU, then routed experts"
    b, s, d = x.shape
    n = b * s
    h = x.reshape(n, d)
    pf = p.reshape(p.shape[0], n, p.shape[-1])
    bf = lambda a: a.astype(BF16)
    row = lambda a: a[None, :]

    oa, ob, oc = _mixer(h, b, s, norm_mix[0], w_in[0], b_forget[0], lambda_q1[0], lambda_k1[0],
                        lambda_q2[0], lambda_k2[0], diff_subln[0], ret_gn[0], 0, t)
    h = dense_layer(h, oa, ob, oc, pf[0], bf(w_out[0]), row(norm_ffn[0]), bf(dense_w_gate[0]),
                    bf(dense_w_up[0]), bf(dense_w_down[0]), row(ple_norm[0]), bf(ple_gate[0]),
                    bf(ple_proj[0]), tm=t.rows, ff_chunk=t.ff_chunk)

    oa, ob, oc = _mixer(h, b, s, norm_mix[1], w_in[1], b_forget[1], lambda_q1[1], lambda_k1[1],
                        lambda_q2[1], lambda_k2[1], diff_subln[1], ret_gn[1], 1, t)
    wr = jnp.pad(router[0], ((0, 0), (0, LANES - N_EXPERTS)))
    h1, xn, meta, counts = route_tokens(h, oa, ob, oc, bf(w_out[1]), row(norm_ffn[1]), wr,
                                         tm=t.rows)
    n_slots = 2 * n + N_EXPERTS * t.moe_tile
    pos, tile_expert, n_tiles, last_tile_start = _route_plan(meta, counts, t.moe_tile, n_slots,
                                                             t.rows)
    xs = dispatch(xn, pos, last_tile_start, n_slots, tm=t.rows, tile=t.moe_tile,
                  nl=d // LANES)
    ys = experts(xs, tile_expert, n_tiles, bf(moe_w_gate[0]), bf(moe_w_up[0]), bf(moe_w_down[0]),
                 tile=t.moe_tile, ff_chunk=t.moe_ff_chunk, sub_chunk=t.moe_sub_chunk)
    out = combine(pos, h1, meta, pf[1], row(ple_norm[1]), bf(ple_gate[1]), bf(ple_proj[1]),
                  row(final_norm), ys, tm=t.rows)
    return out.reshape(b, s, d)


def kernel(x, p, norm_mix, w_in, b_forget, lambda_q1, lambda_k1, lambda_q2, lambda_k2,
           diff_subln, ret_gn, w_out, norm_ffn, dense_w_gate, dense_w_up, dense_w_down,
           router, moe_w_gate, moe_w_up, moe_w_down, ple_norm, ple_gate, ple_proj, final_norm):
    return forward(x, p, norm_mix, w_in, b_forget, lambda_q1, lambda_k1, lambda_q2, lambda_k2,
                   diff_subln, ret_gn, w_out, norm_ffn, dense_w_gate, dense_w_up, dense_w_down,
                   router, moe_w_gate, moe_w_up, moe_w_down, ple_norm, ple_gate, ple_proj,
                   final_norm)
```

```python
import functools
import math
from typing import NamedTuple

import numpy as np
import jax
import jax.numpy as jnp
from jax import lax
from jax.experimental import pallas as pl
from jax.experimental.pallas import tpu as pltpu

F32 = jnp.float32
BF16 = jnp.bfloat16

LANES = 128
ROW_UNROLL = 8
VMEM_LIMIT_BYTES = 56 * 1024 * 1024

HEAD_DIM = 64
DIFF_HEADS = 4
FOX_HEADS = 4
RET_HEADS = 4
RET_THETA = 10000.0
N_EXPERTS = 8
EPS = 1e-6
NEG_INF = -1e30
LOG2_E = math.log2(math.e)

QA, KA = 0, 4
QB, KB = 8, 10
QC, KC, VC, GC = 12, 14, 16, 18
VT_A, VT_B = 0, 4


def _params(semantics):
    return pltpu.CompilerParams(dimension_semantics=semantics,
                                vmem_limit_bytes=VMEM_LIMIT_BYTES)


def _resident(shape):
    nd = len(shape)
    return pl.BlockSpec(shape, lambda *_: (0,) * nd, pipeline_mode=pl.Buffered(1))


def _rms(x, g):
    return x * lax.rsqrt(jnp.mean(x * x, axis=-1, keepdims=True) + EPS) * g


def _sigmoid(x):
    return 1.0 / (1.0 + jnp.exp(-x))


def _dot(a, b):
    return jnp.dot(a, b, preferred_element_type=F32)


def _dot_nt(a, b):
    return lax.dot_general(a, b, (((1,), (1,)), ((), ())), preferred_element_type=F32)


def _split_bf16(x, n):
    pieces = []
    for _ in range(n):
        piece = x.astype(BF16)
        pieces.append(piece)
        x = x - piece.astype(F32)
    return pieces


def _lane_masks(shape):
    lane = lax.broadcasted_iota(jnp.int32, shape, len(shape) - 1)
    return lane < HEAD_DIM


def _norm_proj_kernel(h_ref, g_ref, w_ref, wf_ref, wvt_ref, out_ref, f_ref, vt_ref, *,
                      col_chunk):
    xn = _rms(h_ref[...], g_ref[...]).astype(BF16)
    ncol = w_ref.shape[1]
    for c in range(0, ncol, col_chunk):
        w = min(col_chunk, ncol - c)
        out_ref[:, c:c + w] = _dot(xn, w_ref[:, c:c + w]).astype(out_ref.dtype)
    f_ref[...] = _dot(xn, wf_ref[...])
    vt_ref[...] = _dot_nt(wvt_ref[...], xn).astype(vt_ref.dtype)


def norm_proj(h, g, w_main, w_f, w_vt, *, tm, seq):
    n, d = h.shape
    ncol = w_main.shape[1]
    nvt = w_vt.shape[0]
    per_seq = seq // tm
    return pl.pallas_call(
        functools.partial(_norm_proj_kernel, col_chunk=512),
        grid=(n // tm,),
        in_specs=[pl.BlockSpec((tm, d), lambda i: (i, 0)),
                  _resident((1, d)), _resident((d, ncol)), _resident((d, LANES)),
                  _resident((nvt, d))],
        out_specs=[pl.BlockSpec((tm, ncol), lambda i: (i, 0)),
                   pl.BlockSpec((tm, LANES), lambda i: (i, 0)),
                   pl.BlockSpec((None, nvt, tm), lambda i: (i // per_seq, 0, i % per_seq))],
        out_shape=[jax.ShapeDtypeStruct((n, ncol), BF16),
                   jax.ShapeDtypeStruct((n, LANES), F32),
                   jax.ShapeDtypeStruct((n // seq, nvt, seq), BF16)],
        compiler_params=_params(("arbitrary",)),
        name="norm_proj",
    )(h, g, w_main, w_f, w_vt)


def _fox_prep_kernel(f_ref, b_ref, col_ref, row_ref, carry_ref):
    @pl.when(pl.program_id(1) == 0)
    def _():
        carry_ref[...] = jnp.zeros_like(carry_ref)

    x = f_ref[...] + b_ref[...]
    log_f = (jnp.minimum(x, 0.0) - jnp.log1p(jnp.exp(-jnp.abs(x)))) * LOG2_E
    blk = x.shape[0]
    r = lax.broadcasted_iota(jnp.int32, (blk, blk), 0)
    c = lax.broadcasted_iota(jnp.int32, (blk, blk), 1)
    tri = (c <= r).astype(BF16)
    cum = sum(_dot(tri, piece) for piece in _split_bf16(log_f, 3)) + carry_ref[...]
    col_ref[...] = cum
    row_ref[...] = cum.T[:8, :]
    carry_ref[...] = cum[blk - 1:blk, :]


def fox_prep(f, b_pad, *, blk):
    bsz, s, _ = f.shape
    return pl.pallas_call(
        _fox_prep_kernel,
        grid=(bsz, s // blk),
        in_specs=[pl.BlockSpec((None, blk, LANES), lambda b, j: (b, j, 0)),
                  pl.BlockSpec((1, LANES), lambda b, j: (0, 0))],
        out_specs=[pl.BlockSpec((None, blk, LANES), lambda b, j: (b, j, 0)),
                   pl.BlockSpec((None, 8, blk), lambda b, j: (b, 0, j))],
        out_shape=[jax.ShapeDtypeStruct((bsz, s, LANES), F32),
                   jax.ShapeDtypeStruct((bsz, 8, s), F32)],
        scratch_shapes=[pltpu.VMEM((1, LANES), F32)],
        compiler_params=_params(("arbitrary", "arbitrary")),
        name="fox_prep",
    )(f, b_pad)


def _attn_kernel(*refs, mode, tq, out_scale):
    if mode == "diff":
        lam_ref, q_ref, k_ref, vt_ref, gain_ref, o_ref = refs[:6]
    else:
        q_ref, k_ref, vt_ref, ccol_ref, crow_ref, o_ref = refs[:6]
        cs_scr = refs[16]
    state = (refs[6:9], refs[9:12])
    st_a, st_b = refs[12:14], refs[14:16]
    qi = pl.program_id(2)
    pair = pl.program_id(1)

    q = q_ref[...]
    lo = _lane_masks(q.shape)
    zero = jnp.zeros_like(q)
    q_maps = (jnp.where(lo, q, zero), jnp.where(lo, zero, q))

    for m_ref, l_ref, acc_ref in state:
        m_ref[...] = jnp.full_like(m_ref, NEG_INF)
        l_ref[...] = jnp.zeros_like(l_ref)
        acc_ref[...] = jnp.zeros_like(acc_ref)

    if mode == "fox":
        @pl.when(qi == 0)
        def _():
            ccol = ccol_ref[...]
            lane = lax.broadcasted_iota(jnp.int32, ccol.shape, 1)
            for i in range(2):
                col = jnp.sum(jnp.where(lane == 2 * pair + i, ccol, 0.0), axis=-1, keepdims=True)
                cs_scr[i] = jnp.broadcast_to(col, ccol.shape)

        ct = tuple(crow_ref[pl.ds(2 * pair + i, 1), :] for i in range(2))

    def scores(c, bufs):
        k = k_ref[pl.ds(pl.multiple_of(c * tq, tq), tq), :]
        for i in range(2):
            bufs[i][...] = _dot_nt(k, q_maps[i])

    def consume(c, bufs, masked):
        start = pl.multiple_of(c * tq, tq)
        vt = vt_ref[:, pl.ds(start, tq)]
        for i in range(2):
            m_ref, l_ref, acc_ref = state[i]
            st = bufs[i][...]
            if mode == "fox":
                cs = cs_scr[i, pl.ds(start, tq), :]
                st = st - jnp.tile(cs, (1, tq // LANES))
            if masked:
                key = lax.broadcasted_iota(jnp.int32, st.shape, 0)
                qry = lax.broadcasted_iota(jnp.int32, st.shape, 1)
                st = jnp.where(key <= qry, st, NEG_INF)
            m_old = m_ref[...]
            m_chunk = jnp.max(st, axis=0, keepdims=True)
            if mode == "fox":
                m_new = jnp.maximum(m_old, m_chunk + ct[i])
                p = jnp.exp2(st + (ct[i] - m_new))
            else:
                m_new = jnp.maximum(m_old, m_chunk)
                p = jnp.exp2(st - m_new)
            alpha = jnp.exp2(m_old - m_new)
            l_ref[...] = alpha * l_ref[...] + jnp.sum(p, axis=0, keepdims=True)
            acc_ref[...] = alpha * acc_ref[...] + _dot(vt, p.astype(BF16))
            m_ref[...] = m_new

    def body(j, carry):
        scores(2 * j + 1, st_b)
        consume(2 * j, st_a, False)
        scores(2 * j + 2, st_a)
        consume(2 * j + 1, st_b, False)
        return carry

    scores(0, st_a)
    lax.fori_loop(0, qi // 2, body, 0)

    @pl.when(qi % 2 == 0)
    def _():
        consume(qi, st_a, True)

    @pl.when(qi % 2 == 1)
    def _():
        scores(qi, st_b)
        consume(qi - 1, st_a, False)
        consume(qi, st_b, True)

    o1, o2 = (acc_ref[...] * (1.0 / l_ref[...]) for _, l_ref, acc_ref in state)
    if mode == "diff":
        o = (o1 - lam_ref[0] * o2).T
        o = _rms(o, gain_ref[...]) * out_scale
    else:
        chan = lax.broadcasted_iota(jnp.int32, o1.shape, 0)
        o = jnp.where(chan < HEAD_DIM, o1, o2).T
    o_ref[...] = o.astype(o_ref.dtype)


def _attn_scratch(tq):
    per_map = [pltpu.VMEM((1, tq), F32), pltpu.VMEM((1, tq), F32), pltpu.VMEM((LANES, tq), F32)]
    scores = [pltpu.VMEM((tq, tq), F32)] * 4
    return per_map + per_map + scores


def diff_attention(proj, vt, lam, gain, *, tq, out_scale):
    bsz, s, _ = proj.shape
    kern = functools.partial(_attn_kernel, mode="diff", tq=tq, out_scale=out_scale)
    return pl.pallas_call(
        kern,
        grid=(bsz, DIFF_HEADS, s // tq),
        in_specs=[pl.BlockSpec(memory_space=pltpu.SMEM),
                  pl.BlockSpec((None, tq, LANES), lambda b, h, i: (b, i, QA + h)),
                  pl.BlockSpec((None, s, LANES), lambda b, h, i: (b, 0, KA + h)),
                  pl.BlockSpec((None, LANES, s), lambda b, h, i: (b, VT_A + h, 0)),
                  pl.BlockSpec((1, LANES), lambda b, h, i: (0, 0))],
        out_specs=pl.BlockSpec((None, tq, LANES), lambda b, h, i: (b, i, h)),
        out_shape=jax.ShapeDtypeStruct((bsz, s, DIFF_HEADS * LANES), BF16),
        scratch_shapes=_attn_scratch(tq),
        compiler_params=_params(("arbitrary", "arbitrary", "arbitrary")),
        name="diff_attention",
    )(lam, proj, proj, vt, gain)


def fox_attention(proj, vt, cum_col, cum_row, *, tq):
    bsz, s, _ = proj.shape
    npair = FOX_HEADS // 2
    kern = functools.partial(_attn_kernel, mode="fox", tq=tq, out_scale=None)
    return pl.pallas_call(
        kern,
        grid=(bsz, npair, s // tq),
        in_specs=[pl.BlockSpec((None, tq, LANES), lambda b, h, i: (b, i, QB + h)),
                  pl.BlockSpec((None, s, LANES), lambda b, h, i: (b, 0, KB + h)),
                  pl.BlockSpec((None, LANES, s), lambda b, h, i: (b, VT_B + h, 0)),
                  pl.BlockSpec((None, s, LANES), lambda b, h, i: (b, 0, 0)),
                  pl.BlockSpec((None, 8, tq), lambda b, h, i: (b, 0, i))],
        out_specs=pl.BlockSpec((None, tq, LANES), lambda b, h, i: (b, i, h)),
        out_shape=jax.ShapeDtypeStruct((bsz, s, npair * LANES), BF16),
        scratch_shapes=_attn_scratch(tq) + [pltpu.VMEM((2, s, LANES), F32)],
        compiler_params=_params(("arbitrary", "arbitrary", "arbitrary")),
        name="fox_attention",
    )(proj, proj, vt, cum_col, cum_row)


def _retention_tables(s, chunk):
    half = HEAD_DIM // 2
    inv = RET_THETA ** (-np.arange(half, dtype=np.float64) / half)
    ang = np.arange(s, dtype=np.float64)[:, None] * inv[None, :]
    cos = np.tile(np.cos(ang), (1, 4))
    sin = np.tile(np.concatenate([-np.sin(ang), np.sin(ang)], axis=1), (1, 2))
    log_g = np.log1p(-np.exp2(-5.0 - np.arange(RET_HEADS, dtype=np.float64)))
    j = np.arange(chunk, dtype=np.float64)
    diff = j[:, None] - j[None, :]
    inner = np.where(diff[None] >= 0, np.exp(np.where(diff[None] >= 0, diff[None], 0.0)
                                             * log_g[:, None, None]), 0.0)
    q_dec = np.exp((j + 1.0)[None, :] * log_g[:, None])
    k_dec = np.exp((chunk - 1.0 - j)[None, :] * log_g[:, None])
    c_dec = np.exp(chunk * log_g)
    npair = RET_HEADS // 2

    def lanes(t):
        t = t.reshape(npair, 2, chunk)
        return np.repeat(t.transpose(0, 2, 1), HEAD_DIM, axis=2)

    blk = np.kron(np.eye(2), np.ones((HEAD_DIM, HEAD_DIM)))
    c_mat = np.stack([np.kron(np.diag(c_dec[2 * p:2 * p + 2]), np.ones((HEAD_DIM, HEAD_DIM)))
                      for p in range(npair)])
    f = lambda a: jnp.asarray(a, F32)
    return dict(cos=f(cos), sin=f(sin), inner=f(inner.reshape(npair, 2, chunk, chunk)),
                q_dec=f(lanes(q_dec)), k_dec=f(lanes(k_dec)), c_mat=f(c_mat), blk=f(blk))


def _retention_kernel(q_ref, k_ref, v_ref, g_ref, cos_ref, sin_ref, inner_ref, qdec_ref,
                      kdec_ref, cmat_ref, blk_ref, gain_ref, o_ref, state_ref):
    @pl.when(pl.program_id(1) == 0)
    def _():
        state_ref[...] = jnp.zeros_like(state_ref)

    cos = cos_ref[...]
    sin = sin_ref[...]
    lane = lax.broadcasted_iota(jnp.int32, cos.shape, 1)
    first_half = (lane % HEAD_DIM) < (HEAD_DIM // 2)
    lo = lane < HEAD_DIM

    def rope(x):
        x = x.astype(F32)
        swapped = jnp.where(first_half, pltpu.roll(x, LANES - HEAD_DIM // 2, 1),
                            pltpu.roll(x, HEAD_DIM // 2, 1))
        return x * cos + swapped * sin

    for p in range(RET_HEADS // 2):
        cols = slice(p * LANES, (p + 1) * LANES)
        qr = rope(q_ref[:, cols])
        kr = rope(k_ref[:, cols])
        v = v_ref[:, cols]
        qb = qr.astype(BF16)
        kb = kr.astype(BF16)
        zero = jnp.zeros_like(qb)
        a1 = (_dot_nt(jnp.where(lo, qb, zero), kb) * inner_ref[p, 0]).astype(BF16)
        a2 = (_dot_nt(jnp.where(lo, zero, qb), kb) * inner_ref[p, 1]).astype(BF16)
        state = state_ref[p]
        o = jnp.where(lo, _dot(a1, v), _dot(a2, v))
        o = o + _dot((qr * qdec_ref[p]).astype(BF16), state.astype(BF16))
        kd = (kr * kdec_ref[p]).astype(BF16)
        state_ref[p] = state * cmat_ref[p] + _dot(kd.T, v) * blk_ref[...]

        inv_n = 1.0 / HEAD_DIM
        sum_lo = jnp.sum(jnp.where(lo, o, 0.0), axis=-1, keepdims=True)
        sum_hi = jnp.sum(jnp.where(lo, 0.0, o), axis=-1, keepdims=True)
        d = o - jnp.where(lo, sum_lo, sum_hi) * inv_n
        d2 = d * d
        var_lo = jnp.sum(jnp.where(lo, d2, 0.0), axis=-1, keepdims=True)
        var_hi = jnp.sum(jnp.where(lo, 0.0, d2), axis=-1, keepdims=True)
        y = d * lax.rsqrt(jnp.where(lo, var_lo, var_hi) * inv_n + EPS)
        g = g_ref[:, cols].astype(F32)
        o_ref[:, cols] = (g * _sigmoid(g) * (y * gain_ref[:, cols])).astype(o_ref.dtype)


def retention(proj, gain, *, chunk):
    bsz, s, _ = proj.shape
    npair = RET_HEADS // 2
    t = _retention_tables(s, chunk)

    width = npair * LANES

    def act(off):
        return pl.BlockSpec((None, chunk, width), lambda b, c: (b, c, off // npair))

    pos = pl.BlockSpec((chunk, LANES), lambda b, c: (c, 0))
    return pl.pallas_call(
        _retention_kernel,
        grid=(bsz, s // chunk),
        in_specs=[act(QC), act(KC), act(VC), act(GC), pos, pos,
                  _resident((npair, 2, chunk, chunk)), _resident((npair, chunk, LANES)),
                  _resident((npair, chunk, LANES)), _resident((npair, LANES, LANES)),
                  _resident((LANES, LANES)), _resident((1, width))],
        out_specs=pl.BlockSpec((None, chunk, width), lambda b, c: (b, c, 0)),
        out_shape=jax.ShapeDtypeStruct((bsz, s, width), BF16),
        scratch_shapes=[pltpu.VMEM((npair, LANES, LANES), F32)],
        compiler_params=_params(("arbitrary", "arbitrary")),
        name="retention",
    )(proj, proj, proj, proj, t["cos"], t["sin"], t["inner"], t["q_dec"], t["k_dec"],
      t["c_mat"], t["blk"], gain)


def _mix_residual(h_ref, oa_ref, ob_ref, oc_ref, wo_ref):
    wa = oa_ref.shape[1]
    wb = ob_ref.shape[1]
    return (h_ref[...] + _dot(oa_ref[...], wo_ref[0:wa, :])
            + _dot(ob_ref[...], wo_ref[wa:wa + wb, :])
            + _dot(oc_ref[...], wo_ref[wa + wb:, :]))


def _swiglu_accumulate(xn, wg_ref, wu_ref, wd_ref, acc_ref, chunk):
    def body(i, carry):
        c = pl.multiple_of(i * chunk, chunk)
        g = _dot(xn, wg_ref[:, pl.ds(c, chunk)])
        u = _dot(xn, wu_ref[:, pl.ds(c, chunk)])
        a = (g * _sigmoid(g) * u).astype(BF16)
        acc_ref[...] += _dot(a, wd_ref[pl.ds(c, chunk), :])
        return carry

    lax.fori_loop(0, wg_ref.shape[1] // chunk, body, 0)


def _ple(h, p_ref, pn_ref, pg_ref, pp_ref):
    gate = _sigmoid(_dot(_rms(h, pn_ref[...]).astype(BF16), pg_ref[...]))
    return h + gate * _dot(p_ref[...].astype(BF16), pp_ref[...])


def _dense_layer_kernel(h_ref, oa_ref, ob_ref, oc_ref, p_ref, wo_ref, gn_ref, wg_ref, wu_ref,
                        wd_ref, pn_ref, pg_ref, pp_ref, out_ref, acc_ref, *, ff_chunk):
    h1 = _mix_residual(h_ref, oa_ref, ob_ref, oc_ref, wo_ref)
    xn = _rms(h1, gn_ref[...]).astype(BF16)
    acc_ref[...] = h1
    _swiglu_accumulate(xn, wg_ref, wu_ref, wd_ref, acc_ref, ff_chunk)
    out_ref[...] = _ple(acc_ref[...], p_ref, pn_ref, pg_ref, pp_ref)


def dense_layer(h, oa, ob, oc, p, wo, gn, wg, wu, wd, pn, pg, pp, *, tm, ff_chunk):
    n, d = h.shape
    row = lambda w: pl.BlockSpec((tm, w), lambda i: (i, 0))
    weights = (wo, gn, wg, wu, wd, pn, pg, pp)
    return pl.pallas_call(
        functools.partial(_dense_layer_kernel, ff_chunk=ff_chunk),
        grid=(n // tm,),
        in_specs=[row(d), row(oa.shape[1]), row(ob.shape[1]), row(oc.shape[1]), row(p.shape[1])]
                 + [_resident(w.shape) for w in weights],
        out_specs=row(d),
        out_shape=jax.ShapeDtypeStruct((n, d), F32),
        scratch_shapes=[pltpu.VMEM((tm, d), F32)],
        compiler_params=_params(("arbitrary",)),
        name="dense_layer",
    )(h, oa, ob, oc, p, *weights)


def _store_slabs(ref, x):
    rows, d = x.shape
    nl = d // LANES
    for c in range(nl):
        ref[pl.ds(c, rows, stride=nl), :] = x[:, c * LANES:(c + 1) * LANES].astype(ref.dtype)


def _load_slabs(ref, nl, dtype=None):
    parts = [ref[pl.ds(c, ref.shape[0] // nl, stride=nl), :] for c in range(nl)]
    if dtype is not None:
        parts = [p.astype(dtype) for p in parts]
    return jnp.concatenate(parts, axis=1)


def _slab_at(ref, row, nl):
    return ref.at[pl.ds(pl.multiple_of(row * nl, nl), nl), :]


def _router_kernel(h_ref, oa_ref, ob_ref, oc_ref, wo_ref, gn_ref, wr_ref,
                   h1_ref, xn_ref, meta_ref, cnt_ref):
    @pl.when(pl.program_id(0) == 0)
    def _():
        cnt_ref[...] = jnp.zeros_like(cnt_ref)

    h1 = _mix_residual(h_ref, oa_ref, ob_ref, oc_ref, wo_ref)
    h1_ref[...] = h1
    xn = _rms(h1, gn_ref[...])
    _store_slabs(xn_ref, xn)

    x_hi, x_lo = _split_bf16(xn, 2)
    w_hi, w_lo = _split_bf16(wr_ref[...], 2)
    logits = _dot(x_hi, w_hi) + (_dot(x_hi, w_lo) + _dot(x_lo, w_hi))
    tm = logits.shape[0]
    lane = lax.broadcasted_iota(jnp.int32, logits.shape, 1)
    logits = jnp.where(lane < N_EXPERTS, logits, -jnp.inf)

    def top(vals):
        best = jnp.max(vals, axis=-1, keepdims=True)
        idx = jnp.min(jnp.where(vals == best, lane, LANES), axis=-1, keepdims=True)
        return best, idx

    v1, i1 = top(logits)
    v2, i2 = top(jnp.where(lane == i1, -jnp.inf, logits))
    e = jnp.exp(v2 - v1)
    g1 = 1.0 / (1.0 + e)
    g2 = e / (1.0 + e)

    oh1 = (lane == i1).astype(F32)
    oh2 = (lane == i2).astype(F32)
    both = oh1 + oh2
    r = lax.broadcasted_iota(jnp.int32, (tm, tm), 0)
    c = lax.broadcasted_iota(jnp.int32, (tm, tm), 1)
    before = _dot((c < r).astype(BF16), both.astype(BF16)) + cnt_ref[...]
    rank1 = jnp.sum(before * oh1, axis=-1, keepdims=True)
    rank2 = jnp.sum(before * oh2, axis=-1, keepdims=True)
    cnt_ref[...] += jnp.sum(both, axis=0, keepdims=True)

    meta = jnp.zeros(logits.shape, F32)
    for k, val in enumerate((i1.astype(F32), i2.astype(F32), rank1, rank2, g1, g2)):
        meta = jnp.where(lane == k, val, meta)
    meta_ref[...] = meta


def route_tokens(h, oa, ob, oc, wo, gn, wr, *, tm):
    n, d = h.shape
    row = lambda w: pl.BlockSpec((tm, w), lambda i: (i, 0))
    weights = (wo, gn, wr)
    return pl.pallas_call(
        _router_kernel,
        grid=(n // tm,),
        in_specs=[row(d), row(oa.shape[1]), row(ob.shape[1]), row(oc.shape[1])]
                 + [_resident(w.shape) for w in weights],
        out_specs=[row(d), pl.BlockSpec((tm * (d // LANES), LANES), lambda i: (i, 0)),
                   row(LANES), pl.BlockSpec((1, LANES), lambda i: (0, 0))],
        out_shape=[jax.ShapeDtypeStruct((n, d), F32),
                   jax.ShapeDtypeStruct((n * (d // LANES), LANES), F32),
                   jax.ShapeDtypeStruct((n, LANES), F32),
                   jax.ShapeDtypeStruct((1, LANES), F32)],
        compiler_params=_params(("arbitrary",)),
        name="router",
    )(h, oa, ob, oc, *weights)


def _dispatch_kernel(last_ref, pos_ref, x_ref, xs_ref, zero_scr, sem, zero_sem, *, tile, nl):
    tm = x_ref.shape[0] // nl
    zrows = zero_scr.shape[0] // nl

    @pl.when(pl.program_id(0) == 0)
    def _():
        zero_scr[...] = jnp.zeros_like(zero_scr)

        def zero_copy(e, q):
            start = pl.multiple_of((last_ref[e] + q * zrows) * nl, zrows * nl)
            return pltpu.make_async_copy(zero_scr, xs_ref.at[pl.ds(start, zrows * nl), :],
                                         zero_sem)

        for e in range(N_EXPERTS):
            for q in range(tile // zrows):
                zero_copy(e, q).start()
        for e in range(N_EXPERTS):
            for q in range(tile // zrows):
                zero_copy(e, q).wait()

        n_trailing = (xs_ref.shape[0] // nl - last_ref[N_EXPERTS]) // zrows

        def trailing(q, carry, wait):
            cp = zero_copy(N_EXPERTS, q)
            cp.wait() if wait else cp.start()
            return carry

        lax.fori_loop(0, n_trailing, functools.partial(trailing, wait=False), 0)
        lax.fori_loop(0, n_trailing, functools.partial(trailing, wait=True), 0)

    def row_copy(r, k):
        return pltpu.make_async_copy(_slab_at(x_ref, r, nl),
                                     _slab_at(xs_ref, pos_ref[0, k, r], nl), sem)

    _start_rows(tm, row_copy)
    for k in range(2):
        pltpu.make_async_copy(x_ref, xs_ref.at[pl.ds(0, tm * nl), :], sem).wait()


def _start_rows(n_rows, row_copy):
    def body(g, carry):
        base = pl.multiple_of(g * ROW_UNROLL, ROW_UNROLL)
        for j in range(ROW_UNROLL):
            row_copy(base + j, 0).start(priority=0)
            row_copy(base + j, 1).start(priority=1)
        return carry

    lax.fori_loop(0, n_rows // ROW_UNROLL, body, 0)


def dispatch(xn, pos, last_tile_start, n_slots, *, tm, tile, nl):
    n = xn.shape[0] // nl
    zrows = min(tile, 256)
    grid_spec = pltpu.PrefetchScalarGridSpec(
        num_scalar_prefetch=1,
        grid=(n // tm,),
        in_specs=[pl.BlockSpec((1, 2, tm), lambda i, last: (i, 0, 0), memory_space=pltpu.SMEM),
                  pl.BlockSpec((tm * nl, LANES), lambda i, last: (i, 0))],
        out_specs=pl.BlockSpec(memory_space=pl.ANY),
        scratch_shapes=[pltpu.VMEM((zrows * nl, LANES), xn.dtype),
                        pltpu.SemaphoreType.DMA(()), pltpu.SemaphoreType.DMA(())],
    )
    return pl.pallas_call(
        functools.partial(_dispatch_kernel, tile=tile, nl=nl),
        grid_spec=grid_spec,
        out_shape=jax.ShapeDtypeStruct((n_slots * nl, LANES), xn.dtype),
        compiler_params=_params(("arbitrary",)),
        name="moe_dispatch",
    )(last_tile_start, pos, xn)


def _experts_kernel(te_ref, nt_ref, x_ref, wg_ref, wu_ref, wd_ref, o_ref, x_scr, acc_ref, *,
                    sub_chunk):
    del te_ref
    t = pl.program_id(0)
    c = pl.program_id(1)
    last = pl.num_programs(1) - 1

    @pl.when(c == 0)
    def _():
        acc_ref[...] = jnp.zeros_like(acc_ref)

    @pl.when(t < nt_ref[0])
    def _():
        @pl.when(c == 0)
        def _():
            x_scr[...] = _load_slabs(x_ref, x_scr.shape[1] // LANES, BF16)

        _swiglu_accumulate(x_scr[...], wg_ref, wu_ref, wd_ref, acc_ref, sub_chunk)

    @pl.when(c == last)
    def _():
        _store_slabs(o_ref, acc_ref[...])


def experts(xs, tile_expert, n_tiles, wg, wu, wd, *, tile, ff_chunk, sub_chunk):
    d = wg.shape[1]
    nl = d // LANES
    n_slots = xs.shape[0] // nl
    d_ff = wg.shape[2]
    nc = d_ff // ff_chunk
    grid_tiles = n_slots // tile

    def tile_idx(t, c, te, nt):
        return (jnp.maximum(jnp.minimum(t, nt[0] - 1), 0), 0)

    def chunk_idx(t, c, nt):
        return jnp.where(t < nt[0], c, nc - 1)

    grid_spec = pltpu.PrefetchScalarGridSpec(
        num_scalar_prefetch=2,
        grid=(grid_tiles, nc),
        in_specs=[pl.BlockSpec((tile * nl, LANES), tile_idx),
                  pl.BlockSpec((None, d, ff_chunk), lambda t, c, te, nt: (te[t], 0, chunk_idx(t, c, nt))),
                  pl.BlockSpec((None, d, ff_chunk), lambda t, c, te, nt: (te[t], 0, chunk_idx(t, c, nt))),
                  pl.BlockSpec((None, ff_chunk, d), lambda t, c, te, nt: (te[t], chunk_idx(t, c, nt), 0))],
        out_specs=pl.BlockSpec((tile * nl, LANES), lambda t, c, te, nt: (t, 0)),
        scratch_shapes=[pltpu.VMEM((tile, d), BF16), pltpu.VMEM((tile, d), F32)],
    )
    return pl.pallas_call(
        functools.partial(_experts_kernel, sub_chunk=sub_chunk),
        grid_spec=grid_spec,
        out_shape=jax.ShapeDtypeStruct((n_slots * nl, LANES), F32),
        compiler_params=_params(("arbitrary", "arbitrary")),
        name="moe_experts",
    )(tile_expert, n_tiles, xs, wg, wu, wd)


def _combine_kernel(pos_ref, h_ref, meta_ref, p_ref, pn_ref, pg_ref, pp_ref, fn_ref, ys_ref,
                    out_ref, y1_scr, y2_scr, sem):
    tm = h_ref.shape[0]
    bufs = (y1_scr, y2_scr)

    nl = h_ref.shape[1] // LANES

    def row_copy(r, k):
        return pltpu.make_async_copy(_slab_at(ys_ref, pos_ref[0, k, r], nl),
                                     _slab_at(bufs[k], r, nl), sem)

    _start_rows(tm, row_copy)
    for buf in bufs:
        pltpu.make_async_copy(ys_ref.at[pl.ds(0, tm * nl), :], buf, sem).wait()

    meta = meta_ref[...]
    lane = lax.broadcasted_iota(jnp.int32, meta.shape, 1)
    g1 = jnp.sum(jnp.where(lane == 4, meta, 0.0), axis=-1, keepdims=True)
    g2 = jnp.sum(jnp.where(lane == 5, meta, 0.0), axis=-1, keepdims=True)
    h2 = h_ref[...] + g1 * _load_slabs(y1_scr, nl) + g2 * _load_slabs(y2_scr, nl)
    h3 = _ple(h2, p_ref, pn_ref, pg_ref, pp_ref)
    out_ref[...] = _rms(h3, fn_ref[...])


def combine(pos, h1, meta, p, pn, pg, pp, fn, ys, *, tm):
    n, d = h1.shape
    row = lambda w: pl.BlockSpec((tm, w), lambda i: (i, 0))
    weights = (pn, pg, pp, fn)
    return pl.pallas_call(
        _combine_kernel,
        grid=(n // tm,),
        in_specs=[pl.BlockSpec((1, 2, tm), lambda i: (i, 0, 0), memory_space=pltpu.SMEM),
                  row(d), row(LANES), row(p.shape[1])]
                 + [_resident(w.shape) for w in weights]
                 + [pl.BlockSpec(memory_space=pl.ANY)],
        out_specs=row(d),
        out_shape=jax.ShapeDtypeStruct((n, d), F32),
        scratch_shapes=[pltpu.VMEM((tm * (d // LANES), LANES), F32),
                        pltpu.VMEM((tm * (d // LANES), LANES), F32),
                        pltpu.SemaphoreType.DMA(())],
        compiler_params=_params(("arbitrary",)),
        name="moe_combine",
    )(pos, h1, meta, p, *weights, ys)


def _prep_w_in(w):
    scale = HEAD_DIM ** -0.5
    qa, ka, va = w[:, 0:512] * (scale * LOG2_E), w[:, 512:1024], w[:, 1024:1536]
    qb, kb, vb = w[:, 1536:1792] * (scale * LOG2_E), w[:, 1792:2048], w[:, 2048:2304]
    fb = w[:, 2304:2308]
    qc, kc, vc, gc = w[:, 2308:2564], w[:, 2564:2820] * scale, w[:, 2820:3076], w[:, 3076:3332]
    main = jnp.concatenate([qa, ka, qb, kb, qc, kc, vc, gc], axis=1).astype(BF16)
    f = jnp.pad(fb, ((0, 0), (0, LANES - fb.shape[1]))).astype(BF16)
    vt = jnp.concatenate([va, vb], axis=1).T.astype(BF16)
    return main, f, vt


def _pad_lanes(v):
    return jnp.pad(v, (0, LANES - v.shape[0]))[None, :]


class Tiles(NamedTuple):
    rows: int
    attn: int
    scan: int
    ret_chunk: int
    ff_chunk: int
    moe_tile: int
    moe_ff_chunk: int
    moe_sub_chunk: int


TILES = Tiles(rows=1024, attn=512, scan=512, ret_chunk=256, ff_chunk=256,
              moe_tile=1024, moe_ff_chunk=1792, moe_sub_chunk=256)


def _mixer(h, b, s, norm_g, w_in, b_forget, lq1, lk1, lq2, lk2, subln, ret_gn, layer_idx, t):
    w_main, w_f, w_vt = _prep_w_in(w_in)
    proj, f, vt = norm_proj(h, norm_g[None, :], w_main, w_f, w_vt, tm=t.rows, seq=s)
    proj = proj.reshape(b, s, -1)
    cum_col, cum_row = fox_prep(f.reshape(b, s, LANES), _pad_lanes(b_forget), blk=t.scan)
    lam_init = 0.8 - 0.6 * math.exp(-0.3 * layer_idx)
    lam = jnp.exp(jnp.sum(lq1 * lk1)) - jnp.exp(jnp.sum(lq2 * lk2)) + lam_init
    oa = diff_attention(proj, vt, lam.reshape(1), subln[None, :], tq=t.attn,
                        out_scale=1.0 - lam_init)
    ob = fox_attention(proj, vt, cum_col, cum_row, tq=t.attn)
    oc = retention(proj, ret_gn[None, :], chunk=t.ret_chunk)
    n = b * s
    return oa.reshape(n, -1), ob.reshape(n, -1), oc.reshape(n, -1)


def _route_plan(meta, counts, tile, n_slots, tm):
    n = meta.shape[0]
    idx = meta[:, 0:2].astype(jnp.int32)
    rank = meta[:, 2:4].astype(jnp.int32)
    cnt = counts[0, :N_EXPERTS].astype(jnp.int32)
    padded = ((cnt + tile - 1) // tile) * tile
    ends = jnp.cumsum(padded)
    starts = ends - padded
    pos = starts[idx] + rank
    n_tiles = ends[-1] // tile
    tiles = jnp.arange(n_slots // tile, dtype=jnp.int32)
    tile_expert = jnp.sum(tiles[:, None] >= (ends // tile)[None, :], axis=1)
    last = jnp.sum(n_tiles - 1 >= ends // tile)
    tile_expert = jnp.where(tiles < n_tiles, tile_expert, last).astype(jnp.int32)
    tile_expert = jnp.clip(tile_expert, 0, N_EXPERTS - 1)
    pos = jnp.clip(pos, 0, n_slots - 1)
    pos = pos.reshape(n // tm, tm, 2).transpose(0, 2, 1)
    last_tile_start = jnp.concatenate([jnp.clip(ends - tile, 0, n_slots - tile),
                                       jnp.clip(ends[-1:], 0, n_slots)]).astype(jnp.int32)
    return pos, tile_expert, n_tiles.reshape(1).astype(jnp.int32), last_tile_start


def forward(x, p, norm_mix, w_in, b_forget, lambda_q1, lambda_k1, lambda_q2, lambda_k2,
            diff_subln, ret_gn, w_out, norm_ffn, dense_w_gate, dense_w_up, dense_w_down,
            router, moe_w_gate, moe_w_up, moe_w_down, ple_norm, ple_gate, ple_proj, final_norm,
            t=TILES):
    assert w_in.shape[0] == 2, "two layers: dense SwiGLU, then routed experts"
    b, s, d = x.shape
    n = b * s
    h = x.reshape(n, d)
    pf = p.reshape(p.shape[0], n, p.shape[-1])
    bf = lambda a: a.astype(BF16)
    row = lambda a: a[None, :]

    oa, ob, oc = _mixer(h, b, s, norm_mix[0], w_in[0], b_forget[0], lambda_q1[0], lambda_k1[0],
                        lambda_q2[0], lambda_k2[0], diff_subln[0], ret_gn[0], 0, t)
    h = dense_layer(h, oa, ob, oc, pf[0], bf(w_out[0]), row(norm_ffn[0]), bf(dense_w_gate[0]),
                    bf(dense_w_up[0]), bf(dense_w_down[0]), row(ple_norm[0]), bf(ple_gate[0]),
                    bf(ple_proj[0]), tm=t.rows, ff_chunk=t.ff_chunk)

    oa, ob, oc = _mixer(h, b, s, norm_mix[1], w_in[1], b_forget[1], lambda_q1[1], lambda_k1[1],
                        lambda_q2[1], lambda_k2[1], diff_subln[1], ret_gn[1], 1, t)
    wr = jnp.pad(router[0], ((0, 0), (0, LANES - N_EXPERTS)))
    h1, xn, meta, counts = route_tokens(h, oa, ob, oc, bf(w_out[1]), row(norm_ffn[1]), wr,
                                         tm=t.rows)
    n_slots = 2 * n + N_EXPERTS * t.moe_tile
    pos, tile_expert, n_tiles, last_tile_start = _route_plan(meta, counts, t.moe_tile, n_slots,
                                                             t.rows)
    xs = dispatch(xn, pos, last_tile_start, n_slots, tm=t.rows, tile=t.moe_tile,
                  nl=d // LANES)
    ys = experts(xs, tile_expert, n_tiles, bf(moe_w_gate[0]), bf(moe_w_up[0]), bf(moe_w_down[0]),
                 tile=t.moe_tile, ff_chunk=t.moe_ff_chunk, sub_chunk=t.moe_sub_chunk)
    out = combine(pos, h1, meta, pf[1], row(ple_norm[1]), bf(ple_gate[1]), bf(ple_proj[1]),
                  row(final_norm), ys, tm=t.rows)
    return out.reshape(b, s, d)


def kernel(x, p, norm_mix, w_in, b_forget, lambda_q1, lambda_k1, lambda_q2, lambda_k2,
           diff_subln, ret_gn, w_out, norm_ffn, dense_w_gate, dense_w_up, dense_w_down,
           router, moe_w_gate, moe_w_up, moe_w_down, ple_norm, ple_gate, ple_proj, final_norm):
    return forward(x, p, norm_mix, w_in, b_forget, lambda_q1, lambda_k1, lambda_q2, lambda_k2,
                   diff_subln, ret_gn, w_out, norm_ffn, dense_w_gate, dense_w_up, dense_w_down,
                   router, moe_w_gate, moe_w_up, moe_w_down, ple_norm, ple_gate, ple_proj,
                   final_norm)
```

```python
import functools
import math
from typing import NamedTuple

import numpy as np
import jax
import jax.numpy as jnp
from jax import lax
from jax.experimental import pallas as pl
from jax.experimental.pallas import tpu as pltpu

F32 = jnp.float32
BF16 = jnp.bfloat16

LANES = 128
ROW_UNROLL = 8
VMEM_LIMIT_BYTES = 56 * 1024 * 1024

HEAD_DIM = 64
DIFF_HEADS = 4
FOX_HEADS = 4
RET_HEADS = 4
RET_THETA = 10000.0
N_EXPERTS = 8
EPS = 1e-6
NEG_INF = -1e30
LOG2_E = math.log2(math.e)

QA, KA = 0, 4
QB, KB = 8, 10
QC, KC, VC, GC = 12, 14, 16, 18
VT_A, VT_B = 0, 4


def _params(semantics):
    return pltpu.CompilerParams(dimension_semantics=semantics,
                                vmem_limit_bytes=VMEM_LIMIT_BYTES)


def _resident(shape):
    nd = len(shape)
    return pl.BlockSpec(shape, lambda *_: (0,) * nd, pipeline_mode=pl.Buffered(1))


def _rms(x, g):
    return x * lax.rsqrt(jnp.mean(x * x, axis=-1, keepdims=True) + EPS) * g


def _sigmoid(x):
    return 1.0 / (1.0 + jnp.exp(-x))


def _dot(a, b):
    return jnp.dot(a, b, preferred_element_type=F32)


def _dot_nt(a, b):
    return lax.dot_general(a, b, (((1,), (1,)), ((), ())), preferred_element_type=F32)


def _split_bf16(x, n):
    pieces = []
    for _ in range(n):
        piece = x.astype(BF16)
        pieces.append(piece)
        x = x - piece.astype(F32)
    return pieces


def _lane_masks(shape):
    lane = lax.broadcasted_iota(jnp.int32, shape, len(shape) - 1)
    return lane < HEAD_DIM


def _norm_proj_kernel(h_ref, g_ref, w_ref, wf_ref, wvt_ref, out_ref, f_ref, vt_ref, *,
                      col_chunk):
    xn = _rms(h_ref[...], g_ref[...]).astype(BF16)
    ncol = w_ref.shape[1]
    for c in range(0, ncol, col_chunk):
        w = min(col_chunk, ncol - c)
        out_ref[:, c:c + w] = _dot(xn, w_ref[:, c:c + w]).astype(out_ref.dtype)
    f_ref[...] = _dot(xn, wf_ref[...])
    vt_ref[...] = _dot_nt(wvt_ref[...], xn).astype(vt_ref.dtype)


def norm_proj(h, g, w_main, w_f, w_vt, *, tm, seq):
    n, d = h.shape
    ncol = w_main.shape[1]
    nvt = w_vt.shape[0]
    per_seq = seq // tm
    return pl.pallas_call(
        functools.partial(_norm_proj_kernel, col_chunk=512),
        grid=(n // tm,),
        in_specs=[pl.BlockSpec((tm, d), lambda i: (i, 0)),
                  _resident((1, d)), _resident((d, ncol)), _resident((d, LANES)),
                  _resident((nvt, d))],
        out_specs=[pl.BlockSpec((tm, ncol), lambda i: (i, 0)),
                   pl.BlockSpec((tm, LANES), lambda i: (i, 0)),
                   pl.BlockSpec((None, nvt, tm), lambda i: (i // per_seq, 0, i % per_seq))],
        out_shape=[jax.ShapeDtypeStruct((n, ncol), BF16),
                   jax.ShapeDtypeStruct((n, LANES), F32),
                   jax.ShapeDtypeStruct((n // seq, nvt, seq), BF16)],
        compiler_params=_params(("arbitrary",)),
        name="norm_proj",
    )(h, g, w_main, w_f, w_vt)


def _fox_prep_kernel(f_ref, b_ref, col_ref, row_ref, carry_ref):
    @pl.when(pl.program_id(1) == 0)
    def _():
        carry_ref[...] = jnp.zeros_like(carry_ref)

    x = f_ref[...] + b_ref[...]
    log_f = (jnp.minimum(x, 0.0) - jnp.log1p(jnp.exp(-jnp.abs(x)))) * LOG2_E
    blk = x.shape[0]
    r = lax.broadcasted_iota(jnp.int32, (blk, blk), 0)
    c = lax.broadcasted_iota(jnp.int32, (blk, blk), 1)
    tri = (c <= r).astype(BF16)
    cum = sum(_dot(tri, piece) for piece in _split_bf16(log_f, 3)) + carry_ref[...]
    col_ref[...] = cum
    row_ref[...] = cum.T[:8, :]
    carry_ref[...] = cum[blk - 1:blk, :]


def fox_prep(f, b_pad, *, blk):
    bsz, s, _ = f.shape
    return pl.pallas_call(
        _fox_prep_kernel,
        grid=(bsz, s // blk),
        in_specs=[pl.BlockSpec((None, blk, LANES), lambda b, j: (b, j, 0)),
                  pl.BlockSpec((1, LANES), lambda b, j: (0, 0))],
        out_specs=[pl.BlockSpec((None, blk, LANES), lambda b, j: (b, j, 0)),
                   pl.BlockSpec((None, 8, blk), lambda b, j: (b, 0, j))],
        out_shape=[jax.ShapeDtypeStruct((bsz, s, LANES), F32),
                   jax.ShapeDtypeStruct((bsz, 8, s), F32)],
        scratch_shapes=[pltpu.VMEM((1, LANES), F32)],
        compiler_params=_params(("arbitrary", "arbitrary")),
        name="fox_prep",
    )(f, b_pad)


def _attn_kernel(*refs, mode, tq, out_scale):
    if mode == "diff":
        lam_ref, q_ref, k_ref, vt_ref, gain_ref, o_ref = refs[:6]
    else:
        q_ref, k_ref, vt_ref, ccol_ref, crow_ref, o_ref = refs[:6]
        cs_scr = refs[16]
    state = (refs[6:9], refs[9:12])
    st_a, st_b = refs[12:14], refs[14:16]
    qi = pl.program_id(2)
    pair = pl.program_id(1)

    q = q_ref[...]
    lo = _lane_masks(q.shape)
    zero = jnp.zeros_like(q)
    q_maps = (jnp.where(lo, q, zero), jnp.where(lo, zero, q))

    for m_ref, l_ref, acc_ref in state:
        m_ref[...] = jnp.full_like(m_ref, NEG_INF)
        l_ref[...] = jnp.zeros_like(l_ref)
        acc_ref[...] = jnp.zeros_like(acc_ref)

    if mode == "fox":
        @pl.when(qi == 0)
        def _():
            ccol = ccol_ref[...]
            lane = lax.broadcasted_iota(jnp.int32, ccol.shape, 1)
            for i in range(2):
                col = jnp.sum(jnp.where(lane == 2 * pair + i, ccol, 0.0), axis=-1, keepdims=True)
                cs_scr[i] = jnp.broadcast_to(col, ccol.shape)

        ct = tuple(crow_ref[pl.ds(2 * pair + i, 1), :] for i in range(2))

    def scores(c, bufs):
        k = k_ref[pl.ds(pl.multiple_of(c * tq, tq), tq), :]
        for i in range(2):
            bufs[i][...] = _dot_nt(k, q_maps[i])

    def consume(c, bufs, masked):
        start = pl.multiple_of(c * tq, tq)
        vt = vt_ref[:, pl.ds(start, tq)]
        for i in range(2):
            m_ref, l_ref, acc_ref = state[i]
            st = bufs[i][...]
            if mode == "fox":
                cs = cs_scr[i, pl.ds(start, tq), :]
                st = st - jnp.tile(cs, (1, tq // LANES))
            if masked:
                key = lax.broadcasted_iota(jnp.int32, st.shape, 0)
                qry = lax.broadcasted_iota(jnp.int32, st.shape, 1)
                st = jnp.where(key <= qry, st, NEG_INF)
            m_old = m_ref[...]
            m_chunk = jnp.max(st, axis=0, keepdims=True)
            if mode == "fox":
                m_new = jnp.maximum(m_old, m_chunk + ct[i])
                p = jnp.exp2(st + (ct[i] - m_new))
            else:
                m_new = jnp.maximum(m_old, m_chunk)
                p = jnp.exp2(st - m_new)
            alpha = jnp.exp2(m_old - m_new)
            l_ref[...] = alpha * l_ref[...] + jnp.sum(p, axis=0, keepdims=True)
            acc_ref[...] = alpha * acc_ref[...] + _dot(vt, p.astype(BF16))
            m_ref[...] = m_new

    def body(j, carry):
        scores(2 * j + 1, st_b)
        consume(2 * j, st_a, False)
        scores(2 * j + 2, st_a)
        consume(2 * j + 1, st_b, False)
        return carry

    scores(0, st_a)
    lax.fori_loop(0, qi // 2, body, 0)

    @pl.when(qi % 2 == 0)
    def _():
        consume(qi, st_a, True)

    @pl.when(qi % 2 == 1)
    def _():
        scores(qi, st_b)
        consume(qi - 1, st_a, False)
        consume(qi, st_b, True)

    o1, o2 = (acc_ref[...] * (1.0 / l_ref[...]) for _, l_ref, acc_ref in state)
    if mode == "diff":
        o = (o1 - lam_ref[0] * o2).T
        o = _rms(o, gain_ref[...]) * out_scale
    else:
        chan = lax.broadcasted_iota(jnp.int32, o1.shape, 0)
        o = jnp.where(chan < HEAD_DIM, o1, o2).T
    o_ref[...] = o.astype(o_ref.dtype)


def _attn_scratch(tq):
    per_map = [pltpu.VMEM((1, tq), F32), pltpu.VMEM((1, tq), F32), pltpu.VMEM((LANES, tq), F32)]
    scores = [pltpu.VMEM((tq, tq), F32)] * 4
    return per_map + per_map + scores


def diff_attention(proj, vt, lam, gain, *, tq, out_scale):
    bsz, s, _ = proj.shape
    kern = functools.partial(_attn_kernel, mode="diff", tq=tq, out_scale=out_scale)
    return pl.pallas_call(
        kern,
        grid=(bsz, DIFF_HEADS, s // tq),
        in_specs=[pl.BlockSpec(memory_space=pltpu.SMEM),
                  pl.BlockSpec((None, tq, LANES), lambda b, h, i: (b, i, QA + h)),
                  pl.BlockSpec((None, s, LANES), lambda b, h, i: (b, 0, KA + h)),
                  pl.BlockSpec((None, LANES, s), lambda b, h, i: (b, VT_A + h, 0)),
                  pl.BlockSpec((1, LANES), lambda b, h, i: (0, 0))],
        out_specs=pl.BlockSpec((None, tq, LANES), lambda b, h, i: (b, i, h)),
        out_shape=jax.ShapeDtypeStruct((bsz, s, DIFF_HEADS * LANES), BF16),
        scratch_shapes=_attn_scratch(tq),
        compiler_params=_params(("arbitrary", "arbitrary", "arbitrary")),
        name="diff_attention",
    )(lam, proj, proj, vt, gain)


def fox_attention(proj, vt, cum_col, cum_row, *, tq):
    bsz, s, _ = proj.shape
    npair = FOX_HEADS // 2
    kern = functools.partial(_attn_kernel, mode="fox", tq=tq, out_scale=None)
    return pl.pallas_call(
        kern,
        grid=(bsz, npair, s // tq),
        in_specs=[pl.BlockSpec((None, tq, LANES), lambda b, h, i: (b, i, QB + h)),
                  pl.BlockSpec((None, s, LANES), lambda b, h, i: (b, 0, KB + h)),
                  pl.BlockSpec((None, LANES, s), lambda b, h, i: (b, VT_B + h, 0)),
                  pl.BlockSpec((None, s, LANES), lambda b, h, i: (b, 0, 0)),
                  pl.BlockSpec((None, 8, tq), lambda b, h, i: (b, 0, i))],
        out_specs=pl.BlockSpec((None, tq, LANES), lambda b, h, i: (b, i, h)),
        out_shape=jax.ShapeDtypeStruct((bsz, s, npair * LANES), BF16),
        scratch_shapes=_attn_scratch(tq) + [pltpu.VMEM((2, s, LANES), F32)],
        compiler_params=_params(("arbitrary", "arbitrary", "arbitrary")),
        name="fox_attention",
    )(proj, proj, vt, cum_col, cum_row)


def _retention_tables(s, chunk):
    half = HEAD_DIM // 2
    inv = RET_THETA ** (-np.arange(half, dtype=np.float64) / half)
    ang = np.arange(s, dtype=np.float64)[:, None] * inv[None, :]
    cos = np.tile(np.cos(ang), (1, 4))
    sin = np.tile(np.concatenate([-np.sin(ang), np.sin(ang)], axis=1), (1, 2))
    log_g = np.log1p(-np.exp2(-5.0 - np.arange(RET_HEADS, dtype=np.float64)))
    j = np.arange(chunk, dtype=np.float64)
    diff = j[:, None] - j[None, :]
    inner = np.where(diff[None] >= 0, np.exp(np.where(diff[None] >= 0, diff[None], 0.0)
                                             * log_g[:, None, None]), 0.0)
    q_dec = np.exp((j + 1.0)[None, :] * log_g[:, None])
    k_dec = np.exp((chunk - 1.0 - j)[None, :] * log_g[:, None])
    c_dec = np.exp(chunk * log_g)
    npair = RET_HEADS // 2

    def lanes(t):
        t = t.reshape(npair, 2, chunk)
        return np.repeat(t.transpose(0, 2, 1), HEAD_DIM, axis=2)

    blk = np.kron(np.eye(2), np.ones((HEAD_DIM, HEAD_DIM)))
    c_mat = np.stack([np.kron(np.diag(c_dec[2 * p:2 * p + 2]), np.ones((HEAD_DIM, HEAD_DIM)))
                      for p in range(npair)])
    f = lambda a: jnp.asarray(a, F32)
    return dict(cos=f(cos), sin=f(sin), inner=f(inner.reshape(npair, 2, chunk, chunk)),
                q_dec=f(lanes(q_dec)), k_dec=f(lanes(k_dec)), c_mat=f(c_mat), blk=f(blk))


def _retention_kernel(q_ref, k_ref, v_ref, g_ref, cos_ref, sin_ref, inner_ref, qdec_ref,
                      kdec_ref, cmat_ref, blk_ref, gain_ref, o_ref, state_ref):
    @pl.when(pl.program_id(1) == 0)
    def _():
        state_ref[...] = jnp.zeros_like(state_ref)

    cos = cos_ref[...]
    sin = sin_ref[...]
    lane = lax.broadcasted_iota(jnp.int32, cos.shape, 1)
    first_half = (lane % HEAD_DIM) < (HEAD_DIM // 2)
    lo = lane < HEAD_DIM

    def rope(x):
        x = x.astype(F32)
        swapped = jnp.where(first_half, pltpu.roll(x, LANES - HEAD_DIM // 2, 1),
                            pltpu.roll(x, HEAD_DIM // 2, 1))
        return x * cos + swapped * sin

    for p in range(RET_HEADS // 2):
        cols = slice(p * LANES, (p + 1) * LANES)
        qr = rope(q_ref[:, cols])
        kr = rope(k_ref[:, cols])
        v = v_ref[:, cols]
        qb = qr.astype(BF16)
        kb = kr.astype(BF16)
        zero = jnp.zeros_like(qb)
        a1 = (_dot_nt(jnp.where(lo, qb, zero), kb) * inner_ref[p, 0]).astype(BF16)
        a2 = (_dot_nt(jnp.where(lo, zero, qb), kb) * inner_ref[p, 1]).astype(BF16)
        state = state_ref[p]
        o = jnp.where(lo, _dot(a1, v), _dot(a2, v))
        o = o + _dot((qr * qdec_ref[p]).astype(BF16), state.astype(BF16))
        kd = (kr * kdec_ref[p]).astype(BF16)
        state_ref[p] = state * cmat_ref[p] + _dot(kd.T, v) * blk_ref[...]

        inv_n = 1.0 / HEAD_DIM
        sum_lo = jnp.sum(jnp.where(lo, o, 0.0), axis=-1, keepdims=True)
        sum_hi = jnp.sum(jnp.where(lo, 0.0, o), axis=-1, keepdims=True)
        d = o - jnp.where(lo, sum_lo, sum_hi) * inv_n
        d2 = d * d
        var_lo = jnp.sum(jnp.where(lo, d2, 0.0), axis=-1, keepdims=True)
        var_hi = jnp.sum(jnp.where(lo, 0.0, d2), axis=-1, keepdims=True)
        y = d * lax.rsqrt(jnp.where(lo, var_lo, var_hi) * inv_n + EPS)
        g = g_ref[:, cols].astype(F32)
        o_ref[:, cols] = (g * _sigmoid(g) * (y * gain_ref[:, cols])).astype(o_ref.dtype)


def retention(proj, gain, *, chunk):
    bsz, s, _ = proj.shape
    npair = RET_HEADS // 2
    t = _retention_tables(s, chunk)

    width = npair * LANES

    def act(off):
        return pl.BlockSpec((None, chunk, width), lambda b, c: (b, c, off // npair))

    pos = pl.BlockSpec((chunk, LANES), lambda b, c: (c, 0))
    return pl.pallas_call(
        _retention_kernel,
        grid=(bsz, s // chunk),
        in_specs=[act(QC), act(KC), act(VC), act(GC), pos, pos,
                  _resident((npair, 2, chunk, chunk)), _resident((npair, chunk, LANES)),
                  _resident((npair, chunk, LANES)), _resident((npair, LANES, LANES)),
                  _resident((LANES, LANES)), _resident((1, width))],
        out_specs=pl.BlockSpec((None, chunk, width), lambda b, c: (b, c, 0)),
        out_shape=jax.ShapeDtypeStruct((bsz, s, width), BF16),
        scratch_shapes=[pltpu.VMEM((npair, LANES, LANES), F32)],
        compiler_params=_params(("arbitrary", "arbitrary")),
        name="retention",
    )(proj, proj, proj, proj, t["cos"], t["sin"], t["inner"], t["q_dec"], t["k_dec"],
      t["c_mat"], t["blk"], gain)


def _mix_residual(h_ref, oa_ref, ob_ref, oc_ref, wo_ref):
    wa = oa_ref.shape[1]
    wb = ob_ref.shape[1]
    return (h_ref[...] + _dot(oa_ref[...], wo_ref[0:wa, :])
            + _dot(ob_ref[...], wo_ref[wa:wa + wb, :])
            + _dot(oc_ref[...], wo_ref[wa + wb:, :]))


def _swiglu_accumulate(xn, wg_ref, wu_ref, wd_ref, acc_ref, chunk):
    def body(i, carry):
        c = pl.multiple_of(i * chunk, chunk)
        g = _dot(xn, wg_ref[:, pl.ds(c, chunk)])
        u = _dot(xn, wu_ref[:, pl.ds(c, chunk)])
        a = (g * _sigmoid(g) * u).astype(BF16)
        acc_ref[...] += _dot(a, wd_ref[pl.ds(c, chunk), :])
        return carry

    lax.fori_loop(0, wg_ref.shape[1] // chunk, body, 0)


def _ple(h, p_ref, pn_ref, pg_ref, pp_ref):
    gate = _sigmoid(_dot(_rms(h, pn_ref[...]).astype(BF16), pg_ref[...]))
    return h + gate * _dot(p_ref[...].astype(BF16), pp_ref[...])


def _dense_layer_kernel(h_ref, oa_ref, ob_ref, oc_ref, p_ref, wo_ref, gn_ref, wg_ref, wu_ref,
                        wd_ref, pn_ref, pg_ref, pp_ref, out_ref, acc_ref, *, ff_chunk):
    h1 = _mix_residual(h_ref, oa_ref, ob_ref, oc_ref, wo_ref)
    xn = _rms(h1, gn_ref[...]).astype(BF16)
    acc_ref[...] = h1
    _swiglu_accumulate(xn, wg_ref, wu_ref, wd_ref, acc_ref, ff_chunk)
    out_ref[...] = _ple(acc_ref[...], p_ref, pn_ref, pg_ref, pp_ref)


def dense_layer(h, oa, ob, oc, p, wo, gn, wg, wu, wd, pn, pg, pp, *, tm, ff_chunk):
    n, d = h.shape
    row = lambda w: pl.BlockSpec((tm, w), lambda i: (i, 0))
    weights = (wo, gn, wg, wu, wd, pn, pg, pp)
    return pl.pallas_call(
        functools.partial(_dense_layer_kernel, ff_chunk=ff_chunk),
        grid=(n // tm,),
        in_specs=[row(d), row(oa.shape[1]), row(ob.shape[1]), row(oc.shape[1]), row(p.shape[1])]
                 + [_resident(w.shape) for w in weights],
        out_specs=row(d),
        out_shape=jax.ShapeDtypeStruct((n, d), F32),
        scratch_shapes=[pltpu.VMEM((tm, d), F32)],
        compiler_params=_params(("arbitrary",)),
        name="dense_layer",
    )(h, oa, ob, oc, p, *weights)


def _store_slabs(ref, x):
    rows, d = x.shape
    nl = d // LANES
    for c in range(nl):
        ref[pl.ds(c, rows, stride=nl), :] = x[:, c * LANES:(c + 1) * LANES].astype(ref.dtype)


def _load_slabs(ref, nl, dtype=None):
    parts = [ref[pl.ds(c, ref.shape[0] // nl, stride=nl), :] for c in range(nl)]
    if dtype is not None:
        parts = [p.astype(dtype) for p in parts]
    return jnp.concatenate(parts, axis=1)


def _slab_at(ref, row, nl):
    return ref.at[pl.ds(pl.multiple_of(row * nl, nl), nl), :]


def _router_kernel(h_ref, oa_ref, ob_ref, oc_ref, wo_ref, gn_ref, wr_ref,
                   h1_ref, xn_ref, meta_ref, cnt_ref):
    @pl.when(pl.program_id(0) == 0)
    def _():
        cnt_ref[...] = jnp.zeros_like(cnt_ref)

    h1 = _mix_residual(h_ref, oa_ref, ob_ref, oc_ref, wo_ref)
    h1_ref[...] = h1
    xn = _rms(h1, gn_ref[...])
    _store_slabs(xn_ref, xn)

    x_hi, x_lo = _split_bf16(xn, 2)
    w_hi, w_lo = _split_bf16(wr_ref[...], 2)
    logits = _dot(x_hi, w_hi) + (_dot(x_hi, w_lo) + _dot(x_lo, w_hi))
    tm = logits.shape[0]
    lane = lax.broadcasted_iota(jnp.int32, logits.shape, 1)
    logits = jnp.where(lane < N_EXPERTS, logits, -jnp.inf)

    def top(vals):
        best = jnp.max(vals, axis=-1, keepdims=True)
        idx = jnp.min(jnp.where(vals == best, lane, LANES), axis=-1, keepdims=True)
        return best, idx

    v1, i1 = top(logits)
    v2, i2 = top(jnp.where(lane == i1, -jnp.inf, logits))
    e = jnp.exp(v2 - v1)
    g1 = 1.0 / (1.0 + e)
    g2 = e / (1.0 + e)

    oh1 = (lane == i1).astype(F32)
    oh2 = (lane == i2).astype(F32)
    both = oh1 + oh2
    r = lax.broadcasted_iota(jnp.int32, (tm, tm), 0)
    c = lax.broadcasted_iota(jnp.int32, (tm, tm), 1)
    before = _dot((c < r).astype(BF16), both.astype(BF16)) + cnt_ref[...]
    rank1 = jnp.sum(before * oh1, axis=-1, keepdims=True)
    rank2 = jnp.sum(before * oh2, axis=-1, keepdims=True)
    cnt_ref[...] += jnp.sum(both, axis=0, keepdims=True)

    meta = jnp.zeros(logits.shape, F32)
    for k, val in enumerate((i1.astype(F32), i2.astype(F32), rank1, rank2, g1, g2)):
        meta = jnp.where(lane == k, val, meta)
    meta_ref[...] = meta


def route_tokens(h, oa, ob, oc, wo, gn, wr, *, tm):
    n, d = h.shape
    row = lambda w: pl.BlockSpec((tm, w), lambda i: (i, 0))
    weights = (wo, gn, wr)
    return pl.pallas_call(
        _router_kernel,
        grid=(n // tm,),
        in_specs=[row(d), row(oa.shape[1]), row(ob.shape[1]), row(oc.shape[1])]
                 + [_resident(w.shape) for w in weights],
        out_specs=[row(d), pl.BlockSpec((tm * (d // LANES), LANES), lambda i: (i, 0)),
                   row(LANES), pl.BlockSpec((1, LANES), lambda i: (0, 0))],
        out_shape=[jax.ShapeDtypeStruct((n, d), F32),
                   jax.ShapeDtypeStruct((n * (d // LANES), LANES), F32),
                   jax.ShapeDtypeStruct((n, LANES), F32),
                   jax.ShapeDtypeStruct((1, LANES), F32)],
        compiler_params=_params(("arbitrary",)),
        name="router",
    )(h, oa, ob, oc, *weights)


def _dispatch_kernel(last_ref, pos_ref, x_ref, xs_ref, zero_scr, sem, zero_sem, *, tile, nl):
    tm = x_ref.shape[0] // nl
    zrows = zero_scr.shape[0] // nl

    @pl.when(pl.program_id(0) == 0)
    def _():
        zero_scr[...] = jnp.zeros_like(zero_scr)

        def zero_copy(e, q):
            start = pl.multiple_of((last_ref[e] + q * zrows) * nl, zrows * nl)
            return pltpu.make_async_copy(zero_scr, xs_ref.at[pl.ds(start, zrows * nl), :],
                                         zero_sem)

        for e in range(N_EXPERTS):
            for q in range(tile // zrows):
                zero_copy(e, q).start()
        for e in range(N_EXPERTS):
            for q in range(tile // zrows):
                zero_copy(e, q).wait()

        n_trailing = (xs_ref.shape[0] // nl - last_ref[N_EXPERTS]) // zrows

        def trailing(q, carry, wait):
            cp = zero_copy(N_EXPERTS, q)
            cp.wait() if wait else cp.start()
            return carry

        lax.fori_loop(0, n_trailing, functools.partial(trailing, wait=False), 0)
        lax.fori_loop(0, n_trailing, functools.partial(trailing, wait=True), 0)

    def row_copy(r, k):
        return pltpu.make_async_copy(_slab_at(x_ref, r, nl),
                                     _slab_at(xs_ref, pos_ref[0, k, r], nl), sem)

    _start_rows(tm, row_copy)
    for k in range(2):
        pltpu.make_async_copy(x_ref, xs_ref.at[pl.ds(0, tm * nl), :], sem).wait()


def _start_rows(n_rows, row_copy):
    def body(g, carry):
        base = pl.multiple_of(g * ROW_UNROLL, ROW_UNROLL)
        for j in range(ROW_UNROLL):
            row_copy(base + j, 0).start(priority=0)
            row_copy(base + j, 1).start(priority=1)
        return carry

    lax.fori_loop(0, n_rows // ROW_UNROLL, body, 0)


def dispatch(xn, pos, last_tile_start, n_slots, *, tm, tile, nl):
    n = xn.shape[0] // nl
    zrows = min(tile, 256)
    grid_spec = pltpu.PrefetchScalarGridSpec(
        num_scalar_prefetch=1,
        grid=(n // tm,),
        in_specs=[pl.BlockSpec((1, 2, tm), lambda i, last: (i, 0, 0), memory_space=pltpu.SMEM),
                  pl.BlockSpec((tm * nl, LANES), lambda i, last: (i, 0))],
        out_specs=pl.BlockSpec(memory_space=pl.ANY),
        scratch_shapes=[pltpu.VMEM((zrows * nl, LANES), xn.dtype),
                        pltpu.SemaphoreType.DMA(()), pltpu.SemaphoreType.DMA(())],
    )
    return pl.pallas_call(
        functools.partial(_dispatch_kernel, tile=tile, nl=nl),
        grid_spec=grid_spec,
        out_shape=jax.ShapeDtypeStruct((n_slots * nl, LANES), xn.dtype),
        compiler_params=_params(("arbitrary",)),
        name="moe_dispatch",
    )(last_tile_start, pos, xn)


def _experts_kernel(te_ref, nt_ref, rows_ref, x_ref, wg_ref, wu_ref, wd_ref, o_ref, x_scr,
                    acc_ref, *, sub_chunk):
    del te_ref, nt_ref
    t = pl.program_id(0)
    c = pl.program_id(1)
    last = pl.num_programs(1) - 1
    rows = rows_ref[t]
    half = x_scr.shape[0] // 2

    @pl.when(c == 0)
    def _():
        acc_ref[...] = jnp.zeros_like(acc_ref)

    @pl.when(jnp.logical_and(rows > 0, c == 0))
    def _():
        x_scr[...] = _load_slabs(x_ref, x_scr.shape[1] // LANES, BF16)

    @pl.when(rows > half)
    def _():
        _swiglu_accumulate(x_scr[...], wg_ref, wu_ref, wd_ref, acc_ref, sub_chunk)

    @pl.when(jnp.logical_and(rows > 0, rows <= half))
    def _():
        _swiglu_accumulate(x_scr[:half, :], wg_ref, wu_ref, wd_ref,
                           acc_ref.at[pl.ds(0, half), :], sub_chunk)

    @pl.when(c == last)
    def _():
        _store_slabs(o_ref, acc_ref[...])


def experts(xs, tile_expert, n_tiles, tile_rows, wg, wu, wd, *, tile, ff_chunk, sub_chunk):
    d = wg.shape[1]
    nl = d // LANES
    n_slots = xs.shape[0] // nl
    d_ff = wg.shape[2]
    nc = d_ff // ff_chunk
    grid_tiles = n_slots // tile

    def tile_idx(t, c, te, nt, tr):
        return (jnp.maximum(jnp.minimum(t, nt[0] - 1), 0), 0)

    def chunk_idx(t, c, nt):
        return jnp.where(t < nt[0], c, nc - 1)

    grid_spec = pltpu.PrefetchScalarGridSpec(
        num_scalar_prefetch=3,
        grid=(grid_tiles, nc),
        in_specs=[pl.BlockSpec((tile * nl, LANES), tile_idx),
                  pl.BlockSpec((None, d, ff_chunk),
                               lambda t, c, te, nt, tr: (te[t], 0, chunk_idx(t, c, nt))),
                  pl.BlockSpec((None, d, ff_chunk),
                               lambda t, c, te, nt, tr: (te[t], 0, chunk_idx(t, c, nt))),
                  pl.BlockSpec((None, ff_chunk, d),
                               lambda t, c, te, nt, tr: (te[t], chunk_idx(t, c, nt), 0))],
        out_specs=pl.BlockSpec((tile * nl, LANES), lambda t, c, te, nt, tr: (t, 0)),
        scratch_shapes=[pltpu.VMEM((tile, d), BF16), pltpu.VMEM((tile, d), F32)],
    )
    return pl.pallas_call(
        functools.partial(_experts_kernel, sub_chunk=sub_chunk),
        grid_spec=grid_spec,
        out_shape=jax.ShapeDtypeStruct((n_slots * nl, LANES), F32),
        compiler_params=_params(("arbitrary", "arbitrary")),
        name="moe_experts",
    )(tile_expert, n_tiles, tile_rows, xs, wg, wu, wd)


def _combine_kernel(pos_ref, h_ref, meta_ref, p_ref, pn_ref, pg_ref, pp_ref, fn_ref, ys_ref,
                    out_ref, y1_scr, y2_scr, sem):
    tm = h_ref.shape[0]
    bufs = (y1_scr, y2_scr)

    nl = h_ref.shape[1] // LANES

    def row_copy(r, k):
        return pltpu.make_async_copy(_slab_at(ys_ref, pos_ref[0, k, r], nl),
                                     _slab_at(bufs[k], r, nl), sem)

    _start_rows(tm, row_copy)
    for buf in bufs:
        pltpu.make_async_copy(ys_ref.at[pl.ds(0, tm * nl), :], buf, sem).wait()

    meta = meta_ref[...]
    lane = lax.broadcasted_iota(jnp.int32, meta.shape, 1)
    g1 = jnp.sum(jnp.where(lane == 4, meta, 0.0), axis=-1, keepdims=True)
    g2 = jnp.sum(jnp.where(lane == 5, meta, 0.0), axis=-1, keepdims=True)
    h2 = h_ref[...] + g1 * _load_slabs(y1_scr, nl) + g2 * _load_slabs(y2_scr, nl)
    h3 = _ple(h2, p_ref, pn_ref, pg_ref, pp_ref)
    out_ref[...] = _rms(h3, fn_ref[...])


def combine(pos, h1, meta, p, pn, pg, pp, fn, ys, *, tm):
    n, d = h1.shape
    row = lambda w: pl.BlockSpec((tm, w), lambda i: (i, 0))
    weights = (pn, pg, pp, fn)
    return pl.pallas_call(
        _combine_kernel,
        grid=(n // tm,),
        in_specs=[pl.BlockSpec((1, 2, tm), lambda i: (i, 0, 0), memory_space=pltpu.SMEM),
                  row(d), row(LANES), row(p.shape[1])]
                 + [_resident(w.shape) for w in weights]
                 + [pl.BlockSpec(memory_space=pl.ANY)],
        out_specs=row(d),
        out_shape=jax.ShapeDtypeStruct((n, d), F32),
        scratch_shapes=[pltpu.VMEM((tm * (d // LANES), LANES), F32),
                        pltpu.VMEM((tm * (d // LANES), LANES), F32),
                        pltpu.SemaphoreType.DMA(())],
        compiler_params=_params(("arbitrary",)),
        name="moe_combine",
    )(pos, h1, meta, p, *weights, ys)


def _prep_w_in(w):
    scale = HEAD_DIM ** -0.5
    qa, ka, va = w[:, 0:512] * (scale * LOG2_E), w[:, 512:1024], w[:, 1024:1536]
    qb, kb, vb = w[:, 1536:1792] * (scale * LOG2_E), w[:, 1792:2048], w[:, 2048:2304]
    fb = w[:, 2304:2308]
    qc, kc, vc, gc = w[:, 2308:2564], w[:, 2564:2820] * scale, w[:, 2820:3076], w[:, 3076:3332]
    main = jnp.concatenate([qa, ka, qb, kb, qc, kc, vc, gc], axis=1).astype(BF16)
    f = jnp.pad(fb, ((0, 0), (0, LANES - fb.shape[1]))).astype(BF16)
    vt = jnp.concatenate([va, vb], axis=1).T.astype(BF16)
    return main, f, vt


def _pad_lanes(v):
    return jnp.pad(v, (0, LANES - v.shape[0]))[None, :]


class Tiles(NamedTuple):
    rows: int
    attn: int
    scan: int
    ret_chunk: int
    ff_chunk: int
    moe_tile: int
    moe_ff_chunk: int
    moe_sub_chunk: int


TILES = Tiles(rows=1024, attn=512, scan=512, ret_chunk=256, ff_chunk=256,
              moe_tile=1024, moe_ff_chunk=1792, moe_sub_chunk=256)


def _mixer(h, b, s, norm_g, w_in, b_forget, lq1, lk1, lq2, lk2, subln, ret_gn, layer_idx, t):
    w_main, w_f, w_vt = _prep_w_in(w_in)
    proj, f, vt = norm_proj(h, norm_g[None, :], w_main, w_f, w_vt, tm=t.rows, seq=s)
    proj = proj.reshape(b, s, -1)
    cum_col, cum_row = fox_prep(f.reshape(b, s, LANES), _pad_lanes(b_forget), blk=t.scan)
    lam_init = 0.8 - 0.6 * math.exp(-0.3 * layer_idx)
    lam = jnp.exp(jnp.sum(lq1 * lk1)) - jnp.exp(jnp.sum(lq2 * lk2)) + lam_init
    oa = diff_attention(proj, vt, lam.reshape(1), subln[None, :], tq=t.attn,
                        out_scale=1.0 - lam_init)
    ob = fox_attention(proj, vt, cum_col, cum_row, tq=t.attn)
    oc = retention(proj, ret_gn[None, :], chunk=t.ret_chunk)
    n = b * s
    return oa.reshape(n, -1), ob.reshape(n, -1), oc.reshape(n, -1)


def _route_plan(meta, counts, tile, n_slots, tm):
    n = meta.shape[0]
    idx = meta[:, 0:2].astype(jnp.int32)
    rank = meta[:, 2:4].astype(jnp.int32)
    cnt = counts[0, :N_EXPERTS].astype(jnp.int32)
    padded = ((cnt + tile - 1) // tile) * tile
    ends = jnp.cumsum(padded)
    starts = ends - padded
    pos = starts[idx] + rank
    n_tiles = ends[-1] // tile
    tiles = jnp.arange(n_slots // tile, dtype=jnp.int32)
    tile_expert = jnp.sum(tiles[:, None] >= (ends // tile)[None, :], axis=1)
    last = jnp.sum(n_tiles - 1 >= ends // tile)
    tile_expert = jnp.where(tiles < n_tiles, tile_expert, last).astype(jnp.int32)
    tile_expert = jnp.clip(tile_expert, 0, N_EXPERTS - 1)
    pos = jnp.clip(pos, 0, n_slots - 1)
    tile_rows = jnp.clip(cnt[tile_expert] - (tiles * tile - starts[tile_expert]), 0, tile)
    tile_rows = jnp.where(tiles < n_tiles, tile_rows, 0).astype(jnp.int32)
    pos = pos.reshape(n // tm, tm, 2).transpose(0, 2, 1)
    last_tile_start = jnp.concatenate([jnp.clip(ends - tile, 0, n_slots - tile),
                                       jnp.clip(ends[-1:], 0, n_slots)]).astype(jnp.int32)
    return pos, tile_expert, n_tiles.reshape(1).astype(jnp.int32), tile_rows, last_tile_start


def forward(x, p, norm_mix, w_in, b_forget, lambda_q1, lambda_k1, lambda_q2, lambda_k2,
            diff_subln, ret_gn, w_out, norm_ffn, dense_w_gate, dense_w_up, dense_w_down,
            router, moe_w_gate, moe_w_up, moe_w_down, ple_norm, ple_gate, ple_proj, final_norm,
            t=TILES):
    assert w_in.shape[0] == 2, "two layers: dense SwiGLU, then routed experts"
    b, s, d = x.shape
    n = b * s
    h = x.reshape(n, d)
    pf = p.reshape(p.shape[0], n, p.shape[-1])
    bf = lambda a: a.astype(BF16)
    row = lambda a: a[None, :]

    oa, ob, oc = _mixer(h, b, s, norm_mix[0], w_in[0], b_forget[0], lambda_q1[0], lambda_k1[0],
                        lambda_q2[0], lambda_k2[0], diff_subln[0], ret_gn[0], 0, t)
    h = dense_layer(h, oa, ob, oc, pf[0], bf(w_out[0]), row(norm_ffn[0]), bf(dense_w_gate[0]),
                    bf(dense_w_up[0]), bf(dense_w_down[0]), row(ple_norm[0]), bf(ple_gate[0]),
                    bf(ple_proj[0]), tm=t.rows, ff_chunk=t.ff_chunk)

    oa, ob, oc = _mixer(h, b, s, norm_mix[1], w_in[1], b_forget[1], lambda_q1[1], lambda_k1[1],
                        lambda_q2[1], lambda_k2[1], diff_subln[1], ret_gn[1], 1, t)
    wr = jnp.pad(router[0], ((0, 0), (0, LANES - N_EXPERTS)))
    h1, xn, meta, counts = route_tokens(h, oa, ob, oc, bf(w_out[1]), row(norm_ffn[1]), wr,
                                         tm=t.rows)
    n_slots = 2 * n + N_EXPERTS * t.moe_tile
    pos, tile_expert, n_tiles, tile_rows, last_tile_start = _route_plan(
        meta, counts, t.moe_tile, n_slots, t.rows)
    xs = dispatch(xn, pos, last_tile_start, n_slots, tm=t.rows, tile=t.moe_tile,
                  nl=d // LANES)
    ys = experts(xs, tile_expert, n_tiles, tile_rows, bf(moe_w_gate[0]), bf(moe_w_up[0]),
                 bf(moe_w_down[0]), tile=t.moe_tile, ff_chunk=t.moe_ff_chunk,
                 sub_chunk=t.moe_sub_chunk)
    out = combine(pos, h1, meta, pf[1], row(ple_norm[1]), bf(ple_gate[1]), bf(ple_proj[1]),
                  row(final_norm), ys, tm=t.rows)
    return out.reshape(b, s, d)


def kernel(x, p, norm_mix, w_in, b_forget, lambda_q1, lambda_k1, lambda_q2, lambda_k2,
           diff_subln, ret_gn, w_out, norm_ffn, dense_w_gate, dense_w_up, dense_w_down,
           router, moe_w_gate, moe_w_up, moe_w_down, ple_norm, ple_gate, ple_proj, final_norm):
    return forward(x, p, norm_mix, w_in, b_forget, lambda_q1, lambda_k1, lambda_q2, lambda_k2,
                   diff_subln, ret_gn, w_out, norm_ffn, dense_w_gate, dense_w_up, dense_w_down,
                   router, moe_w_gate, moe_w_up, moe_w_down, ple_norm, ple_gate, ple_proj,
                   final_norm)
```

```python
import functools
import math
from typing import NamedTuple

import numpy as np
import jax
import jax.numpy as jnp
from jax import lax
from jax.experimental import pallas as pl
from jax.experimental.pallas import tpu as pltpu

F32 = jnp.float32
BF16 = jnp.bfloat16

LANES = 128
SUBLANES = 8
VMEM_LIMIT_BYTES = 56 * 1024 * 1024

ROW_UNROLL = 8

HEAD_DIM = 64
DIFF_HEADS = 4
FOX_HEADS = 4
RET_HEADS = 4
RET_THETA = 10000.0
N_EXPERTS = 8
EPS = 1e-6
NEG_INF = -1e30
LOG2_E = math.log2(math.e)

QA, KA = 0, 4
QB, KB = 8, 10
QC, KC, VC, GC = 12, 14, 16, 18
VT_A, VT_B = 0, 4


def _params(semantics):
    return pltpu.CompilerParams(dimension_semantics=semantics,
                                vmem_limit_bytes=VMEM_LIMIT_BYTES)


def _resident(shape):
    nd = len(shape)
    return pl.BlockSpec(shape, lambda *_: (0,) * nd, pipeline_mode=pl.Buffered(1))


def _rms(x, g):
    return x * lax.rsqrt(jnp.mean(x * x, axis=-1, keepdims=True) + EPS) * g


def _sigmoid(x):
    return 1.0 / (1.0 + jnp.exp(-x))


def _dot(a, b):
    return jnp.dot(a, b, preferred_element_type=F32)


def _dot_nt(a, b):
    return lax.dot_general(a, b, (((1,), (1,)), ((), ())), preferred_element_type=F32)


def _split_bf16(x, n):
    pieces = []
    for _ in range(n):
        piece = x.astype(BF16)
        pieces.append(piece)
        x = x - piece.astype(F32)
    return pieces


def _lane_masks(shape):
    lane = lax.broadcasted_iota(jnp.int32, shape, len(shape) - 1)
    return lane < HEAD_DIM


def _norm_proj_kernel(h_ref, g_ref, w_ref, wf_ref, wvt_ref, out_ref, f_ref, vt_ref, *,
                      col_chunk):
    xn = _rms(h_ref[...], g_ref[...]).astype(BF16)
    ncol = w_ref.shape[1]
    for c in range(0, ncol, col_chunk):
        w = min(col_chunk, ncol - c)
        out_ref[:, c:c + w] = _dot(xn, w_ref[:, c:c + w]).astype(out_ref.dtype)
    f_ref[...] = _dot(xn, wf_ref[...])
    vt_ref[...] = _dot_nt(wvt_ref[...], xn).astype(vt_ref.dtype)


def norm_proj(h, g, w_main, w_f, w_vt, *, tm, seq, col_chunk):
    n, d = h.shape
    ncol = w_main.shape[1]
    nvt = w_vt.shape[0]
    per_seq = seq // tm
    return pl.pallas_call(
        functools.partial(_norm_proj_kernel, col_chunk=col_chunk),
        grid=(n // tm,),
        in_specs=[pl.BlockSpec((tm, d), lambda i: (i, 0)),
                  _resident((1, d)), _resident((d, ncol)), _resident((d, LANES)),
                  _resident((nvt, d))],
        out_specs=[pl.BlockSpec((tm, ncol), lambda i: (i, 0)),
                   pl.BlockSpec((tm, LANES), lambda i: (i, 0)),
                   pl.BlockSpec((None, nvt, tm), lambda i: (i // per_seq, 0, i % per_seq))],
        out_shape=[jax.ShapeDtypeStruct((n, ncol), BF16),
                   jax.ShapeDtypeStruct((n, LANES), F32),
                   jax.ShapeDtypeStruct((n // seq, nvt, seq), BF16)],
        compiler_params=_params(("arbitrary",)),
        name="norm_proj",
    )(h, g, w_main, w_f, w_vt)


def _fox_prep_kernel(f_ref, b_ref, col_ref, row_ref, carry_ref):
    @pl.when(pl.program_id(1) == 0)
    def _():
        carry_ref[...] = jnp.zeros_like(carry_ref)

    x = f_ref[...] + b_ref[...]
    log_f = (jnp.minimum(x, 0.0) - jnp.log1p(jnp.exp(-jnp.abs(x)))) * LOG2_E
    blk = x.shape[0]
    r = lax.broadcasted_iota(jnp.int32, (blk, blk), 0)
    c = lax.broadcasted_iota(jnp.int32, (blk, blk), 1)
    tri = (c <= r).astype(BF16)
    cum = sum(_dot(tri, piece) for piece in _split_bf16(log_f, 3)) + carry_ref[...]
    col_ref[...] = cum
    row_ref[...] = cum.T[:SUBLANES, :]
    carry_ref[...] = cum[blk - 1:blk, :]


def fox_prep(f, b_pad, *, blk):
    bsz, s, _ = f.shape
    return pl.pallas_call(
        _fox_prep_kernel,
        grid=(bsz, s // blk),
        in_specs=[pl.BlockSpec((None, blk, LANES), lambda b, j: (b, j, 0)),
                  pl.BlockSpec((1, LANES), lambda b, j: (0, 0))],
        out_specs=[pl.BlockSpec((None, blk, LANES), lambda b, j: (b, j, 0)),
                   pl.BlockSpec((None, SUBLANES, blk), lambda b, j: (b, 0, j))],
        out_shape=[jax.ShapeDtypeStruct((bsz, s, LANES), F32),
                   jax.ShapeDtypeStruct((bsz, SUBLANES, s), F32)],
        scratch_shapes=[pltpu.VMEM((1, LANES), F32)],
        compiler_params=_params(("arbitrary", "arbitrary")),
        name="fox_prep",
    )(f, b_pad)


def _attn_kernel(*refs, mode, tq, out_scale):
    if mode == "diff":
        lam_ref, q_ref, k_ref, vt_ref, gain_ref, o_ref = refs[:6]
    else:
        q_ref, k_ref, vt_ref, ccol_ref, crow_ref, o_ref = refs[:6]
        cs_scr = refs[16]
    state = (refs[6:9], refs[9:12])
    st_a, st_b = refs[12:14], refs[14:16]
    qi = pl.program_id(2)
    pair = pl.program_id(1)

    q = q_ref[...]
    lo = _lane_masks(q.shape)
    zero = jnp.zeros_like(q)
    q_maps = (jnp.where(lo, q, zero), jnp.where(lo, zero, q))

    for m_ref, l_ref, acc_ref in state:
        m_ref[...] = jnp.full_like(m_ref, NEG_INF)
        l_ref[...] = jnp.zeros_like(l_ref)
        acc_ref[...] = jnp.zeros_like(acc_ref)

    if mode == "fox":
        @pl.when(qi == 0)
        def _():
            ccol = ccol_ref[...]
            lane = lax.broadcasted_iota(jnp.int32, ccol.shape, 1)
            for i in range(2):
                col = jnp.sum(jnp.where(lane == 2 * pair + i, ccol, 0.0), axis=-1, keepdims=True)
                cs_scr[i] = jnp.broadcast_to(col, ccol.shape)

        ct = tuple(crow_ref[pl.ds(2 * pair + i, 1), :] for i in range(2))

    def scores(c, bufs):
        k = k_ref[pl.ds(pl.multiple_of(c * tq, tq), tq), :]
        for i in range(2):
            bufs[i][...] = _dot_nt(k, q_maps[i])

    def consume(c, bufs, masked):
        start = pl.multiple_of(c * tq, tq)
        vt = vt_ref[:, pl.ds(start, tq)]
        for i in range(2):
            m_ref, l_ref, acc_ref = state[i]
            st = bufs[i][...]
            if mode == "fox":
                cs = cs_scr[i, pl.ds(start, tq), :]
                st = st - jnp.tile(cs, (1, tq // LANES))
            if masked:
                key = lax.broadcasted_iota(jnp.int32, st.shape, 0)
                qry = lax.broadcasted_iota(jnp.int32, st.shape, 1)
                st = jnp.where(key <= qry, st, NEG_INF)
            m_old = m_ref[...]
            m_chunk = jnp.max(st, axis=0, keepdims=True)
            if mode == "fox":
                m_new = jnp.maximum(m_old, m_chunk + ct[i])
                p = jnp.exp2(st + (ct[i] - m_new))
            else:
                m_new = jnp.maximum(m_old, m_chunk)
                p = jnp.exp2(st - m_new)
            alpha = jnp.exp2(m_old - m_new)
            l_ref[...] = alpha * l_ref[...] + jnp.sum(p, axis=0, keepdims=True)
            acc_ref[...] = alpha * acc_ref[...] + _dot(vt, p.astype(BF16))
            m_ref[...] = m_new

    def body(j, carry):
        scores(2 * j + 1, st_b)
        consume(2 * j, st_a, False)
        scores(2 * j + 2, st_a)
        consume(2 * j + 1, st_b, False)
        return carry

    scores(0, st_a)
    lax.fori_loop(0, qi // 2, body, 0)

    @pl.when(qi % 2 == 0)
    def _():
        consume(qi, st_a, True)

    @pl.when(qi % 2 == 1)
    def _():
        scores(qi, st_b)
        consume(qi - 1, st_a, False)
        consume(qi, st_b, True)

    o1, o2 = (acc_ref[...] * (1.0 / l_ref[...]) for _, l_ref, acc_ref in state)
    if mode == "diff":
        ot = o1 - lam_ref[0] * o2
        ot = ot * lax.rsqrt(jnp.mean(ot * ot, axis=0, keepdims=True) + EPS)
        o = ot.T * (gain_ref[...] * out_scale)
    else:
        chan = lax.broadcasted_iota(jnp.int32, o1.shape, 0)
        o = jnp.where(chan < HEAD_DIM, o1, o2).T
    o_ref[...] = o.astype(o_ref.dtype)


def _attn_scratch(tq):
    per_map = [pltpu.VMEM((1, tq), F32), pltpu.VMEM((1, tq), F32), pltpu.VMEM((LANES, tq), F32)]
    scores = [pltpu.VMEM((tq, tq), F32)] * 4
    return per_map + per_map + scores


def diff_attention(proj, vt, lam, gain, *, tq, out_scale):
    bsz, s, _ = proj.shape
    kern = functools.partial(_attn_kernel, mode="diff", tq=tq, out_scale=out_scale)
    return pl.pallas_call(
        kern,
        grid=(bsz, DIFF_HEADS, s // tq),
        in_specs=[pl.BlockSpec(memory_space=pltpu.SMEM),
                  pl.BlockSpec((None, tq, LANES), lambda b, h, i: (b, i, QA + h)),
                  pl.BlockSpec((None, s, LANES), lambda b, h, i: (b, 0, KA + h)),
                  pl.BlockSpec((None, LANES, s), lambda b, h, i: (b, VT_A + h, 0)),
                  pl.BlockSpec((1, LANES), lambda b, h, i: (0, 0))],
        out_specs=pl.BlockSpec((None, tq, LANES), lambda b, h, i: (b, i, h)),
        out_shape=jax.ShapeDtypeStruct((bsz, s, DIFF_HEADS * LANES), BF16),
        scratch_shapes=_attn_scratch(tq),
        compiler_params=_params(("arbitrary", "arbitrary", "arbitrary")),
        name="diff_attention",
    )(lam, proj, proj, vt, gain)


def fox_attention(proj, vt, cum_col, cum_row, *, tq):
    bsz, s, _ = proj.shape
    npair = FOX_HEADS // 2
    kern = functools.partial(_attn_kernel, mode="fox", tq=tq, out_scale=None)
    return pl.pallas_call(
        kern,
        grid=(bsz, npair, s // tq),
        in_specs=[pl.BlockSpec((None, tq, LANES), lambda b, h, i: (b, i, QB + h)),
                  pl.BlockSpec((None, s, LANES), lambda b, h, i: (b, 0, KB + h)),
                  pl.BlockSpec((None, LANES, s), lambda b, h, i: (b, VT_B + h, 0)),
                  pl.BlockSpec((None, s, LANES), lambda b, h, i: (b, 0, 0)),
                  pl.BlockSpec((None, SUBLANES, tq), lambda b, h, i: (b, 0, i))],
        out_specs=pl.BlockSpec((None, tq, LANES), lambda b, h, i: (b, i, h)),
        out_shape=jax.ShapeDtypeStruct((bsz, s, npair * LANES), BF16),
        scratch_shapes=_attn_scratch(tq) + [pltpu.VMEM((2, s, LANES), F32)],
        compiler_params=_params(("arbitrary", "arbitrary", "arbitrary")),
        name="fox_attention",
    )(proj, proj, vt, cum_col, cum_row)


def _retention_tables(s, chunk):
    half = HEAD_DIM // 2
    inv = RET_THETA ** (-np.arange(half, dtype=np.float64) / half)
    ang = np.arange(s, dtype=np.float64)[:, None] * inv[None, :]
    cos = np.tile(np.cos(ang), (1, 4))
    sin = np.tile(np.concatenate([-np.sin(ang), np.sin(ang)], axis=1), (1, 2))
    log_g = np.log1p(-np.exp2(-5.0 - np.arange(RET_HEADS, dtype=np.float64)))
    j = np.arange(chunk, dtype=np.float64)
    diff = j[:, None] - j[None, :]
    inner = np.where(diff[None] >= 0, np.exp(np.where(diff[None] >= 0, diff[None], 0.0)
                                             * log_g[:, None, None]), 0.0)
    q_dec = np.exp((j + 1.0)[None, :] * log_g[:, None])
    k_dec = np.exp((chunk - 1.0 - j)[None, :] * log_g[:, None])
    c_dec = np.exp(chunk * log_g)
    npair = RET_HEADS // 2

    def lanes(t):
        t = t.reshape(npair, 2, chunk)
        return np.repeat(t.transpose(0, 2, 1), HEAD_DIM, axis=2)

    blk = np.kron(np.eye(2), np.ones((HEAD_DIM, HEAD_DIM)))
    c_mat = np.stack([np.kron(np.diag(c_dec[2 * p:2 * p + 2]), np.ones((HEAD_DIM, HEAD_DIM)))
                      for p in range(npair)])
    f = lambda a: jnp.asarray(a, F32)
    return dict(cos=f(cos), sin=f(sin), inner=f(inner.reshape(npair, 2, chunk, chunk)),
                q_dec=f(lanes(q_dec)), k_dec=f(lanes(k_dec)), c_mat=f(c_mat), blk=f(blk))


def _retention_kernel(q_ref, k_ref, v_ref, g_ref, cos_ref, sin_ref, inner_ref, qdec_ref,
                      kdec_ref, cmat_ref, blk_ref, gain_ref, o_ref, state_ref):
    @pl.when(pl.program_id(1) == 0)
    def _():
        state_ref[...] = jnp.zeros_like(state_ref)

    cos = cos_ref[...]
    sin = sin_ref[...]
    lane = lax.broadcasted_iota(jnp.int32, cos.shape, 1)
    first_half = (lane % HEAD_DIM) < (HEAD_DIM // 2)
    lo = lane < HEAD_DIM

    def rope(x):
        x = x.astype(F32)
        swapped = jnp.where(first_half, pltpu.roll(x, LANES - HEAD_DIM // 2, 1),
                            pltpu.roll(x, HEAD_DIM // 2, 1))
        return x * cos + swapped * sin

    for p in range(RET_HEADS // 2):
        cols = slice(p * LANES, (p + 1) * LANES)
        qr = rope(q_ref[:, cols])
        kr = rope(k_ref[:, cols])
        v = v_ref[:, cols]
        qb = qr.astype(BF16)
        kb = kr.astype(BF16)
        zero = jnp.zeros_like(qb)
        a1 = (_dot_nt(jnp.where(lo, qb, zero), kb) * inner_ref[p, 0]).astype(BF16)
        a2 = (_dot_nt(jnp.where(lo, zero, qb), kb) * inner_ref[p, 1]).astype(BF16)
        state = state_ref[p]
        o = jnp.where(lo, _dot(a1, v), _dot(a2, v))
        o = o + _dot((qr * qdec_ref[p]).astype(BF16), state.astype(BF16))
        kd = (kr * kdec_ref[p]).astype(BF16)
        state_ref[p] = state * cmat_ref[p] + _dot(kd.T, v) * blk_ref[...]

        inv_n = 1.0 / HEAD_DIM
        sum_lo = jnp.sum(jnp.where(lo, o, 0.0), axis=-1, keepdims=True)
        sum_hi = jnp.sum(jnp.where(lo, 0.0, o), axis=-1, keepdims=True)
        d = o - jnp.where(lo, sum_lo, sum_hi) * inv_n
        d2 = d * d
        var_lo = jnp.sum(jnp.where(lo, d2, 0.0), axis=-1, keepdims=True)
        var_hi = jnp.sum(jnp.where(lo, 0.0, d2), axis=-1, keepdims=True)
        y = d * lax.rsqrt(jnp.where(lo, var_lo, var_hi) * inv_n + EPS)
        g = g_ref[:, cols].astype(F32)
        o_ref[:, cols] = (g * _sigmoid(g) * (y * gain_ref[:, cols])).astype(o_ref.dtype)


def retention(proj, gain, *, chunk):
    bsz, s, _ = proj.shape
    npair = RET_HEADS // 2
    t = _retention_tables(s, chunk)

    width = npair * LANES

    def act(off):
        return pl.BlockSpec((None, chunk, width), lambda b, c: (b, c, off // npair))

    pos = pl.BlockSpec((chunk, LANES), lambda b, c: (c, 0))
    return pl.pallas_call(
        _retention_kernel,
        grid=(bsz, s // chunk),
        in_specs=[act(QC), act(KC), act(VC), act(GC), pos, pos,
                  _resident((npair, 2, chunk, chunk)), _resident((npair, chunk, LANES)),
                  _resident((npair, chunk, LANES)), _resident((npair, LANES, LANES)),
                  _resident((LANES, LANES)), _resident((1, width))],
        out_specs=pl.BlockSpec((None, chunk, width), lambda b, c: (b, c, 0)),
        out_shape=jax.ShapeDtypeStruct((bsz, s, width), BF16),
        scratch_shapes=[pltpu.VMEM((npair, LANES, LANES), F32)],
        compiler_params=_params(("arbitrary", "arbitrary")),
        name="retention",
    )(proj, proj, proj, proj, t["cos"], t["sin"], t["inner"], t["q_dec"], t["k_dec"],
      t["c_mat"], t["blk"], gain)


def _mix_residual(h_ref, oa_ref, ob_ref, oc_ref, wo_ref):
    wa = oa_ref.shape[1]
    wb = ob_ref.shape[1]
    return (h_ref[...] + _dot(oa_ref[...], wo_ref[0:wa, :])
            + _dot(ob_ref[...], wo_ref[wa:wa + wb, :])
            + _dot(oc_ref[...], wo_ref[wa + wb:, :]))


def _swiglu_accumulate(xn, wg_ref, wu_ref, wd_ref, acc_ref, chunk):
    def body(i, carry):
        c = pl.multiple_of(i * chunk, chunk)
        g = _dot(xn, wg_ref[:, pl.ds(c, chunk)])
        u = _dot(xn, wu_ref[:, pl.ds(c, chunk)])
        a = (g * _sigmoid(g) * u).astype(BF16)
        acc_ref[...] += _dot(a, wd_ref[pl.ds(c, chunk), :])
        return carry

    lax.fori_loop(0, wg_ref.shape[1] // chunk, body, 0)


def _ple(h, p_ref, pn_ref, pg_ref, pp_ref):
    gate = _sigmoid(_dot(_rms(h, pn_ref[...]).astype(BF16), pg_ref[...]))
    return h + gate * _dot(p_ref[...].astype(BF16), pp_ref[...])


def _dense_layer_kernel(h_ref, oa_ref, ob_ref, oc_ref, p_ref, wo_ref, gn_ref, wg_ref, wu_ref,
                        wd_ref, pn_ref, pg_ref, pp_ref, out_ref, acc_ref, *, ff_chunk):
    h1 = _mix_residual(h_ref, oa_ref, ob_ref, oc_ref, wo_ref)
    xn = _rms(h1, gn_ref[...]).astype(BF16)
    acc_ref[...] = h1
    _swiglu_accumulate(xn, wg_ref, wu_ref, wd_ref, acc_ref, ff_chunk)
    out_ref[...] = _ple(acc_ref[...], p_ref, pn_ref, pg_ref, pp_ref)


def dense_layer(h, oa, ob, oc, p, wo, gn, wg, wu, wd, pn, pg, pp, *, tm, ff_chunk):
    n, d = h.shape
    row = lambda w: pl.BlockSpec((tm, w), lambda i: (i, 0))
    weights = (wo, gn, wg, wu, wd, pn, pg, pp)
    return pl.pallas_call(
        functools.partial(_dense_layer_kernel, ff_chunk=ff_chunk),
        grid=(n // tm,),
        in_specs=[row(d), row(oa.shape[1]), row(ob.shape[1]), row(oc.shape[1]), row(p.shape[1])]
                 + [_resident(w.shape) for w in weights],
        out_specs=row(d),
        out_shape=jax.ShapeDtypeStruct((n, d), F32),
        scratch_shapes=[pltpu.VMEM((tm, d), F32)],
        compiler_params=_params(("arbitrary",)),
        name="dense_layer",
    )(h, oa, ob, oc, p, *weights)


def _store_slabs(ref, x):
    rows, d = x.shape
    nl = d // LANES
    for c in range(nl):
        ref[pl.ds(c, rows, stride=nl), :] = x[:, c * LANES:(c + 1) * LANES].astype(ref.dtype)


def _load_slabs(ref, nl, dtype=None):
    parts = [ref[pl.ds(c, ref.shape[0] // nl, stride=nl), :] for c in range(nl)]
    if dtype is not None:
        parts = [p.astype(dtype) for p in parts]
    return jnp.concatenate(parts, axis=1)


def _slab_at(ref, row, nl):
    return ref.at[pl.ds(pl.multiple_of(row * nl, nl), nl), :]


def _router_kernel(h_ref, oa_ref, ob_ref, oc_ref, wo_ref, gn_ref, wr_ref,
                   h1_ref, xn_ref, meta_ref, cnt_ref):
    @pl.when(pl.program_id(0) == 0)
    def _():
        cnt_ref[...] = jnp.zeros_like(cnt_ref)

    h1 = _mix_residual(h_ref, oa_ref, ob_ref, oc_ref, wo_ref)
    h1_ref[...] = h1
    xn = _rms(h1, gn_ref[...])
    _store_slabs(xn_ref, xn)

    x_hi, x_lo = _split_bf16(xn, 2)
    w_hi, w_lo = _split_bf16(wr_ref[...], 2)
    logits = _dot(x_hi, w_hi) + (_dot(x_hi, w_lo) + _dot(x_lo, w_hi))
    tm = logits.shape[0]
    lane = lax.broadcasted_iota(jnp.int32, logits.shape, 1)
    logits = jnp.where(lane < N_EXPERTS, logits, -jnp.inf)

    def top(vals):
        best = jnp.max(vals, axis=-1, keepdims=True)
        idx = jnp.min(jnp.where(vals == best, lane, LANES), axis=-1, keepdims=True)
        return best, idx

    v1, i1 = top(logits)
    v2, i2 = top(jnp.where(lane == i1, -jnp.inf, logits))
    e = jnp.exp(v2 - v1)
    g1 = 1.0 / (1.0 + e)
    g2 = e / (1.0 + e)

    oh1 = (lane == i1).astype(F32)
    oh2 = (lane == i2).astype(F32)
    both = oh1 + oh2
    r = lax.broadcasted_iota(jnp.int32, (tm, tm), 0)
    c = lax.broadcasted_iota(jnp.int32, (tm, tm), 1)
    before = _dot((c < r).astype(BF16), both.astype(BF16)) + cnt_ref[...]
    rank1 = jnp.sum(before * oh1, axis=-1, keepdims=True)
    rank2 = jnp.sum(before * oh2, axis=-1, keepdims=True)
    cnt_ref[...] += jnp.sum(both, axis=0, keepdims=True)

    meta = jnp.zeros(logits.shape, F32)
    for k, val in enumerate((i1.astype(F32), i2.astype(F32), rank1, rank2, g1, g2)):
        meta = jnp.where(lane == k, val, meta)
    meta_ref[...] = meta


def route_tokens(h, oa, ob, oc, wo, gn, wr, *, tm):
    n, d = h.shape
    row = lambda w: pl.BlockSpec((tm, w), lambda i: (i, 0))
    weights = (wo, gn, wr)
    return pl.pallas_call(
        _router_kernel,
        grid=(n // tm,),
        in_specs=[row(d), row(oa.shape[1]), row(ob.shape[1]), row(oc.shape[1])]
                 + [_resident(w.shape) for w in weights],
        out_specs=[row(d), pl.BlockSpec((tm * (d // LANES), LANES), lambda i: (i, 0)),
                   row(LANES), pl.BlockSpec((1, LANES), lambda i: (0, 0))],
        out_shape=[jax.ShapeDtypeStruct((n, d), F32),
                   jax.ShapeDtypeStruct((n * (d // LANES), LANES), F32),
                   jax.ShapeDtypeStruct((n, LANES), F32),
                   jax.ShapeDtypeStruct((1, LANES), F32)],
        compiler_params=_params(("arbitrary",)),
        name="router",
    )(h, oa, ob, oc, *weights)


def _dispatch_kernel(last_ref, pos_ref, x_ref, xs_ref, zero_scr, sem, zero_sem, *, tile, nl):
    tm = x_ref.shape[0] // nl
    zrows = zero_scr.shape[0] // nl

    @pl.when(pl.program_id(0) == 0)
    def _():
        zero_scr[...] = jnp.zeros_like(zero_scr)

        def zero_copy(e, q):
            start = pl.multiple_of((last_ref[e] + q * zrows) * nl, zrows * nl)
            return pltpu.make_async_copy(zero_scr, xs_ref.at[pl.ds(start, zrows * nl), :],
                                         zero_sem)

        for e in range(N_EXPERTS):
            for q in range(tile // zrows):
                zero_copy(e, q).start()
        for e in range(N_EXPERTS):
            for q in range(tile // zrows):
                zero_copy(e, q).wait()

        n_trailing = (xs_ref.shape[0] // nl - last_ref[N_EXPERTS]) // zrows

        def trailing(q, carry, wait):
            cp = zero_copy(N_EXPERTS, q)
            cp.wait() if wait else cp.start()
            return carry

        lax.fori_loop(0, n_trailing, functools.partial(trailing, wait=False), 0)
        lax.fori_loop(0, n_trailing, functools.partial(trailing, wait=True), 0)

    def row_copy(r, k):
        return pltpu.make_async_copy(_slab_at(x_ref, r, nl),
                                     _slab_at(xs_ref, pos_ref[0, k, r], nl), sem)

    _start_rows(tm, row_copy)
    for k in range(2):
        pltpu.make_async_copy(x_ref, xs_ref.at[pl.ds(0, tm * nl), :], sem).wait()


def _start_rows(n_rows, row_copy):
    def body(g, carry):
        base = pl.multiple_of(g * ROW_UNROLL, ROW_UNROLL)
        for j in range(ROW_UNROLL):
            row_copy(base + j, 0).start(priority=0)
            row_copy(base + j, 1).start(priority=1)
        return carry

    lax.fori_loop(0, n_rows // ROW_UNROLL, body, 0)


def dispatch(xn, pos, last_tile_start, n_slots, *, tm, tile, zrows, nl):
    n = xn.shape[0] // nl
    assert tile % zrows == 0
    grid_spec = pltpu.PrefetchScalarGridSpec(
        num_scalar_prefetch=1,
        grid=(n // tm,),
        in_specs=[pl.BlockSpec((1, 2, tm), lambda i, last: (i, 0, 0), memory_space=pltpu.SMEM),
                  pl.BlockSpec((tm * nl, LANES), lambda i, last: (i, 0))],
        out_specs=pl.BlockSpec(memory_space=pl.ANY),
        scratch_shapes=[pltpu.VMEM((zrows * nl, LANES), xn.dtype),
                        pltpu.SemaphoreType.DMA(()), pltpu.SemaphoreType.DMA(())],
    )
    return pl.pallas_call(
        functools.partial(_dispatch_kernel, tile=tile, nl=nl),
        grid_spec=grid_spec,
        out_shape=jax.ShapeDtypeStruct((n_slots * nl, LANES), xn.dtype),
        compiler_params=_params(("arbitrary",)),
        name="moe_dispatch",
    )(last_tile_start, pos, xn)


def _experts_kernel(te_ref, nt_ref, rows_ref, x_ref, wg_ref, wu_ref, wd_ref, o_ref, x_scr,
                    acc_ref, *, sub_chunk):
    del te_ref, nt_ref
    t = pl.program_id(0)
    c = pl.program_id(1)
    last = pl.num_programs(1) - 1
    rows = rows_ref[t]
    half = x_scr.shape[0] // 2

    @pl.when(c == 0)
    def _():
        acc_ref[...] = jnp.zeros_like(acc_ref)

    @pl.when(jnp.logical_and(rows > 0, c == 0))
    def _():
        x_scr[...] = _load_slabs(x_ref, x_scr.shape[1] // LANES, BF16)

    @pl.when(rows > half)
    def _():
        _swiglu_accumulate(x_scr[...], wg_ref, wu_ref, wd_ref, acc_ref, sub_chunk)

    @pl.when(jnp.logical_and(rows > 0, rows <= half))
    def _():
        _swiglu_accumulate(x_scr[:half, :], wg_ref, wu_ref, wd_ref,
                           acc_ref.at[pl.ds(0, half), :], sub_chunk)

    @pl.when(c == last)
    def _():
        _store_slabs(o_ref, acc_ref[...])


def experts(xs, tile_expert, n_tiles, tile_rows, wg, wu, wd, *, tile, ff_chunk, sub_chunk):
    d = wg.shape[1]
    nl = d // LANES
    n_slots = xs.shape[0] // nl
    d_ff = wg.shape[2]
    nc = d_ff // ff_chunk
    grid_tiles = n_slots // tile

    def tile_idx(t, c, te, nt, tr):
        return (jnp.maximum(jnp.minimum(t, nt[0] - 1), 0), 0)

    def chunk_idx(t, c, nt):
        return jnp.where(t < nt[0], c, nc - 1)

    grid_spec = pltpu.PrefetchScalarGridSpec(
        num_scalar_prefetch=3,
        grid=(grid_tiles, nc),
        in_specs=[pl.BlockSpec((tile * nl, LANES), tile_idx),
                  pl.BlockSpec((None, d, ff_chunk),
                               lambda t, c, te, nt, tr: (te[t], 0, chunk_idx(t, c, nt))),
                  pl.BlockSpec((None, d, ff_chunk),
                               lambda t, c, te, nt, tr: (te[t], 0, chunk_idx(t, c, nt))),
                  pl.BlockSpec((None, ff_chunk, d),
                               lambda t, c, te, nt, tr: (te[t], chunk_idx(t, c, nt), 0))],
        out_specs=pl.BlockSpec((tile * nl, LANES), lambda t, c, te, nt, tr: (t, 0)),
        scratch_shapes=[pltpu.VMEM((tile, d), BF16), pltpu.VMEM((tile, d), F32)],
    )
    return pl.pallas_call(
        functools.partial(_experts_kernel, sub_chunk=sub_chunk),
        grid_spec=grid_spec,
        out_shape=jax.ShapeDtypeStruct((n_slots * nl, LANES), F32),
        compiler_params=_params(("arbitrary", "arbitrary")),
        name="moe_experts",
    )(tile_expert, n_tiles, tile_rows, xs, wg, wu, wd)


def _combine_kernel(pos_ref, h_ref, meta_ref, p_ref, pn_ref, pg_ref, pp_ref, fn_ref, ys_ref,
                    out_ref, y1_scr, y2_scr, sem):
    tm = h_ref.shape[0]
    bufs = (y1_scr, y2_scr)

    nl = h_ref.shape[1] // LANES

    def row_copy(r, k):
        return pltpu.make_async_copy(_slab_at(ys_ref, pos_ref[0, k, r], nl),
                                     _slab_at(bufs[k], r, nl), sem)

    _start_rows(tm, row_copy)
    for buf in bufs:
        pltpu.make_async_copy(ys_ref.at[pl.ds(0, tm * nl), :], buf, sem).wait()

    meta = meta_ref[...]
    lane = lax.broadcasted_iota(jnp.int32, meta.shape, 1)
    g1 = jnp.sum(jnp.where(lane == 4, meta, 0.0), axis=-1, keepdims=True)
    g2 = jnp.sum(jnp.where(lane == 5, meta, 0.0), axis=-1, keepdims=True)
    h2 = h_ref[...] + g1 * _load_slabs(y1_scr, nl) + g2 * _load_slabs(y2_scr, nl)
    h3 = _ple(h2, p_ref, pn_ref, pg_ref, pp_ref)
    out_ref[...] = _rms(h3, fn_ref[...])


def combine(pos, h1, meta, p, pn, pg, pp, fn, ys, *, tm):
    n, d = h1.shape
    row = lambda w: pl.BlockSpec((tm, w), lambda i: (i, 0))
    weights = (pn, pg, pp, fn)
    return pl.pallas_call(
        _combine_kernel,
        grid=(n // tm,),
        in_specs=[pl.BlockSpec((1, 2, tm), lambda i: (i, 0, 0), memory_space=pltpu.SMEM),
                  row(d), row(LANES), row(p.shape[1])]
                 + [_resident(w.shape) for w in weights]
                 + [pl.BlockSpec(memory_space=pl.ANY)],
        out_specs=row(d),
        out_shape=jax.ShapeDtypeStruct((n, d), F32),
        scratch_shapes=[pltpu.VMEM((tm * (d // LANES), LANES), F32),
                        pltpu.VMEM((tm * (d // LANES), LANES), F32),
                        pltpu.SemaphoreType.DMA(())],
        compiler_params=_params(("arbitrary",)),
        name="moe_combine",
    )(pos, h1, meta, p, *weights, ys)


def _prep_w_in(w):
    scale = HEAD_DIM ** -0.5
    qa, ka, va = w[:, 0:512] * (scale * LOG2_E), w[:, 512:1024], w[:, 1024:1536]
    qb, kb, vb = w[:, 1536:1792] * (scale * LOG2_E), w[:, 1792:2048], w[:, 2048:2304]
    fb = w[:, 2304:2308]
    qc, kc, vc, gc = w[:, 2308:2564], w[:, 2564:2820] * scale, w[:, 2820:3076], w[:, 3076:3332]
    main = jnp.concatenate([qa, ka, qb, kb, qc, kc, vc, gc], axis=1).astype(BF16)
    f = jnp.pad(fb, ((0, 0), (0, LANES - fb.shape[1]))).astype(BF16)
    vt = jnp.concatenate([va, vb], axis=1).T.astype(BF16)
    return main, f, vt


def _pad_lanes(v):
    return jnp.pad(v, (0, LANES - v.shape[0]))[None, :]


class Tiles(NamedTuple):
    rows: int
    proj_cols: int
    attn: int
    scan: int
    ret_chunk: int
    ff_chunk: int
    moe_tile: int
    moe_ff_chunk: int
    moe_sub_chunk: int
    zero_rows: int


TILES = Tiles(rows=1024, proj_cols=512, attn=512, scan=512, ret_chunk=256, ff_chunk=256,
              moe_tile=1024, moe_ff_chunk=1792, moe_sub_chunk=256, zero_rows=256)


def _mixer(h, b, s, norm_g, w_in, b_forget, lq1, lk1, lq2, lk2, subln, ret_gn, layer_idx, t):
    w_main, w_f, w_vt = _prep_w_in(w_in)
    proj, f, vt = norm_proj(h, norm_g[None, :], w_main, w_f, w_vt, tm=t.rows, seq=s,
                            col_chunk=t.proj_cols)
    proj = proj.reshape(b, s, -1)
    cum_col, cum_row = fox_prep(f.reshape(b, s, LANES), _pad_lanes(b_forget), blk=t.scan)
    lam_init = 0.8 - 0.6 * math.exp(-0.3 * layer_idx)
    lam = jnp.exp(jnp.sum(lq1 * lk1)) - jnp.exp(jnp.sum(lq2 * lk2)) + lam_init
    oa = diff_attention(proj, vt, lam.reshape(1), subln[None, :], tq=t.attn,
                        out_scale=1.0 - lam_init)
    ob = fox_attention(proj, vt, cum_col, cum_row, tq=t.attn)
    oc = retention(proj, ret_gn[None, :], chunk=t.ret_chunk)
    n = b * s
    return oa.reshape(n, -1), ob.reshape(n, -1), oc.reshape(n, -1)


def _route_plan(meta, counts, tile, n_slots, tm):
    n = meta.shape[0]
    idx = meta[:, 0:2].astype(jnp.int32)
    rank = meta[:, 2:4].astype(jnp.int32)
    cnt = counts[0, :N_EXPERTS].astype(jnp.int32)
    padded = ((cnt + tile - 1) // tile) * tile
    ends = jnp.cumsum(padded)
    starts = ends - padded
    pos = starts[idx] + rank
    n_tiles = ends[-1] // tile
    tiles = jnp.arange(n_slots // tile, dtype=jnp.int32)
    tile_expert = jnp.sum(tiles[:, None] >= (ends // tile)[None, :], axis=1)
    last = jnp.sum(n_tiles - 1 >= ends // tile)
    tile_expert = jnp.where(tiles < n_tiles, tile_expert, last).astype(jnp.int32)
    tile_expert = jnp.clip(tile_expert, 0, N_EXPERTS - 1)
    pos = jnp.clip(pos, 0, n_slots - 1)
    tile_rows = jnp.clip(cnt[tile_expert] - (tiles * tile - starts[tile_expert]), 0, tile)
    tile_rows = jnp.where(tiles < n_tiles, tile_rows, 0).astype(jnp.int32)
    pos = pos.reshape(n // tm, tm, 2).transpose(0, 2, 1)
    last_tile_start = jnp.concatenate([jnp.clip(ends - tile, 0, n_slots - tile),
                                       jnp.clip(ends[-1:], 0, n_slots)]).astype(jnp.int32)
    return pos, tile_expert, n_tiles.reshape(1).astype(jnp.int32), tile_rows, last_tile_start


def forward(x, p, norm_mix, w_in, b_forget, lambda_q1, lambda_k1, lambda_q2, lambda_k2,
            diff_subln, ret_gn, w_out, norm_ffn, dense_w_gate, dense_w_up, dense_w_down,
            router, moe_w_gate, moe_w_up, moe_w_down, ple_norm, ple_gate, ple_proj, final_norm,
            t=TILES):
    assert w_in.shape[0] == 2, "two layers: dense SwiGLU, then routed experts"
    b, s, d = x.shape
    n = b * s
    h = x.reshape(n, d)
    pf = p.reshape(p.shape[0], n, p.shape[-1])
    bf = lambda a: a.astype(BF16)
    row = lambda a: a[None, :]

    oa, ob, oc = _mixer(h, b, s, norm_mix[0], w_in[0], b_forget[0], lambda_q1[0], lambda_k1[0],
                        lambda_q2[0], lambda_k2[0], diff_subln[0], ret_gn[0], 0, t)
    h = dense_layer(h, oa, ob, oc, pf[0], bf(w_out[0]), row(norm_ffn[0]), bf(dense_w_gate[0]),
                    bf(dense_w_up[0]), bf(dense_w_down[0]), row(ple_norm[0]), bf(ple_gate[0]),
                    bf(ple_proj[0]), tm=t.rows, ff_chunk=t.ff_chunk)

    oa, ob, oc = _mixer(h, b, s, norm_mix[1], w_in[1], b_forget[1], lambda_q1[1], lambda_k1[1],
                        lambda_q2[1], lambda_k2[1], diff_subln[1], ret_gn[1], 1, t)
    wr = jnp.pad(router[0], ((0, 0), (0, LANES - N_EXPERTS)))
    h1, xn, meta, counts = route_tokens(h, oa, ob, oc, bf(w_out[1]), row(norm_ffn[1]), wr,
                                         tm=t.rows)
    n_slots = 2 * n + N_EXPERTS * t.moe_tile
    pos, tile_expert, n_tiles, tile_rows, last_tile_start = _route_plan(
        meta, counts, t.moe_tile, n_slots, t.rows)
    xs = dispatch(xn, pos, last_tile_start, n_slots, tm=t.rows, tile=t.moe_tile,
                  zrows=t.zero_rows, nl=d // LANES)
    ys = experts(xs, tile_expert, n_tiles, tile_rows, bf(moe_w_gate[0]), bf(moe_w_up[0]),
                 bf(moe_w_down[0]), tile=t.moe_tile, ff_chunk=t.moe_ff_chunk,
                 sub_chunk=t.moe_sub_chunk)
    out = combine(pos, h1, meta, pf[1], row(ple_norm[1]), bf(ple_gate[1]), bf(ple_proj[1]),
                  row(final_norm), ys, tm=t.rows)
    return out.reshape(b, s, d)


def kernel(x, p, norm_mix, w_in, b_forget, lambda_q1, lambda_k1, lambda_q2, lambda_k2,
           diff_subln, ret_gn, w_out, norm_ffn, dense_w_gate, dense_w_up, dense_w_down,
           router, moe_w_gate, moe_w_up, moe_w_down, ple_norm, ple_gate, ple_proj, final_norm):
    return forward(x, p, norm_mix, w_in, b_forget, lambda_q1, lambda_k1, lambda_q2, lambda_k2,
                   diff_subln, ret_gn, w_out, norm_ffn, dense_w_gate, dense_w_up, dense_w_down,
                   router, moe_w_gate, moe_w_up, moe_w_down, ple_norm, ple_gate, ple_proj,
                   final_norm)
```

```python
import functools
import math
from typing import NamedTuple

import numpy as np
import jax
import jax.numpy as jnp
from jax import lax
from jax.experimental import pallas as pl
from jax.experimental.pallas import tpu as pltpu

F32 = jnp.float32
BF16 = jnp.bfloat16

LANES = 128
SUBLANES = 8
VMEM_LIMIT_BYTES = 56 * 1024 * 1024

ROW_UNROLL = 8

HEAD_DIM = 64
DIFF_HEADS = 4
FOX_HEADS = 4
RET_HEADS = 4
RET_THETA = 10000.0
N_EXPERTS = 8
EPS = 1e-6
NEG_INF = -1e30
LOG2_E = math.log2(math.e)

QA, KA = 0, 4
QB, KB = 8, 10
QC, KC, VC, GC = 12, 14, 16, 18
VT_A, VT_B = 0, 4


def _params(semantics):
    return pltpu.CompilerParams(dimension_semantics=semantics,
                                vmem_limit_bytes=VMEM_LIMIT_BYTES)


def _resident(shape):
    nd = len(shape)
    return pl.BlockSpec(shape, lambda *_: (0,) * nd, pipeline_mode=pl.Buffered(1))


def _rms(x, g):
    return x * lax.rsqrt(jnp.mean(x * x, axis=-1, keepdims=True) + EPS) * g


def _sigmoid(x):
    return 1.0 / (1.0 + jnp.exp(-x))


def _dot(a, b):
    return jnp.dot(a, b, preferred_element_type=F32)


def _dot_nt(a, b):
    return lax.dot_general(a, b, (((1,), (1,)), ((), ())), preferred_element_type=F32)


def _split_bf16(x, n):
    pieces = []
    for _ in range(n):
        piece = x.astype(BF16)
        pieces.append(piece)
        x = x - piece.astype(F32)
    return pieces


def _lane_masks(shape):
    lane = lax.broadcasted_iota(jnp.int32, shape, len(shape) - 1)
    return lane < HEAD_DIM


def _norm_proj_kernel(h_ref, g_ref, w_ref, wf_ref, wvt_ref, out_ref, f_ref, vt_ref, *,
                      col_chunk):
    xn = _rms(h_ref[...], g_ref[...]).astype(BF16)
    ncol = w_ref.shape[1]
    for c in range(0, ncol, col_chunk):
        w = min(col_chunk, ncol - c)
        out_ref[:, c:c + w] = _dot(xn, w_ref[:, c:c + w]).astype(out_ref.dtype)
    f_ref[...] = _dot(xn, wf_ref[...])
    vt_ref[...] = _dot_nt(wvt_ref[...], xn).astype(vt_ref.dtype)


def norm_proj(h, g, w_main, w_f, w_vt, *, tm, seq, col_chunk):
    n, d = h.shape
    ncol = w_main.shape[1]
    nvt = w_vt.shape[0]
    per_seq = seq // tm
    return pl.pallas_call(
        functools.partial(_norm_proj_kernel, col_chunk=col_chunk),
        grid=(n // tm,),
        in_specs=[pl.BlockSpec((tm, d), lambda i: (i, 0)),
                  _resident((1, d)), _resident((d, ncol)), _resident((d, LANES)),
                  _resident((nvt, d))],
        out_specs=[pl.BlockSpec((tm, ncol), lambda i: (i, 0)),
                   pl.BlockSpec((tm, LANES), lambda i: (i, 0)),
                   pl.BlockSpec((None, nvt, tm), lambda i: (i // per_seq, 0, i % per_seq))],
        out_shape=[jax.ShapeDtypeStruct((n, ncol), BF16),
                   jax.ShapeDtypeStruct((n, LANES), F32),
                   jax.ShapeDtypeStruct((n // seq, nvt, seq), BF16)],
        compiler_params=_params(("arbitrary",)),
        name="norm_proj",
    )(h, g, w_main, w_f, w_vt)


def _fox_prep_kernel(f_ref, b_ref, col_ref, row_ref, carry_ref):
    @pl.when(pl.program_id(1) == 0)
    def _():
        carry_ref[...] = jnp.zeros_like(carry_ref)

    x = f_ref[...] + b_ref[...]
    log_f = (jnp.minimum(x, 0.0) - jnp.log1p(jnp.exp(-jnp.abs(x)))) * LOG2_E
    blk = x.shape[0]
    r = lax.broadcasted_iota(jnp.int32, (blk, blk), 0)
    c = lax.broadcasted_iota(jnp.int32, (blk, blk), 1)
    tri = (c <= r).astype(BF16)
    cum = sum(_dot(tri, piece) for piece in _split_bf16(log_f, 3)) + carry_ref[...]
    col_ref[...] = cum
    row_ref[...] = cum.T[:SUBLANES, :]
    carry_ref[...] = cum[blk - 1:blk, :]


def fox_prep(f, b_pad, *, blk):
    bsz, s, _ = f.shape
    return pl.pallas_call(
        _fox_prep_kernel,
        grid=(bsz, s // blk),
        in_specs=[pl.BlockSpec((None, blk, LANES), lambda b, j: (b, j, 0)),
                  pl.BlockSpec((1, LANES), lambda b, j: (0, 0))],
        out_specs=[pl.BlockSpec((None, blk, LANES), lambda b, j: (b, j, 0)),
                   pl.BlockSpec((None, SUBLANES, blk), lambda b, j: (b, 0, j))],
        out_shape=[jax.ShapeDtypeStruct((bsz, s, LANES), F32),
                   jax.ShapeDtypeStruct((bsz, SUBLANES, s), F32)],
        scratch_shapes=[pltpu.VMEM((1, LANES), F32)],
        compiler_params=_params(("arbitrary", "arbitrary")),
        name="fox_prep",
    )(f, b_pad)


def _attn_kernel(*refs, mode, tq, nblk, out_scale):
    nmap = 2 * nblk
    if mode == "diff":
        lam_ref, q_ref, k_ref, vt_ref, gain_ref, o_ref = refs[:6]
    else:
        q_ref, k_ref, vt_ref, ccol_ref, crow_ref, o_ref = refs[:6]
        cs_scr = refs[6 + 5 * nmap]
    state = tuple(refs[6 + 3 * j:9 + 3 * j] for j in range(nmap))
    st_a = refs[6 + 3 * nmap:6 + 4 * nmap]
    st_b = refs[6 + 4 * nmap:6 + 5 * nmap]
    qi = pl.program_id(2)
    first_head = nmap * pl.program_id(1)
    block = lambda u: slice(u * LANES, (u + 1) * LANES)

    q_maps = []
    for u in range(nblk):
        q = q_ref[:, block(u)]
        lo = _lane_masks(q.shape)
        zero = jnp.zeros_like(q)
        q_maps += [jnp.where(lo, q, zero), jnp.where(lo, zero, q)]

    for m_ref, l_ref, acc_ref in state:
        m_ref[...] = jnp.full_like(m_ref, NEG_INF)
        l_ref[...] = jnp.zeros_like(l_ref)
        acc_ref[...] = jnp.zeros_like(acc_ref)

    if mode == "fox":
        @pl.when(qi == 0)
        def _():
            ccol = ccol_ref[...]
            lane = lax.broadcasted_iota(jnp.int32, ccol.shape, 1)
            for j in range(nmap):
                col = jnp.sum(jnp.where(lane == first_head + j, ccol, 0.0), axis=-1,
                              keepdims=True)
                cs_scr[j] = jnp.broadcast_to(col, ccol.shape)

        ct = tuple(crow_ref[pl.ds(first_head + j, 1), :] for j in range(nmap))

    def scores(c, bufs):
        rows = pl.ds(pl.multiple_of(c * tq, tq), tq)
        for j in range(nmap):
            bufs[j][...] = _dot_nt(k_ref[rows, block(j // 2)], q_maps[j])

    def consume(c, bufs, masked):
        start = pl.multiple_of(c * tq, tq)
        for i in range(nmap):
            vt = vt_ref[block(i // 2), pl.ds(start, tq)]
            m_ref, l_ref, acc_ref = state[i]
            st = bufs[i][...]
            if mode == "fox":
                cs = cs_scr[i, pl.ds(start, tq), :]
                st = st - jnp.tile(cs, (1, tq // LANES))
            if masked:
                key = lax.broadcasted_iota(jnp.int32, st.shape, 0)
                qry = lax.broadcasted_iota(jnp.int32, st.shape, 1)
                st = jnp.where(key <= qry, st, NEG_INF)
            m_old = m_ref[...]
            m_chunk = jnp.max(st, axis=0, keepdims=True)
            if mode == "fox":
                m_new = jnp.maximum(m_old, m_chunk + ct[i])
                p = jnp.exp2(st + (ct[i] - m_new))
            else:
                m_new = jnp.maximum(m_old, m_chunk)
                p = jnp.exp2(st - m_new)
            alpha = jnp.exp2(m_old - m_new)
            l_ref[...] = alpha * l_ref[...] + jnp.sum(p, axis=0, keepdims=True)
            acc_ref[...] = alpha * acc_ref[...] + _dot(vt, p.astype(BF16))
            m_ref[...] = m_new

    def body(j, carry):
        scores(2 * j + 1, st_b)
        consume(2 * j, st_a, False)
        scores(2 * j + 2, st_a)
        consume(2 * j + 1, st_b, False)
        return carry

    scores(0, st_a)
    lax.fori_loop(0, qi // 2, body, 0)

    @pl.when(qi % 2 == 0)
    def _():
        consume(qi, st_a, True)

    @pl.when(qi % 2 == 1)
    def _():
        scores(qi, st_b)
        consume(qi - 1, st_a, False)
        consume(qi, st_b, True)

    for u in range(nblk):
        o1, o2 = (acc_ref[...] * (1.0 / l_ref[...])
                  for _, l_ref, acc_ref in state[2 * u:2 * u + 2])
        if mode == "diff":
            ot = o1 - lam_ref[0] * o2
            ot = ot * lax.rsqrt(jnp.mean(ot * ot, axis=0, keepdims=True) + EPS)
            o = ot.T * (gain_ref[...] * out_scale)
        else:
            chan = lax.broadcasted_iota(jnp.int32, o1.shape, 0)
            o = jnp.where(chan < HEAD_DIM, o1, o2).T
        o_ref[:, block(u)] = o.astype(o_ref.dtype)


def _attn_scratch(tq, nblk):
    nmap = 2 * nblk
    per_map = [pltpu.VMEM((1, tq), F32), pltpu.VMEM((1, tq), F32), pltpu.VMEM((LANES, tq), F32)]
    scores = [pltpu.VMEM((tq, tq), F32)] * (2 * nmap)
    return per_map * nmap + scores


def diff_attention(proj, vt, lam, gain, *, tq, nblk, out_scale):
    bsz, s, _ = proj.shape
    w = nblk * LANES
    kern = functools.partial(_attn_kernel, mode="diff", tq=tq, nblk=nblk, out_scale=out_scale)
    return pl.pallas_call(
        kern,
        grid=(bsz, DIFF_HEADS // nblk, s // tq),
        in_specs=[pl.BlockSpec(memory_space=pltpu.SMEM),
                  pl.BlockSpec((None, tq, w), lambda b, h, i: (b, i, QA // nblk + h)),
                  pl.BlockSpec((None, s, w), lambda b, h, i: (b, 0, KA // nblk + h)),
                  pl.BlockSpec((None, w, s), lambda b, h, i: (b, VT_A // nblk + h, 0)),
                  pl.BlockSpec((1, LANES), lambda b, h, i: (0, 0))],
        out_specs=pl.BlockSpec((None, tq, w), lambda b, h, i: (b, i, h)),
        out_shape=jax.ShapeDtypeStruct((bsz, s, DIFF_HEADS * LANES), BF16),
        scratch_shapes=_attn_scratch(tq, nblk),
        compiler_params=_params(("arbitrary", "arbitrary", "arbitrary")),
        name="diff_attention",
    )(lam, proj, proj, vt, gain)


def fox_attention(proj, vt, cum_col, cum_row, *, tq, nblk):
    bsz, s, _ = proj.shape
    npair = FOX_HEADS // 2
    w = nblk * LANES
    kern = functools.partial(_attn_kernel, mode="fox", tq=tq, nblk=nblk, out_scale=None)
    return pl.pallas_call(
        kern,
        grid=(bsz, npair // nblk, s // tq),
        in_specs=[pl.BlockSpec((None, tq, w), lambda b, h, i: (b, i, QB // nblk + h)),
                  pl.BlockSpec((None, s, w), lambda b, h, i: (b, 0, KB // nblk + h)),
                  pl.BlockSpec((None, w, s), lambda b, h, i: (b, VT_B // nblk + h, 0)),
                  pl.BlockSpec((None, s, LANES), lambda b, h, i: (b, 0, 0)),
                  pl.BlockSpec((None, SUBLANES, tq), lambda b, h, i: (b, 0, i))],
        out_specs=pl.BlockSpec((None, tq, w), lambda b, h, i: (b, i, h)),
        out_shape=jax.ShapeDtypeStruct((bsz, s, npair * LANES), BF16),
        scratch_shapes=_attn_scratch(tq, nblk) + [pltpu.VMEM((2 * nblk, s, LANES), F32)],
        compiler_params=_params(("arbitrary", "arbitrary", "arbitrary")),
        name="fox_attention",
    )(proj, proj, vt, cum_col, cum_row)


def _retention_tables(s, chunk):
    half = HEAD_DIM // 2
    inv = RET_THETA ** (-np.arange(half, dtype=np.float64) / half)
    ang = np.arange(s, dtype=np.float64)[:, None] * inv[None, :]
    cos = np.tile(np.cos(ang), (1, 4))
    sin = np.tile(np.concatenate([-np.sin(ang), np.sin(ang)], axis=1), (1, 2))
    log_g = np.log1p(-np.exp2(-5.0 - np.arange(RET_HEADS, dtype=np.float64)))
    j = np.arange(chunk, dtype=np.float64)
    diff = j[:, None] - j[None, :]
    inner = np.where(diff[None] >= 0, np.exp(np.where(diff[None] >= 0, diff[None], 0.0)
                                             * log_g[:, None, None]), 0.0)
    q_dec = np.exp((j + 1.0)[None, :] * log_g[:, None])
    k_dec = np.exp((chunk - 1.0 - j)[None, :] * log_g[:, None])
    c_dec = np.exp(chunk * log_g)
    npair = RET_HEADS // 2

    def lanes(t):
        t = t.reshape(npair, 2, chunk)
        return np.repeat(t.transpose(0, 2, 1), HEAD_DIM, axis=2)

    blk = np.kron(np.eye(2), np.ones((HEAD_DIM, HEAD_DIM)))
    c_mat = np.stack([np.kron(np.diag(c_dec[2 * p:2 * p + 2]), np.ones((HEAD_DIM, HEAD_DIM)))
                      for p in range(npair)])
    f = lambda a: jnp.asarray(a, F32)
    return dict(cos=f(cos), sin=f(sin), inner=f(inner.reshape(npair, 2, chunk, chunk)),
                q_dec=f(lanes(q_dec)), k_dec=f(lanes(k_dec)), c_mat=f(c_mat), blk=f(blk))


def _retention_kernel(q_ref, k_ref, v_ref, g_ref, cos_ref, sin_ref, inner_ref, qdec_ref,
                      kdec_ref, cmat_ref, blk_ref, gain_ref, o_ref, state_ref):
    @pl.when(pl.program_id(1) == 0)
    def _():
        state_ref[...] = jnp.zeros_like(state_ref)

    cos = cos_ref[...]
    sin = sin_ref[...]
    lane = lax.broadcasted_iota(jnp.int32, cos.shape, 1)
    first_half = (lane % HEAD_DIM) < (HEAD_DIM // 2)
    lo = lane < HEAD_DIM

    def rope(x):
        x = x.astype(F32)
        swapped = jnp.where(first_half, pltpu.roll(x, LANES - HEAD_DIM // 2, 1),
                            pltpu.roll(x, HEAD_DIM // 2, 1))
        return x * cos + swapped * sin

    for p in range(RET_HEADS // 2):
        cols = slice(p * LANES, (p + 1) * LANES)
        qr = rope(q_ref[:, cols])
        kr = rope(k_ref[:, cols])
        v = v_ref[:, cols]
        qb = qr.astype(BF16)
        kb = kr.astype(BF16)
        zero = jnp.zeros_like(qb)
        a1 = (_dot_nt(jnp.where(lo, qb, zero), kb) * inner_ref[p, 0]).astype(BF16)
        a2 = (_dot_nt(jnp.where(lo, zero, qb), kb) * inner_ref[p, 1]).astype(BF16)
        state = state_ref[p]
        o = jnp.where(lo, _dot(a1, v), _dot(a2, v))
        o = o + _dot((qr * qdec_ref[p]).astype(BF16), state.astype(BF16))
        kd = (kr * kdec_ref[p]).astype(BF16)
        state_ref[p] = state * cmat_ref[p] + _dot(kd.T, v) * blk_ref[...]

        inv_n = 1.0 / HEAD_DIM
        sum_lo = jnp.sum(jnp.where(lo, o, 0.0), axis=-1, keepdims=True)
        sum_hi = jnp.sum(jnp.where(lo, 0.0, o), axis=-1, keepdims=True)
        d = o - jnp.where(lo, sum_lo, sum_hi) * inv_n
        d2 = d * d
        var_lo = jnp.sum(jnp.where(lo, d2, 0.0), axis=-1, keepdims=True)
        var_hi = jnp.sum(jnp.where(lo, 0.0, d2), axis=-1, keepdims=True)
        y = d * lax.rsqrt(jnp.where(lo, var_lo, var_hi) * inv_n + EPS)
        g = g_ref[:, cols].astype(F32)
        o_ref[:, cols] = (g * _sigmoid(g) * (y * gain_ref[:, cols])).astype(o_ref.dtype)


def retention(proj, gain, *, chunk):
    bsz, s, _ = proj.shape
    npair = RET_HEADS // 2
    t = _retention_tables(s, chunk)

    width = npair * LANES

    def act(off):
        return pl.BlockSpec((None, chunk, width), lambda b, c: (b, c, off // npair))

    pos = pl.BlockSpec((chunk, LANES), lambda b, c: (c, 0))
    return pl.pallas_call(
        _retention_kernel,
        grid=(bsz, s // chunk),
        in_specs=[act(QC), act(KC), act(VC), act(GC), pos, pos,
                  _resident((npair, 2, chunk, chunk)), _resident((npair, chunk, LANES)),
                  _resident((npair, chunk, LANES)), _resident((npair, LANES, LANES)),
                  _resident((LANES, LANES)), _resident((1, width))],
        out_specs=pl.BlockSpec((None, chunk, width), lambda b, c: (b, c, 0)),
        out_shape=jax.ShapeDtypeStruct((bsz, s, width), BF16),
        scratch_shapes=[pltpu.VMEM((npair, LANES, LANES), F32)],
        compiler_params=_params(("arbitrary", "arbitrary")),
        name="retention",
    )(proj, proj, proj, proj, t["cos"], t["sin"], t["inner"], t["q_dec"], t["k_dec"],
      t["c_mat"], t["blk"], gain)


def _mix_residual(h_ref, oa_ref, ob_ref, oc_ref, wo_ref):
    wa = oa_ref.shape[1]
    wb = ob_ref.shape[1]
    return (h_ref[...] + _dot(oa_ref[...], wo_ref[0:wa, :])
            + _dot(ob_ref[...], wo_ref[wa:wa + wb, :])
            + _dot(oc_ref[...], wo_ref[wa + wb:, :]))


def _swiglu_accumulate(xn, wg_ref, wu_ref, wd_ref, acc_ref, chunk):
    def body(i, carry):
        c = pl.multiple_of(i * chunk, chunk)
        g = _dot(xn, wg_ref[:, pl.ds(c, chunk)])
        u = _dot(xn, wu_ref[:, pl.ds(c, chunk)])
        a = (g * _sigmoid(g) * u).astype(BF16)
        acc_ref[...] += _dot(a, wd_ref[pl.ds(c, chunk), :])
        return carry

    lax.fori_loop(0, wg_ref.shape[1] // chunk, body, 0)


def _ple(h, p_ref, pn_ref, pg_ref, pp_ref):
    gate = _sigmoid(_dot(_rms(h, pn_ref[...]).astype(BF16), pg_ref[...]))
    return h + gate * _dot(p_ref[...].astype(BF16), pp_ref[...])


def _dense_layer_kernel(h_ref, oa_ref, ob_ref, oc_ref, p_ref, wo_ref, gn_ref, wg_ref, wu_ref,
                        wd_ref, pn_ref, pg_ref, pp_ref, out_ref, acc_ref, *, ff_chunk):
    h1 = _mix_residual(h_ref, oa_ref, ob_ref, oc_ref, wo_ref)
    xn = _rms(h1, gn_ref[...]).astype(BF16)
    acc_ref[...] = h1
    _swiglu_accumulate(xn, wg_ref, wu_ref, wd_ref, acc_ref, ff_chunk)
    out_ref[...] = _ple(acc_ref[...], p_ref, pn_ref, pg_ref, pp_ref)


def dense_layer(h, oa, ob, oc, p, wo, gn, wg, wu, wd, pn, pg, pp, *, tm, ff_chunk):
    n, d = h.shape
    row = lambda w: pl.BlockSpec((tm, w), lambda i: (i, 0))
    weights = (wo, gn, wg, wu, wd, pn, pg, pp)
    return pl.pallas_call(
        functools.partial(_dense_layer_kernel, ff_chunk=ff_chunk),
        grid=(n // tm,),
        in_specs=[row(d), row(oa.shape[1]), row(ob.shape[1]), row(oc.shape[1]), row(p.shape[1])]
                 + [_resident(w.shape) for w in weights],
        out_specs=row(d),
        out_shape=jax.ShapeDtypeStruct((n, d), F32),
        scratch_shapes=[pltpu.VMEM((tm, d), F32)],
        compiler_params=_params(("arbitrary",)),
        name="dense_layer",
    )(h, oa, ob, oc, p, *weights)


def _store_slabs(ref, x):
    rows, d = x.shape
    nl = d // LANES
    for c in range(nl):
        ref[pl.ds(c, rows, stride=nl), :] = x[:, c * LANES:(c + 1) * LANES].astype(ref.dtype)


def _load_slabs(ref, nl, dtype=None):
    parts = [ref[pl.ds(c, ref.shape[0] // nl, stride=nl), :] for c in range(nl)]
    if dtype is not None:
        parts = [p.astype(dtype) for p in parts]
    return jnp.concatenate(parts, axis=1)


def _slab_at(ref, row, nl):
    return ref.at[pl.ds(pl.multiple_of(row * nl, nl), nl), :]


def _router_kernel(h_ref, oa_ref, ob_ref, oc_ref, wo_ref, gn_ref, wr_ref,
                   h1_ref, xn_ref, meta_ref, cnt_ref):
    @pl.when(pl.program_id(0) == 0)
    def _():
        cnt_ref[...] = jnp.zeros_like(cnt_ref)

    h1 = _mix_residual(h_ref, oa_ref, ob_ref, oc_ref, wo_ref)
    h1_ref[...] = h1
    xn = _rms(h1, gn_ref[...])
    _store_slabs(xn_ref, xn)

    x_hi, x_lo = _split_bf16(xn, 2)
    w_hi, w_lo = _split_bf16(wr_ref[...], 2)
    logits = _dot(x_hi, w_hi) + (_dot(x_hi, w_lo) + _dot(x_lo, w_hi))
    tm = logits.shape[0]
    lane = lax.broadcasted_iota(jnp.int32, logits.shape, 1)
    logits = jnp.where(lane < N_EXPERTS, logits, -jnp.inf)

    def top(vals):
        best = jnp.max(vals, axis=-1, keepdims=True)
        idx = jnp.min(jnp.where(vals == best, lane, LANES), axis=-1, keepdims=True)
        return best, idx

    v1, i1 = top(logits)
    v2, i2 = top(jnp.where(lane == i1, -jnp.inf, logits))
    e = jnp.exp(v2 - v1)
    g1 = 1.0 / (1.0 + e)
    g2 = e / (1.0 + e)

    oh1 = (lane == i1).astype(F32)
    oh2 = (lane == i2).astype(F32)
    both = oh1 + oh2
    r = lax.broadcasted_iota(jnp.int32, (tm, tm), 0)
    c = lax.broadcasted_iota(jnp.int32, (tm, tm), 1)
    before = _dot((c < r).astype(BF16), both.astype(BF16)) + cnt_ref[...]
    rank1 = jnp.sum(before * oh1, axis=-1, keepdims=True)
    rank2 = jnp.sum(before * oh2, axis=-1, keepdims=True)
    cnt_ref[...] += jnp.sum(both, axis=0, keepdims=True)

    meta = jnp.zeros(logits.shape, F32)
    for k, val in enumerate((i1.astype(F32), i2.astype(F32), rank1, rank2, g1, g2)):
        meta = jnp.where(lane == k, val, meta)
    meta_ref[...] = meta


def route_tokens(h, oa, ob, oc, wo, gn, wr, *, tm):
    n, d = h.shape
    row = lambda w: pl.BlockSpec((tm, w), lambda i: (i, 0))
    weights = (wo, gn, wr)
    return pl.pallas_call(
        _router_kernel,
        grid=(n // tm,),
        in_specs=[row(d), row(oa.shape[1]), row(ob.shape[1]), row(oc.shape[1])]
                 + [_resident(w.shape) for w in weights],
        out_specs=[row(d), pl.BlockSpec((tm * (d // LANES), LANES), lambda i: (i, 0)),
                   row(LANES), pl.BlockSpec((1, LANES), lambda i: (0, 0))],
        out_shape=[jax.ShapeDtypeStruct((n, d), F32),
                   jax.ShapeDtypeStruct((n * (d // LANES), LANES), F32),
                   jax.ShapeDtypeStruct((n, LANES), F32),
                   jax.ShapeDtypeStruct((1, LANES), F32)],
        compiler_params=_params(("arbitrary",)),
        name="router",
    )(h, oa, ob, oc, *weights)


def _dispatch_kernel(last_ref, pos_ref, x_ref, xs_ref, zero_scr, sem, zero_sem, *, tile, nl):
    tm = x_ref.shape[0] // nl
    zrows = zero_scr.shape[0] // nl

    @pl.when(pl.program_id(0) == 0)
    def _():
        zero_scr[...] = jnp.zeros_like(zero_scr)

        def zero_copy(e, q):
            start = pl.multiple_of((last_ref[e] + q * zrows) * nl, zrows * nl)
            return pltpu.make_async_copy(zero_scr, xs_ref.at[pl.ds(start, zrows * nl), :],
                                         zero_sem)

        for e in range(N_EXPERTS):
            for q in range(tile // zrows):
                zero_copy(e, q).start()
        for e in range(N_EXPERTS):
            for q in range(tile // zrows):
                zero_copy(e, q).wait()

        n_trailing = (xs_ref.shape[0] // nl - last_ref[N_EXPERTS]) // zrows

        def trailing(q, carry, wait):
            cp = zero_copy(N_EXPERTS, q)
            cp.wait() if wait else cp.start()
            return carry

        lax.fori_loop(0, n_trailing, functools.partial(trailing, wait=False), 0)
        lax.fori_loop(0, n_trailing, functools.partial(trailing, wait=True), 0)

    def row_copy(r, k):
        return pltpu.make_async_copy(_slab_at(x_ref, r, nl),
                                     _slab_at(xs_ref, pos_ref[0, k, r], nl), sem)

    _start_rows(tm, row_copy)
    for k in range(2):
        pltpu.make_async_copy(x_ref, xs_ref.at[pl.ds(0, tm * nl), :], sem).wait()


def _start_rows(n_rows, row_copy):
    def body(g, carry):
        base = pl.multiple_of(g * ROW_UNROLL, ROW_UNROLL)
        for j in range(ROW_UNROLL):
            row_copy(base + j, 0).start(priority=0)
            row_copy(base + j, 1).start(priority=1)
        return carry

    lax.fori_loop(0, n_rows // ROW_UNROLL, body, 0)


def dispatch(xn, pos, last_tile_start, n_slots, *, tm, tile, zrows, nl):
    n = xn.shape[0] // nl
    assert tile % zrows == 0
    grid_spec = pltpu.PrefetchScalarGridSpec(
        num_scalar_prefetch=1,
        grid=(n // tm,),
        in_specs=[pl.BlockSpec((1, 2, tm), lambda i, last: (i, 0, 0), memory_space=pltpu.SMEM),
                  pl.BlockSpec((tm * nl, LANES), lambda i, last: (i, 0))],
        out_specs=pl.BlockSpec(memory_space=pl.ANY),
        scratch_shapes=[pltpu.VMEM((zrows * nl, LANES), xn.dtype),
                        pltpu.SemaphoreType.DMA(()), pltpu.SemaphoreType.DMA(())],
    )
    return pl.pallas_call(
        functools.partial(_dispatch_kernel, tile=tile, nl=nl),
        grid_spec=grid_spec,
        out_shape=jax.ShapeDtypeStruct((n_slots * nl, LANES), xn.dtype),
        compiler_params=_params(("arbitrary",)),
        name="moe_dispatch",
    )(last_tile_start, pos, xn)


def _experts_kernel(te_ref, nt_ref, rows_ref, x_ref, wg_ref, wu_ref, wd_ref, o_ref, x_scr,
                    acc_ref, *, sub_chunk):
    del te_ref, nt_ref
    t = pl.program_id(0)
    c = pl.program_id(1)
    last = pl.num_programs(1) - 1
    rows = rows_ref[t]
    half = x_scr.shape[0] // 2

    @pl.when(c == 0)
    def _():
        acc_ref[...] = jnp.zeros_like(acc_ref)

    @pl.when(jnp.logical_and(rows > 0, c == 0))
    def _():
        x_scr[...] = _load_slabs(x_ref, x_scr.shape[1] // LANES, BF16)

    @pl.when(rows > half)
    def _():
        _swiglu_accumulate(x_scr[...], wg_ref, wu_ref, wd_ref, acc_ref, sub_chunk)

    @pl.when(jnp.logical_and(rows > 0, rows <= half))
    def _():
        _swiglu_accumulate(x_scr[:half, :], wg_ref, wu_ref, wd_ref,
                           acc_ref.at[pl.ds(0, half), :], sub_chunk)

    @pl.when(c == last)
    def _():
        _store_slabs(o_ref, acc_ref[...])


def experts(xs, tile_expert, n_tiles, tile_rows, wg, wu, wd, *, tile, ff_chunk, sub_chunk):
    d = wg.shape[1]
    nl = d // LANES
    n_slots = xs.shape[0] // nl
    d_ff = wg.shape[2]
    nc = d_ff // ff_chunk
    grid_tiles = n_slots // tile

    def tile_idx(t, c, te, nt, tr):
        return (jnp.maximum(jnp.minimum(t, nt[0] - 1), 0), 0)

    def chunk_idx(t, c, nt):
        return jnp.where(t < nt[0], c, nc - 1)

    grid_spec = pltpu.PrefetchScalarGridSpec(
        num_scalar_prefetch=3,
        grid=(grid_tiles, nc),
        in_specs=[pl.BlockSpec((tile * nl, LANES), tile_idx),
                  pl.BlockSpec((None, d, ff_chunk),
                               lambda t, c, te, nt, tr: (te[t], 0, chunk_idx(t, c, nt))),
                  pl.BlockSpec((None, d, ff_chunk),
                               lambda t, c, te, nt, tr: (te[t], 0, chunk_idx(t, c, nt))),
                  pl.BlockSpec((None, ff_chunk, d),
                               lambda t, c, te, nt, tr: (te[t], chunk_idx(t, c, nt), 0))],
        out_specs=pl.BlockSpec((tile * nl, LANES), lambda t, c, te, nt, tr: (t, 0)),
        scratch_shapes=[pltpu.VMEM((tile, d), BF16), pltpu.VMEM((tile, d), F32)],
    )
    return pl.pallas_call(
        functools.partial(_experts_kernel, sub_chunk=sub_chunk),
        grid_spec=grid_spec,
        out_shape=jax.ShapeDtypeStruct((n_slots * nl, LANES), F32),
        compiler_params=_params(("arbitrary", "arbitrary")),
        name="moe_experts",
    )(tile_expert, n_tiles, tile_rows, xs, wg, wu, wd)


def _combine_kernel(pos_ref, h_ref, meta_ref, p_ref, pn_ref, pg_ref, pp_ref, fn_ref, ys_ref,
                    out_ref, y1_scr, y2_scr, sem):
    tm = h_ref.shape[0]
    bufs = (y1_scr, y2_scr)

    nl = h_ref.shape[1] // LANES

    def row_copy(r, k):
        return pltpu.make_async_copy(_slab_at(ys_ref, pos_ref[0, k, r], nl),
                                     _slab_at(bufs[k], r, nl), sem)

    _start_rows(tm, row_copy)
    for buf in bufs:
        pltpu.make_async_copy(ys_ref.at[pl.ds(0, tm * nl), :], buf, sem).wait()

    meta = meta_ref[...]
    lane = lax.broadcasted_iota(jnp.int32, meta.shape, 1)
    g1 = jnp.sum(jnp.where(lane == 4, meta, 0.0), axis=-1, keepdims=True)
    g2 = jnp.sum(jnp.where(lane == 5, meta, 0.0), axis=-1, keepdims=True)
    h2 = h_ref[...] + g1 * _load_slabs(y1_scr, nl) + g2 * _load_slabs(y2_scr, nl)
    h3 = _ple(h2, p_ref, pn_ref, pg_ref, pp_ref)
    out_ref[...] = _rms(h3, fn_ref[...])


def combine(pos, h1, meta, p, pn, pg, pp, fn, ys, *, tm):
    n, d = h1.shape
    row = lambda w: pl.BlockSpec((tm, w), lambda i: (i, 0))
    weights = (pn, pg, pp, fn)
    return pl.pallas_call(
        _combine_kernel,
        grid=(n // tm,),
        in_specs=[pl.BlockSpec((1, 2, tm), lambda i: (i, 0, 0), memory_space=pltpu.SMEM),
                  row(d), row(LANES), row(p.shape[1])]
                 + [_resident(w.shape) for w in weights]
                 + [pl.BlockSpec(memory_space=pl.ANY)],
        out_specs=row(d),
        out_shape=jax.ShapeDtypeStruct((n, d), F32),
        scratch_shapes=[pltpu.VMEM((tm * (d // LANES), LANES), F32),
                        pltpu.VMEM((tm * (d // LANES), LANES), F32),
                        pltpu.SemaphoreType.DMA(())],
        compiler_params=_params(("arbitrary",)),
        name="moe_combine",
    )(pos, h1, meta, p, *weights, ys)


def _prep_w_in(w):
    scale = HEAD_DIM ** -0.5
    qa, ka, va = w[:, 0:512] * (scale * LOG2_E), w[:, 512:1024], w[:, 1024:1536]
    qb, kb, vb = w[:, 1536:1792] * (scale * LOG2_E), w[:, 1792:2048], w[:, 2048:2304]
    fb = w[:, 2304:2308]
    qc, kc, vc, gc = w[:, 2308:2564], w[:, 2564:2820] * scale, w[:, 2820:3076], w[:, 3076:3332]
    main = jnp.concatenate([qa, ka, qb, kb, qc, kc, vc, gc], axis=1).astype(BF16)
    f = jnp.pad(fb, ((0, 0), (0, LANES - fb.shape[1]))).astype(BF16)
    vt = jnp.concatenate([va, vb], axis=1).T.astype(BF16)
    return main, f, vt


def _pad_lanes(v):
    return jnp.pad(v, (0, LANES - v.shape[0]))[None, :]


class Tiles(NamedTuple):
    rows: int
    proj_cols: int
    attn: int
    attn_blocks: int
    scan: int
    ret_chunk: int
    ff_chunk: int
    moe_tile: int
    moe_ff_chunk: int
    moe_sub_chunk: int
    zero_rows: int


TILES = Tiles(rows=1024, proj_cols=512, attn=512, attn_blocks=2, scan=512, ret_chunk=256, ff_chunk=256,
              moe_tile=1024, moe_ff_chunk=1792, moe_sub_chunk=256, zero_rows=256)


def _mixer(h, b, s, norm_g, w_in, b_forget, lq1, lk1, lq2, lk2, subln, ret_gn, layer_idx, t):
    w_main, w_f, w_vt = _prep_w_in(w_in)
    proj, f, vt = norm_proj(h, norm_g[None, :], w_main, w_f, w_vt, tm=t.rows, seq=s,
                            col_chunk=t.proj_cols)
    proj = proj.reshape(b, s, -1)
    cum_col, cum_row = fox_prep(f.reshape(b, s, LANES), _pad_lanes(b_forget), blk=t.scan)
    lam_init = 0.8 - 0.6 * math.exp(-0.3 * layer_idx)
    lam = jnp.exp(jnp.sum(lq1 * lk1)) - jnp.exp(jnp.sum(lq2 * lk2)) + lam_init
    oa = diff_attention(proj, vt, lam.reshape(1), subln[None, :], tq=t.attn,
                        nblk=t.attn_blocks, out_scale=1.0 - lam_init)
    ob = fox_attention(proj, vt, cum_col, cum_row, tq=t.attn, nblk=t.attn_blocks)
    oc = retention(proj, ret_gn[None, :], chunk=t.ret_chunk)
    n = b * s
    return oa.reshape(n, -1), ob.reshape(n, -1), oc.reshape(n, -1)


def _route_plan(meta, counts, tile, n_slots, tm):
    n = meta.shape[0]
    idx = meta[:, 0:2].astype(jnp.int32)
    rank = meta[:, 2:4].astype(jnp.int32)
    cnt = counts[0, :N_EXPERTS].astype(jnp.int32)
    padded = ((cnt + tile - 1) // tile) * tile
    ends = jnp.cumsum(padded)
    starts = ends - padded
    pos = starts[idx] + rank
    n_tiles = ends[-1] // tile
    tiles = jnp.arange(n_slots // tile, dtype=jnp.int32)
    tile_expert = jnp.sum(tiles[:, None] >= (ends // tile)[None, :], axis=1)
    last = jnp.sum(n_tiles - 1 >= ends // tile)
    tile_expert = jnp.where(tiles < n_tiles, tile_expert, last).astype(jnp.int32)
    tile_expert = jnp.clip(tile_expert, 0, N_EXPERTS - 1)
    pos = jnp.clip(pos, 0, n_slots - 1)
    tile_rows = jnp.clip(cnt[tile_expert] - (tiles * tile - starts[tile_expert]), 0, tile)
    tile_rows = jnp.where(tiles < n_tiles, tile_rows, 0).astype(jnp.int32)
    pos = pos.reshape(n // tm, tm, 2).transpose(0, 2, 1)
    last_tile_start = jnp.concatenate([jnp.clip(ends - tile, 0, n_slots - tile),
                                       jnp.clip(ends[-1:], 0, n_slots)]).astype(jnp.int32)
    return pos, tile_expert, n_tiles.reshape(1).astype(jnp.int32), tile_rows, last_tile_start


def forward(x, p, norm_mix, w_in, b_forget, lambda_q1, lambda_k1, lambda_q2, lambda_k2,
            diff_subln, ret_gn, w_out, norm_ffn, dense_w_gate, dense_w_up, dense_w_down,
            router, moe_w_gate, moe_w_up, moe_w_down, ple_norm, ple_gate, ple_proj, final_norm,
            t=TILES):
    assert w_in.shape[0] == 2, "two layers: dense SwiGLU, then routed experts"
    b, s, d = x.shape
    n = b * s
    h = x.reshape(n, d)
    pf = p.reshape(p.shape[0], n, p.shape[-1])
    bf = lambda a: a.astype(BF16)
    row = lambda a: a[None, :]

    oa, ob, oc = _mixer(h, b, s, norm_mix[0], w_in[0], b_forget[0], lambda_q1[0], lambda_k1[0],
                        lambda_q2[0], lambda_k2[0], diff_subln[0], ret_gn[0], 0, t)
    h = dense_layer(h, oa, ob, oc, pf[0], bf(w_out[0]), row(norm_ffn[0]), bf(dense_w_gate[0]),
                    bf(dense_w_up[0]), bf(dense_w_down[0]), row(ple_norm[0]), bf(ple_gate[0]),
                    bf(ple_proj[0]), tm=t.rows, ff_chunk=t.ff_chunk)

    oa, ob, oc = _mixer(h, b, s, norm_mix[1], w_in[1], b_forget[1], lambda_q1[1], lambda_k1[1],
                        lambda_q2[1], lambda_k2[1], diff_subln[1], ret_gn[1], 1, t)
    wr = jnp.pad(router[0], ((0, 0), (0, LANES - N_EXPERTS)))
    h1, xn, meta, counts = route_tokens(h, oa, ob, oc, bf(w_out[1]), row(norm_ffn[1]), wr,
                                         tm=t.rows)
    n_slots = 2 * n + N_EXPERTS * t.moe_tile
    pos, tile_expert, n_tiles, tile_rows, last_tile_start = _route_plan(
        meta, counts, t.moe_tile, n_slots, t.rows)
    xs = dispatch(xn, pos, last_tile_start, n_slots, tm=t.rows, tile=t.moe_tile,
                  zrows=t.zero_rows, nl=d // LANES)
    ys = experts(xs, tile_expert, n_tiles, tile_rows, bf(moe_w_gate[0]), bf(moe_w_up[0]),
                 bf(moe_w_down[0]), tile=t.moe_tile, ff_chunk=t.moe_ff_chunk,
                 sub_chunk=t.moe_sub_chunk)
    out = combine(pos, h1, meta, pf[1], row(ple_norm[1]), bf(ple_gate[1]), bf(ple_proj[1]),
                  row(final_norm), ys, tm=t.rows)
    return out.reshape(b, s, d)


def kernel(x, p, norm_mix, w_in, b_forget, lambda_q1, lambda_k1, lambda_q2, lambda_k2,
           diff_subln, ret_gn, w_out, norm_ffn, dense_w_gate, dense_w_up, dense_w_down,
           router, moe_w_gate, moe_w_up, moe_w_down, ple_norm, ple_gate, ple_proj, final_norm):
    return forward(x, p, norm_mix, w_in, b_forget, lambda_q1, lambda_k1, lambda_q2, lambda_k2,
                   diff_subln, ret_gn, w_out, norm_ffn, dense_w_gate, dense_w_up, dense_w_down,
                   router, moe_w_gate, moe_w_up, moe_w_down, ple_norm, ple_gate, ple_proj,
                   final_norm)
```

```python
import functools
import math
from typing import NamedTuple

import numpy as np
import jax
import jax.numpy as jnp
from jax import lax
from jax.experimental import pallas as pl
from jax.experimental.pallas import tpu as pltpu

F32 = jnp.float32
BF16 = jnp.bfloat16

LANES = 128
SUBLANES = 8
VMEM_LIMIT_BYTES = 56 * 1024 * 1024

ROW_UNROLL = 8

HEAD_DIM = 64
DIFF_HEADS = 4
FOX_HEADS = 4
RET_HEADS = 4
RET_THETA = 10000.0
N_EXPERTS = 8
EPS = 1e-6
NEG_INF = -1e30
LOG2_E = math.log2(math.e)

QA, KA = 0, 4
QB, KB = 8, 10
QC, KC, VC, GC = 12, 14, 16, 18
VT_A, VT_B = 0, 4


def _params(semantics):
    return pltpu.CompilerParams(dimension_semantics=semantics,
                                vmem_limit_bytes=VMEM_LIMIT_BYTES)


def _resident(shape):
    nd = len(shape)
    return pl.BlockSpec(shape, lambda *_: (0,) * nd, pipeline_mode=pl.Buffered(1))


def _rms(x, g):
    return x * lax.rsqrt(jnp.mean(x * x, axis=-1, keepdims=True) + EPS) * g


def _sigmoid(x):
    return 1.0 / (1.0 + jnp.exp(-x))


def _dot(a, b):
    return jnp.dot(a, b, preferred_element_type=F32)


def _dot_nt(a, b):
    return lax.dot_general(a, b, (((1,), (1,)), ((), ())), preferred_element_type=F32)


def _split_bf16(x, n):
    pieces = []
    for _ in range(n):
        piece = x.astype(BF16)
        pieces.append(piece)
        x = x - piece.astype(F32)
    return pieces


def _lane_masks(shape):
    lane = lax.broadcasted_iota(jnp.int32, shape, len(shape) - 1)
    return lane < HEAD_DIM


def _norm_proj_kernel(h_ref, g_ref, w_ref, wf_ref, wvt_ref, out_ref, f_ref, vt_ref, *,
                      col_chunk):
    xn = _rms(h_ref[...], g_ref[...]).astype(BF16)
    ncol = w_ref.shape[1]
    for c in range(0, ncol, col_chunk):
        w = min(col_chunk, ncol - c)
        out_ref[:, c:c + w] = _dot(xn, w_ref[:, c:c + w]).astype(out_ref.dtype)
    f_ref[...] = _dot(xn, wf_ref[...])
    vt_ref[...] = _dot_nt(wvt_ref[...], xn).astype(vt_ref.dtype)


def norm_proj(h, g, w_main, w_f, w_vt, *, tm, seq, col_chunk):
    n, d = h.shape
    ncol = w_main.shape[1]
    nvt = w_vt.shape[0]
    per_seq = seq // tm
    return pl.pallas_call(
        functools.partial(_norm_proj_kernel, col_chunk=col_chunk),
        grid=(n // tm,),
        in_specs=[pl.BlockSpec((tm, d), lambda i: (i, 0)),
                  _resident((1, d)), _resident((d, ncol)), _resident((d, LANES)),
                  _resident((nvt, d))],
        out_specs=[pl.BlockSpec((tm, ncol), lambda i: (i, 0)),
                   pl.BlockSpec((tm, LANES), lambda i: (i, 0)),
                   pl.BlockSpec((None, nvt, tm), lambda i: (i // per_seq, 0, i % per_seq))],
        out_shape=[jax.ShapeDtypeStruct((n, ncol), BF16),
                   jax.ShapeDtypeStruct((n, LANES), F32),
                   jax.ShapeDtypeStruct((n // seq, nvt, seq), BF16)],
        compiler_params=_params(("arbitrary",)),
        name="norm_proj",
    )(h, g, w_main, w_f, w_vt)


def _fox_prep_kernel(f_ref, b_ref, col_ref, row_ref, carry_ref):
    @pl.when(pl.program_id(1) == 0)
    def _():
        carry_ref[...] = jnp.zeros_like(carry_ref)

    x = f_ref[...] + b_ref[...]
    log_f = (jnp.minimum(x, 0.0) - jnp.log1p(jnp.exp(-jnp.abs(x)))) * LOG2_E
    blk = x.shape[0]
    r = lax.broadcasted_iota(jnp.int32, (blk, blk), 0)
    c = lax.broadcasted_iota(jnp.int32, (blk, blk), 1)
    tri = (c <= r).astype(BF16)
    cum = sum(_dot(tri, piece) for piece in _split_bf16(log_f, 3)) + carry_ref[...]
    col_ref[...] = cum
    row_ref[...] = cum.T[:SUBLANES, :]
    carry_ref[...] = cum[blk - 1:blk, :]


def fox_prep(f, b_pad, *, blk):
    bsz, s, _ = f.shape
    return pl.pallas_call(
        _fox_prep_kernel,
        grid=(bsz, s // blk),
        in_specs=[pl.BlockSpec((None, blk, LANES), lambda b, j: (b, j, 0)),
                  pl.BlockSpec((1, LANES), lambda b, j: (0, 0))],
        out_specs=[pl.BlockSpec((None, blk, LANES), lambda b, j: (b, j, 0)),
                   pl.BlockSpec((None, SUBLANES, blk), lambda b, j: (b, 0, j))],
        out_shape=[jax.ShapeDtypeStruct((bsz, s, LANES), F32),
                   jax.ShapeDtypeStruct((bsz, SUBLANES, s), F32)],
        scratch_shapes=[pltpu.VMEM((1, LANES), F32)],
        compiler_params=_params(("arbitrary", "arbitrary")),
        name="fox_prep",
    )(f, b_pad)


def _attn_kernel(*refs, mode, tq, nblk, out_scale):
    nmap = 2 * nblk
    if mode == "diff":
        lam_ref, q_ref, k_ref, vt_ref, gain_ref, o_ref = refs[:6]
    else:
        q_ref, k_ref, vt_ref, ccol_ref, crow_ref, o_ref = refs[:6]
        cs_scr = refs[6 + 5 * nmap]
    state = tuple(refs[6 + 3 * j:9 + 3 * j] for j in range(nmap))
    st_a = refs[6 + 3 * nmap:6 + 4 * nmap]
    st_b = refs[6 + 4 * nmap:6 + 5 * nmap]
    qi = pl.program_id(2)
    first_head = nmap * pl.program_id(1)
    block = lambda u: slice(u * LANES, (u + 1) * LANES)

    q_maps = []
    for u in range(nblk):
        q = q_ref[:, block(u)]
        lo = _lane_masks(q.shape)
        zero = jnp.zeros_like(q)
        q_maps += [jnp.where(lo, q, zero), jnp.where(lo, zero, q)]

    for m_ref, l_ref, acc_ref in state:
        m_ref[...] = jnp.full_like(m_ref, NEG_INF)
        l_ref[...] = jnp.zeros_like(l_ref)
        acc_ref[...] = jnp.zeros_like(acc_ref)

    if mode == "fox":
        @pl.when(qi == 0)
        def _():
            ccol = ccol_ref[...]
            lane = lax.broadcasted_iota(jnp.int32, ccol.shape, 1)
            for j in range(nmap):
                col = jnp.sum(jnp.where(lane == first_head + j, ccol, 0.0), axis=-1,
                              keepdims=True)
                cs_scr[j] = jnp.broadcast_to(col, ccol.shape)

        ct = tuple(crow_ref[pl.ds(first_head + j, 1), :] for j in range(nmap))

    def scores(c, bufs):
        rows = pl.ds(pl.multiple_of(c * tq, tq), tq)
        for j in range(nmap):
            bufs[j][...] = _dot_nt(k_ref[rows, block(j // 2)], q_maps[j])

    def consume(c, bufs, masked):
        start = pl.multiple_of(c * tq, tq)
        for i in range(nmap):
            vt = vt_ref[block(i // 2), pl.ds(start, tq)]
            m_ref, l_ref, acc_ref = state[i]
            st = bufs[i][...]
            if mode == "fox":
                cs = cs_scr[i, pl.ds(start, tq), :]
                st = st - jnp.tile(cs, (1, tq // LANES))
            if masked:
                key = lax.broadcasted_iota(jnp.int32, st.shape, 0)
                qry = lax.broadcasted_iota(jnp.int32, st.shape, 1)
                st = jnp.where(key <= qry, st, NEG_INF)
            m_old = m_ref[...]
            m_chunk = jnp.max(st, axis=0, keepdims=True)
            if mode == "fox":
                m_new = jnp.maximum(m_old, m_chunk + ct[i])
                p = jnp.exp2(st + (ct[i] - m_new))
            else:
                m_new = jnp.maximum(m_old, m_chunk)
                p = jnp.exp2(st - m_new)
            alpha = jnp.exp2(m_old - m_new)
            l_ref[...] = alpha * l_ref[...] + jnp.sum(p, axis=0, keepdims=True)
            acc_ref[...] = alpha * acc_ref[...] + _dot(vt, p.astype(BF16))
            m_ref[...] = m_new

    def body(j, carry):
        scores(2 * j + 1, st_b)
        consume(2 * j, st_a, False)
        scores(2 * j + 2, st_a)
        consume(2 * j + 1, st_b, False)
        return carry

    scores(0, st_a)
    lax.fori_loop(0, qi // 2, body, 0)

    @pl.when(qi % 2 == 0)
    def _():
        consume(qi, st_a, True)

    @pl.when(qi % 2 == 1)
    def _():
        scores(qi, st_b)
        consume(qi - 1, st_a, False)
        consume(qi, st_b, True)

    for u in range(nblk):
        o1, o2 = (acc_ref[...] * (1.0 / l_ref[...])
                  for _, l_ref, acc_ref in state[2 * u:2 * u + 2])
        if mode == "diff":
            ot = o1 - lam_ref[0] * o2
            ot = ot * lax.rsqrt(jnp.mean(ot * ot, axis=0, keepdims=True) + EPS)
            o = ot.T * (gain_ref[...] * out_scale)
        else:
            chan = lax.broadcasted_iota(jnp.int32, o1.shape, 0)
            o = jnp.where(chan < HEAD_DIM, o1, o2).T
        o_ref[:, block(u)] = o.astype(o_ref.dtype)


def _attn_scratch(tq, nblk):
    nmap = 2 * nblk
    per_map = [pltpu.VMEM((1, tq), F32), pltpu.VMEM((1, tq), F32), pltpu.VMEM((LANES, tq), F32)]
    scores = [pltpu.VMEM((tq, tq), F32)] * (2 * nmap)
    return per_map * nmap + scores


def diff_attention(proj, vt, lam, gain, *, tq, nblk, out_scale):
    bsz, s, _ = proj.shape
    w = nblk * LANES
    kern = functools.partial(_attn_kernel, mode="diff", tq=tq, nblk=nblk, out_scale=out_scale)
    return pl.pallas_call(
        kern,
        grid=(bsz, DIFF_HEADS // nblk, s // tq),
        in_specs=[pl.BlockSpec(memory_space=pltpu.SMEM),
                  pl.BlockSpec((None, tq, w), lambda b, h, i: (b, i, QA // nblk + h)),
                  pl.BlockSpec((None, s, w), lambda b, h, i: (b, 0, KA // nblk + h)),
                  pl.BlockSpec((None, w, s), lambda b, h, i: (b, VT_A // nblk + h, 0)),
                  pl.BlockSpec((1, LANES), lambda b, h, i: (0, 0))],
        out_specs=pl.BlockSpec((None, tq, w), lambda b, h, i: (b, i, h)),
        out_shape=jax.ShapeDtypeStruct((bsz, s, DIFF_HEADS * LANES), BF16),
        scratch_shapes=_attn_scratch(tq, nblk),
        compiler_params=_params(("arbitrary", "arbitrary", "arbitrary")),
        name="diff_attention",
    )(lam, proj, proj, vt, gain)


def fox_attention(proj, vt, cum_col, cum_row, *, tq, nblk):
    bsz, s, _ = proj.shape
    npair = FOX_HEADS // 2
    w = nblk * LANES
    kern = functools.partial(_attn_kernel, mode="fox", tq=tq, nblk=nblk, out_scale=None)
    return pl.pallas_call(
        kern,
        grid=(bsz, npair // nblk, s // tq),
        in_specs=[pl.BlockSpec((None, tq, w), lambda b, h, i: (b, i, QB // nblk + h)),
                  pl.BlockSpec((None, s, w), lambda b, h, i: (b, 0, KB // nblk + h)),
                  pl.BlockSpec((None, w, s), lambda b, h, i: (b, VT_B // nblk + h, 0)),
                  pl.BlockSpec((None, s, LANES), lambda b, h, i: (b, 0, 0)),
                  pl.BlockSpec((None, SUBLANES, tq), lambda b, h, i: (b, 0, i))],
        out_specs=pl.BlockSpec((None, tq, w), lambda b, h, i: (b, i, h)),
        out_shape=jax.ShapeDtypeStruct((bsz, s, npair * LANES), BF16),
        scratch_shapes=_attn_scratch(tq, nblk) + [pltpu.VMEM((2 * nblk, s, LANES), F32)],
        compiler_params=_params(("arbitrary", "arbitrary", "arbitrary")),
        name="fox_attention",
    )(proj, proj, vt, cum_col, cum_row)


def _retention_tables(s, chunk):
    half = HEAD_DIM // 2
    inv = RET_THETA ** (-np.arange(half, dtype=np.float64) / half)
    ang = np.arange(s, dtype=np.float64)[:, None] * inv[None, :]
    cos = np.tile(np.cos(ang), (1, 4))
    sin = np.tile(np.concatenate([-np.sin(ang), np.sin(ang)], axis=1), (1, 2))
    log_g = np.log1p(-np.exp2(-5.0 - np.arange(RET_HEADS, dtype=np.float64)))
    j = np.arange(chunk, dtype=np.float64)
    diff = j[:, None] - j[None, :]
    inner = np.where(diff[None] >= 0, np.exp(np.where(diff[None] >= 0, diff[None], 0.0)
                                             * log_g[:, None, None]), 0.0)
    q_dec = np.exp((j + 1.0)[None, :] * log_g[:, None])
    k_dec = np.exp((chunk - 1.0 - j)[None, :] * log_g[:, None])
    c_dec = np.exp(chunk * log_g)
    npair = RET_HEADS // 2

    def lanes(t):
        t = t.reshape(npair, 2, chunk)
        return np.repeat(t.transpose(0, 2, 1), HEAD_DIM, axis=2)

    blk = np.kron(np.eye(2), np.ones((HEAD_DIM, HEAD_DIM)))
    c_mat = np.stack([np.kron(np.diag(c_dec[2 * p:2 * p + 2]), np.ones((HEAD_DIM, HEAD_DIM)))
                      for p in range(npair)])
    f = lambda a: jnp.asarray(a, F32)
    return dict(cos=f(cos), sin=f(sin), inner=f(inner.reshape(npair, 2, chunk, chunk)),
                q_dec=f(lanes(q_dec)), k_dec=f(lanes(k_dec)), c_mat=f(c_mat), blk=f(blk))


def _retention_kernel(q_ref, k_ref, v_ref, g_ref, cos_ref, sin_ref, inner_ref, qdec_ref,
                      kdec_ref, cmat_ref, blk_ref, gain_ref, o_ref, state_ref):
    @pl.when(pl.program_id(1) == 0)
    def _():
        state_ref[...] = jnp.zeros_like(state_ref)

    cos = cos_ref[...]
    sin = sin_ref[...]
    lane = lax.broadcasted_iota(jnp.int32, cos.shape, 1)
    first_half = (lane % HEAD_DIM) < (HEAD_DIM // 2)
    lo = lane < HEAD_DIM

    def rope(x):
        x = x.astype(F32)
        swapped = jnp.where(first_half, pltpu.roll(x, LANES - HEAD_DIM // 2, 1),
                            pltpu.roll(x, HEAD_DIM // 2, 1))
        return x * cos + swapped * sin

    for p in range(RET_HEADS // 2):
        cols = slice(p * LANES, (p + 1) * LANES)
        qr = rope(q_ref[:, cols])
        kr = rope(k_ref[:, cols])
        v = v_ref[:, cols]
        qb = qr.astype(BF16)
        kb = kr.astype(BF16)
        zero = jnp.zeros_like(qb)
        a1 = (_dot_nt(jnp.where(lo, qb, zero), kb) * inner_ref[p, 0]).astype(BF16)
        a2 = (_dot_nt(jnp.where(lo, zero, qb), kb) * inner_ref[p, 1]).astype(BF16)
        state = state_ref[p]
        o = jnp.where(lo, _dot(a1, v), _dot(a2, v))
        o = o + _dot((qr * qdec_ref[p]).astype(BF16), state.astype(BF16))
        kd = (kr * kdec_ref[p]).astype(BF16)
        state_ref[p] = state * cmat_ref[p] + _dot(kd.T, v) * blk_ref[...]

        inv_n = 1.0 / HEAD_DIM
        sum_lo = jnp.sum(jnp.where(lo, o, 0.0), axis=-1, keepdims=True)
        sum_hi = jnp.sum(jnp.where(lo, 0.0, o), axis=-1, keepdims=True)
        d = o - jnp.where(lo, sum_lo, sum_hi) * inv_n
        d2 = d * d
        var_lo = jnp.sum(jnp.where(lo, d2, 0.0), axis=-1, keepdims=True)
        var_hi = jnp.sum(jnp.where(lo, 0.0, d2), axis=-1, keepdims=True)
        y = d * lax.rsqrt(jnp.where(lo, var_lo, var_hi) * inv_n + EPS)
        g = g_ref[:, cols].astype(F32)
        o_ref[:, cols] = (g * _sigmoid(g) * (y * gain_ref[:, cols])).astype(o_ref.dtype)


def retention(proj, gain, *, chunk):
    bsz, s, _ = proj.shape
    npair = RET_HEADS // 2
    t = _retention_tables(s, chunk)

    width = npair * LANES

    def act(off):
        return pl.BlockSpec((None, chunk, width), lambda b, c: (b, c, off // npair))

    pos = pl.BlockSpec((chunk, LANES), lambda b, c: (c, 0))
    return pl.pallas_call(
        _retention_kernel,
        grid=(bsz, s // chunk),
        in_specs=[act(QC), act(KC), act(VC), act(GC), pos, pos,
                  _resident((npair, 2, chunk, chunk)), _resident((npair, chunk, LANES)),
                  _resident((npair, chunk, LANES)), _resident((npair, LANES, LANES)),
                  _resident((LANES, LANES)), _resident((1, width))],
        out_specs=pl.BlockSpec((None, chunk, width), lambda b, c: (b, c, 0)),
        out_shape=jax.ShapeDtypeStruct((bsz, s, width), BF16),
        scratch_shapes=[pltpu.VMEM((npair, LANES, LANES), F32)],
        compiler_params=_params(("arbitrary", "arbitrary")),
        name="retention",
    )(proj, proj, proj, proj, t["cos"], t["sin"], t["inner"], t["q_dec"], t["k_dec"],
      t["c_mat"], t["blk"], gain)


def _mix_residual(h_ref, oa_ref, ob_ref, oc_ref, wo_ref):
    wa = oa_ref.shape[1]
    wb = ob_ref.shape[1]
    return (h_ref[...] + _dot(oa_ref[...], wo_ref[0:wa, :])
            + _dot(ob_ref[...], wo_ref[wa:wa + wb, :])
            + _dot(oc_ref[...], wo_ref[wa + wb:, :]))


def _swiglu_accumulate(xn, wg_ref, wu_ref, wd_ref, acc_ref, chunk):
    def body(i, carry):
        c = pl.multiple_of(i * chunk, chunk)
        g = _dot(xn, wg_ref[:, pl.ds(c, chunk)])
        u = _dot(xn, wu_ref[:, pl.ds(c, chunk)])
        a = (g * _sigmoid(g) * u).astype(BF16)
        acc_ref[...] += _dot(a, wd_ref[pl.ds(c, chunk), :])
        return carry

    lax.fori_loop(0, wg_ref.shape[1] // chunk, body, 0)


def _ple(h, p_ref, pn_ref, pg_ref, pp_ref):
    gate = _sigmoid(_dot(_rms(h, pn_ref[...]).astype(BF16), pg_ref[...]))
    return h + gate * _dot(p_ref[...].astype(BF16), pp_ref[...])


def _dense_layer_kernel(h_ref, oa_ref, ob_ref, oc_ref, p_ref, wo_ref, gn_ref, wg_ref, wu_ref,
                        wd_ref, pn_ref, pg_ref, pp_ref, out_ref, acc_ref, *, ff_chunk):
    h1 = _mix_residual(h_ref, oa_ref, ob_ref, oc_ref, wo_ref)
    xn = _rms(h1, gn_ref[...]).astype(BF16)
    acc_ref[...] = h1
    _swiglu_accumulate(xn, wg_ref, wu_ref, wd_ref, acc_ref, ff_chunk)
    out_ref[...] = _ple(acc_ref[...], p_ref, pn_ref, pg_ref, pp_ref)


def dense_layer(h, oa, ob, oc, p, wo, gn, wg, wu, wd, pn, pg, pp, *, tm, ff_chunk):
    n, d = h.shape
    row = lambda w: pl.BlockSpec((tm, w), lambda i: (i, 0))
    weights = (wo, gn, wg, wu, wd, pn, pg, pp)
    return pl.pallas_call(
        functools.partial(_dense_layer_kernel, ff_chunk=ff_chunk),
        grid=(n // tm,),
        in_specs=[row(d), row(oa.shape[1]), row(ob.shape[1]), row(oc.shape[1]), row(p.shape[1])]
                 + [_resident(w.shape) for w in weights],
        out_specs=row(d),
        out_shape=jax.ShapeDtypeStruct((n, d), F32),
        scratch_shapes=[pltpu.VMEM((tm, d), F32)],
        compiler_params=_params(("arbitrary",)),
        name="dense_layer",
    )(h, oa, ob, oc, p, *weights)


def _store_slabs(ref, x):
    rows, d = x.shape
    nl = d // LANES
    for c in range(nl):
        ref[pl.ds(c, rows, stride=nl), :] = x[:, c * LANES:(c + 1) * LANES].astype(ref.dtype)


def _load_slabs(ref, nl, dtype=None):
    parts = [ref[pl.ds(c, ref.shape[0] // nl, stride=nl), :] for c in range(nl)]
    if dtype is not None:
        parts = [p.astype(dtype) for p in parts]
    return jnp.concatenate(parts, axis=1)


def _slab_at(ref, row, nl):
    return ref.at[pl.ds(pl.multiple_of(row * nl, nl), nl), :]


def _router_kernel(h_ref, oa_ref, ob_ref, oc_ref, wo_ref, gn_ref, wr_ref,
                   h1_ref, xn_ref, meta_ref, cnt_ref):
    @pl.when(pl.program_id(0) == 0)
    def _():
        cnt_ref[...] = jnp.zeros_like(cnt_ref)

    h1 = _mix_residual(h_ref, oa_ref, ob_ref, oc_ref, wo_ref)
    h1_ref[...] = h1
    xn = _rms(h1, gn_ref[...])
    _store_slabs(xn_ref, xn)

    x_hi, x_lo = _split_bf16(xn, 2)
    w_hi, w_lo = _split_bf16(wr_ref[...], 2)
    logits = _dot(x_hi, w_hi) + (_dot(x_hi, w_lo) + _dot(x_lo, w_hi))
    tm = logits.shape[0]
    lane = lax.broadcasted_iota(jnp.int32, logits.shape, 1)
    logits = jnp.where(lane < N_EXPERTS, logits, -jnp.inf)

    def top(vals):
        best = jnp.max(vals, axis=-1, keepdims=True)
        idx = jnp.min(jnp.where(vals == best, lane, LANES), axis=-1, keepdims=True)
        return best, idx

    v1, i1 = top(logits)
    v2, i2 = top(jnp.where(lane == i1, -jnp.inf, logits))
    e = jnp.exp(v2 - v1)
    g1 = 1.0 / (1.0 + e)
    g2 = e / (1.0 + e)

    oh1 = (lane == i1).astype(F32)
    oh2 = (lane == i2).astype(F32)
    both = oh1 + oh2
    r = lax.broadcasted_iota(jnp.int32, (tm, tm), 0)
    c = lax.broadcasted_iota(jnp.int32, (tm, tm), 1)
    before = _dot((c < r).astype(BF16), both.astype(BF16)) + cnt_ref[...]
    rank1 = jnp.sum(before * oh1, axis=-1, keepdims=True)
    rank2 = jnp.sum(before * oh2, axis=-1, keepdims=True)
    cnt_ref[...] += jnp.sum(both, axis=0, keepdims=True)

    meta = jnp.zeros(logits.shape, F32)
    for k, val in enumerate((i1.astype(F32), i2.astype(F32), rank1, rank2, g1, g2)):
        meta = jnp.where(lane == k, val, meta)
    meta_ref[...] = meta


def route_tokens(h, oa, ob, oc, wo, gn, wr, *, tm):
    n, d = h.shape
    row = lambda w: pl.BlockSpec((tm, w), lambda i: (i, 0))
    weights = (wo, gn, wr)
    return pl.pallas_call(
        _router_kernel,
        grid=(n // tm,),
        in_specs=[row(d), row(oa.shape[1]), row(ob.shape[1]), row(oc.shape[1])]
                 + [_resident(w.shape) for w in weights],
        out_specs=[row(d), pl.BlockSpec((tm * (d // LANES), LANES), lambda i: (i, 0)),
                   row(LANES), pl.BlockSpec((1, LANES), lambda i: (0, 0))],
        out_shape=[jax.ShapeDtypeStruct((n, d), F32),
                   jax.ShapeDtypeStruct((n * (d // LANES), LANES), F32),
                   jax.ShapeDtypeStruct((n, LANES), F32),
                   jax.ShapeDtypeStruct((1, LANES), F32)],
        compiler_params=_params(("arbitrary",)),
        name="router",
    )(h, oa, ob, oc, *weights)


def _dispatch_kernel(last_ref, pos_ref, x_ref, xs_ref, zero_scr, sem, zero_sem, *, tile, nl):
    tm = x_ref.shape[0] // nl
    zrows = zero_scr.shape[0] // nl

    @pl.when(pl.program_id(0) == 0)
    def _():
        zero_scr[...] = jnp.zeros_like(zero_scr)

        def zero_copy(e, q):
            start = pl.multiple_of((last_ref[e] + q * zrows) * nl, zrows * nl)
            return pltpu.make_async_copy(zero_scr, xs_ref.at[pl.ds(start, zrows * nl), :],
                                         zero_sem)

        for e in range(N_EXPERTS):
            for q in range(tile // zrows):
                zero_copy(e, q).start()
        for e in range(N_EXPERTS):
            for q in range(tile // zrows):
                zero_copy(e, q).wait()

        n_trailing = (xs_ref.shape[0] // nl - last_ref[N_EXPERTS]) // zrows

        def trailing(q, carry, wait):
            cp = zero_copy(N_EXPERTS, q)
            cp.wait() if wait else cp.start()
            return carry

        lax.fori_loop(0, n_trailing, functools.partial(trailing, wait=False), 0)
        lax.fori_loop(0, n_trailing, functools.partial(trailing, wait=True), 0)

    def row_copy(r, k):
        return pltpu.make_async_copy(_slab_at(x_ref, r, nl),
                                     _slab_at(xs_ref, pos_ref[0, k, r], nl), sem)

    _start_rows(tm, row_copy)
    for k in range(2):
        pltpu.make_async_copy(x_ref, xs_ref.at[pl.ds(0, tm * nl), :], sem).wait()


def _start_rows(n_rows, row_copy):
    def body(g, carry):
        base = pl.multiple_of(g * ROW_UNROLL, ROW_UNROLL)
        for j in range(ROW_UNROLL):
            row_copy(base + j, 0).start(priority=0)
            row_copy(base + j, 1).start(priority=1)
        return carry

    lax.fori_loop(0, n_rows // ROW_UNROLL, body, 0)


def dispatch(xn, pos, last_tile_start, n_slots, *, tm, tile, zrows, nl):
    n = xn.shape[0] // nl
    assert tile % zrows == 0
    grid_spec = pltpu.PrefetchScalarGridSpec(
        num_scalar_prefetch=1,
        grid=(n // tm,),
        in_specs=[pl.BlockSpec((1, 2, tm), lambda i, last: (i, 0, 0), memory_space=pltpu.SMEM),
                  pl.BlockSpec((tm * nl, LANES), lambda i, last: (i, 0))],
        out_specs=pl.BlockSpec(memory_space=pl.ANY),
        scratch_shapes=[pltpu.VMEM((zrows * nl, LANES), xn.dtype),
                        pltpu.SemaphoreType.DMA(()), pltpu.SemaphoreType.DMA(())],
    )
    return pl.pallas_call(
        functools.partial(_dispatch_kernel, tile=tile, nl=nl),
        grid_spec=grid_spec,
        out_shape=jax.ShapeDtypeStruct((n_slots * nl, LANES), xn.dtype),
        compiler_params=_params(("arbitrary",)),
        name="moe_dispatch",
    )(last_tile_start, pos, xn)


def _experts_kernel(te_ref, nt_ref, rows_ref, x_ref, wg_ref, wu_ref, wd_ref, o_ref, x_scr,
                    acc_ref, *, sub_chunk):
    del te_ref, nt_ref
    t = pl.program_id(0)
    c = pl.program_id(1)
    last = pl.num_programs(1) - 1
    rows = rows_ref[t]
    half = x_scr.shape[0] // 2

    @pl.when(c == 0)
    def _():
        acc_ref[...] = jnp.zeros_like(acc_ref)

    @pl.when(jnp.logical_and(rows > 0, c == 0))
    def _():
        x_scr[...] = _load_slabs(x_ref, x_scr.shape[1] // LANES, BF16)

    @pl.when(rows > half)
    def _():
        _swiglu_accumulate(x_scr[...], wg_ref, wu_ref, wd_ref, acc_ref, sub_chunk)

    @pl.when(jnp.logical_and(rows > 0, rows <= half))
    def _():
        _swiglu_accumulate(x_scr[:half, :], wg_ref, wu_ref, wd_ref,
                           acc_ref.at[pl.ds(0, half), :], sub_chunk)

    @pl.when(c == last)
    def _():
        _store_slabs(o_ref, acc_ref[...])


def experts(xs, tile_expert, n_tiles, tile_rows, wg, wu, wd, *, tile, ff_chunk, sub_chunk):
    d = wg.shape[1]
    nl = d // LANES
    n_slots = xs.shape[0] // nl
    d_ff = wg.shape[2]
    nc = d_ff // ff_chunk
    grid_tiles = n_slots // tile

    def tile_idx(t, c, te, nt, tr):
        return (jnp.maximum(jnp.minimum(t, nt[0] - 1), 0), 0)

    def chunk_idx(t, c, nt):
        return jnp.where(t < nt[0], c, nc - 1)

    grid_spec = pltpu.PrefetchScalarGridSpec(
        num_scalar_prefetch=3,
        grid=(grid_tiles, nc),
        in_specs=[pl.BlockSpec((tile * nl, LANES), tile_idx),
                  pl.BlockSpec((None, d, ff_chunk),
                               lambda t, c, te, nt, tr: (te[t], 0, chunk_idx(t, c, nt))),
                  pl.BlockSpec((None, d, ff_chunk),
                               lambda t, c, te, nt, tr: (te[t], 0, chunk_idx(t, c, nt))),
                  pl.BlockSpec((None, ff_chunk, d),
                               lambda t, c, te, nt, tr: (te[t], chunk_idx(t, c, nt), 0))],
        out_specs=pl.BlockSpec((tile * nl, LANES), lambda t, c, te, nt, tr: (t, 0)),
        scratch_shapes=[pltpu.VMEM((tile, d), BF16), pltpu.VMEM((tile, d), F32)],
    )
    return pl.pallas_call(
        functools.partial(_experts_kernel, sub_chunk=sub_chunk),
        grid_spec=grid_spec,
        out_shape=jax.ShapeDtypeStruct((n_slots * nl, LANES), F32),
        compiler_params=_params(("arbitrary", "arbitrary")),
        name="moe_experts",
    )(tile_expert, n_tiles, tile_rows, xs, wg, wu, wd)


def _combine_kernel(pos_ref, h_ref, meta_ref, p_ref, pn_ref, pg_ref, pp_ref, fn_ref, ys_ref,
                    out_ref, y1_scr, y2_scr, sem):
    tm = h_ref.shape[0]
    bufs = (y1_scr, y2_scr)

    nl = h_ref.shape[1] // LANES

    def row_copy(r, k):
        return pltpu.make_async_copy(_slab_at(ys_ref, pos_ref[0, k, r], nl),
                                     _slab_at(bufs[k], r, nl), sem)

    _start_rows(tm, row_copy)
    for buf in bufs:
        pltpu.make_async_copy(ys_ref.at[pl.ds(0, tm * nl), :], buf, sem).wait()

    meta = meta_ref[...]
    lane = lax.broadcasted_iota(jnp.int32, meta.shape, 1)
    g1 = jnp.sum(jnp.where(lane == 4, meta, 0.0), axis=-1, keepdims=True)
    g2 = jnp.sum(jnp.where(lane == 5, meta, 0.0), axis=-1, keepdims=True)
    h2 = h_ref[...] + g1 * _load_slabs(y1_scr, nl) + g2 * _load_slabs(y2_scr, nl)
    h3 = _ple(h2, p_ref, pn_ref, pg_ref, pp_ref)
    out_ref[...] = _rms(h3, fn_ref[...])


def combine(pos, h1, meta, p, pn, pg, pp, fn, ys, *, tm):
    n, d = h1.shape
    row = lambda w: pl.BlockSpec((tm, w), lambda i: (i, 0))
    weights = (pn, pg, pp, fn)
    return pl.pallas_call(
        _combine_kernel,
        grid=(n // tm,),
        in_specs=[pl.BlockSpec((1, 2, tm), lambda i: (i, 0, 0), memory_space=pltpu.SMEM),
                  row(d), row(LANES), row(p.shape[1])]
                 + [_resident(w.shape) for w in weights]
                 + [pl.BlockSpec(memory_space=pl.ANY)],
        out_specs=row(d),
        out_shape=jax.ShapeDtypeStruct((n, d), F32),
        scratch_shapes=[pltpu.VMEM((tm * (d // LANES), LANES), F32),
                        pltpu.VMEM((tm * (d // LANES), LANES), F32),
                        pltpu.SemaphoreType.DMA(())],
        compiler_params=_params(("arbitrary",)),
        name="moe_combine",
    )(pos, h1, meta, p, *weights, ys)


def _prep_w_in(w):
    scale = HEAD_DIM ** -0.5
    qa, ka, va = w[:, 0:512] * (scale * LOG2_E), w[:, 512:1024], w[:, 1024:1536]
    qb, kb, vb = w[:, 1536:1792] * (scale * LOG2_E), w[:, 1792:2048], w[:, 2048:2304]
    fb = w[:, 2304:2308]
    qc, kc, vc, gc = w[:, 2308:2564], w[:, 2564:2820] * scale, w[:, 2820:3076], w[:, 3076:3332]
    main = jnp.concatenate([qa, ka, qb, kb, qc, kc, vc, gc], axis=1).astype(BF16)
    f = jnp.pad(fb, ((0, 0), (0, LANES - fb.shape[1]))).astype(BF16)
    vt = jnp.concatenate([va, vb], axis=1).T.astype(BF16)
    return main, f, vt


def _pad_lanes(v):
    return jnp.pad(v, (0, LANES - v.shape[0]))[None, :]


class Tiles(NamedTuple):
    rows: int
    proj_cols: int
    attn: int
    diff_blocks: int
    fox_blocks: int
    scan: int
    ret_chunk: int
    ff_chunk: int
    moe_tile: int
    moe_ff_chunk: int
    moe_sub_chunk: int
    zero_rows: int


TILES = Tiles(rows=1024, proj_cols=512, attn=512, diff_blocks=4, fox_blocks=2, scan=512, ret_chunk=256, ff_chunk=256,
              moe_tile=1024, moe_ff_chunk=1792, moe_sub_chunk=256, zero_rows=256)


def _mixer(h, b, s, norm_g, w_in, b_forget, lq1, lk1, lq2, lk2, subln, ret_gn, layer_idx, t):
    w_main, w_f, w_vt = _prep_w_in(w_in)
    proj, f, vt = norm_proj(h, norm_g[None, :], w_main, w_f, w_vt, tm=t.rows, seq=s,
                            col_chunk=t.proj_cols)
    proj = proj.reshape(b, s, -1)
    cum_col, cum_row = fox_prep(f.reshape(b, s, LANES), _pad_lanes(b_forget), blk=t.scan)
    lam_init = 0.8 - 0.6 * math.exp(-0.3 * layer_idx)
    lam = jnp.exp(jnp.sum(lq1 * lk1)) - jnp.exp(jnp.sum(lq2 * lk2)) + lam_init
    oa = diff_attention(proj, vt, lam.reshape(1), subln[None, :], tq=t.attn,
                        nblk=t.diff_blocks, out_scale=1.0 - lam_init)
    ob = fox_attention(proj, vt, cum_col, cum_row, tq=t.attn, nblk=t.fox_blocks)
    oc = retention(proj, ret_gn[None, :], chunk=t.ret_chunk)
    n = b * s
    return oa.reshape(n, -1), ob.reshape(n, -1), oc.reshape(n, -1)


def _route_plan(meta, counts, tile, n_slots, tm):
    n = meta.shape[0]
    idx = meta[:, 0:2].astype(jnp.int32)
    rank = meta[:, 2:4].astype(jnp.int32)
    cnt = counts[0, :N_EXPERTS].astype(jnp.int32)
    padded = ((cnt + tile - 1) // tile) * tile
    ends = jnp.cumsum(padded)
    starts = ends - padded
    pos = starts[idx] + rank
    n_tiles = ends[-1] // tile
    tiles = jnp.arange(n_slots // tile, dtype=jnp.int32)
    tile_expert = jnp.sum(tiles[:, None] >= (ends // tile)[None, :], axis=1)
    last = jnp.sum(n_tiles - 1 >= ends // tile)
    tile_expert = jnp.where(tiles < n_tiles, tile_expert, last).astype(jnp.int32)
    tile_expert = jnp.clip(tile_expert, 0, N_EXPERTS - 1)
    pos = jnp.clip(pos, 0, n_slots - 1)
    tile_rows = jnp.clip(cnt[tile_expert] - (tiles * tile - starts[tile_expert]), 0, tile)
    tile_rows = jnp.where(tiles < n_tiles, tile_rows, 0).astype(jnp.int32)
    pos = pos.reshape(n // tm, tm, 2).transpose(0, 2, 1)
    last_tile_start = jnp.concatenate([jnp.clip(ends - tile, 0, n_slots - tile),
                                       jnp.clip(ends[-1:], 0, n_slots)]).astype(jnp.int32)
    return pos, tile_expert, n_tiles.reshape(1).astype(jnp.int32), tile_rows, last_tile_start


def forward(x, p, norm_mix, w_in, b_forget, lambda_q1, lambda_k1, lambda_q2, lambda_k2,
            diff_subln, ret_gn, w_out, norm_ffn, dense_w_gate, dense_w_up, dense_w_down,
            router, moe_w_gate, moe_w_up, moe_w_down, ple_norm, ple_gate, ple_proj, final_norm,
            t=TILES):
    assert w_in.shape[0] == 2, "two layers: dense SwiGLU, then routed experts"
    b, s, d = x.shape
    n = b * s
    h = x.reshape(n, d)
    pf = p.reshape(p.shape[0], n, p.shape[-1])
    bf = lambda a: a.astype(BF16)
    row = lambda a: a[None, :]

    oa, ob, oc = _mixer(h, b, s, norm_mix[0], w_in[0], b_forget[0], lambda_q1[0], lambda_k1[0],
                        lambda_q2[0], lambda_k2[0], diff_subln[0], ret_gn[0], 0, t)
    h = dense_layer(h, oa, ob, oc, pf[0], bf(w_out[0]), row(norm_ffn[0]), bf(dense_w_gate[0]),
                    bf(dense_w_up[0]), bf(dense_w_down[0]), row(ple_norm[0]), bf(ple_gate[0]),
                    bf(ple_proj[0]), tm=t.rows, ff_chunk=t.ff_chunk)

    oa, ob, oc = _mixer(h, b, s, norm_mix[1], w_in[1], b_forget[1], lambda_q1[1], lambda_k1[1],
                        lambda_q2[1], lambda_k2[1], diff_subln[1], ret_gn[1], 1, t)
    wr = jnp.pad(router[0], ((0, 0), (0, LANES - N_EXPERTS)))
    h1, xn, meta, counts = route_tokens(h, oa, ob, oc, bf(w_out[1]), row(norm_ffn[1]), wr,
                                         tm=t.rows)
    n_slots = 2 * n + N_EXPERTS * t.moe_tile
    pos, tile_expert, n_tiles, tile_rows, last_tile_start = _route_plan(
        meta, counts, t.moe_tile, n_slots, t.rows)
    xs = dispatch(xn, pos, last_tile_start, n_slots, tm=t.rows, tile=t.moe_tile,
                  zrows=t.zero_rows, nl=d // LANES)
    ys = experts(xs, tile_expert, n_tiles, tile_rows, bf(moe_w_gate[0]), bf(moe_w_up[0]),
                 bf(moe_w_down[0]), tile=t.moe_tile, ff_chunk=t.moe_ff_chunk,
                 sub_chunk=t.moe_sub_chunk)
    out = combine(pos, h1, meta, pf[1], row(ple_norm[1]), bf(ple_gate[1]), bf(ple_proj[1]),
                  row(final_norm), ys, tm=t.rows)
    return out.reshape(b, s, d)


def kernel(x, p, norm_mix, w_in, b_forget, lambda_q1, lambda_k1, lambda_q2, lambda_k2,
           diff_subln, ret_gn, w_out, norm_ffn, dense_w_gate, dense_w_up, dense_w_down,
           router, moe_w_gate, moe_w_up, moe_w_down, ple_norm, ple_gate, ple_proj, final_norm):
    return forward(x, p, norm_mix, w_in, b_forget, lambda_q1, lambda_k1, lambda_q2, lambda_k2,
                   diff_subln, ret_gn, w_out, norm_ffn, dense_w_gate, dense_w_up, dense_w_down,
                   router, moe_w_gate, moe_w_up, moe_w_down, ple_norm, ple_gate, ple_proj,
                   final_norm)
```

```python
import functools
import math
from typing import NamedTuple

import numpy as np
import jax
import jax.numpy as jnp
from jax import lax
from jax.experimental import pallas as pl
from jax.experimental.pallas import tpu as pltpu

F32 = jnp.float32
BF16 = jnp.bfloat16

LANES = 128
SUBLANES = 8
VMEM_LIMIT_BYTES = 56 * 1024 * 1024

ROW_UNROLL = 8

HEAD_DIM = 64
DIFF_HEADS = 4
FOX_HEADS = 4
RET_HEADS = 4
RET_THETA = 10000.0
N_EXPERTS = 8
EPS = 1e-6
NEG_INF = -1e30
LOG2_E = math.log2(math.e)

QA, KA = 0, 4
QB, KB = 8, 10
QC, KC, VC, GC = 12, 14, 16, 18
VT_A, VT_B = 0, 4


def _params(semantics):
    return pltpu.CompilerParams(dimension_semantics=semantics,
                                vmem_limit_bytes=VMEM_LIMIT_BYTES)


def _resident(shape):
    nd = len(shape)
    return pl.BlockSpec(shape, lambda *_: (0,) * nd, pipeline_mode=pl.Buffered(1))


def _rms(x, g):
    return x * lax.rsqrt(jnp.mean(x * x, axis=-1, keepdims=True) + EPS) * g


def _sigmoid(x):
    return 1.0 / (1.0 + jnp.exp(-x))


def _dot(a, b):
    return jnp.dot(a, b, preferred_element_type=F32)


def _dot_nt(a, b):
    return lax.dot_general(a, b, (((1,), (1,)), ((), ())), preferred_element_type=F32)


def _split_bf16(x, n):
    pieces = []
    for _ in range(n):
        piece = x.astype(BF16)
        pieces.append(piece)
        x = x - piece.astype(F32)
    return pieces


def _lane_masks(shape):
    lane = lax.broadcasted_iota(jnp.int32, shape, len(shape) - 1)
    return lane < HEAD_DIM


def _norm_proj_kernel(h_ref, g_ref, w_ref, wf_ref, wvt_ref, out_ref, f_ref, vt_ref, *,
                      col_chunk):
    xn = _rms(h_ref[...], g_ref[...]).astype(BF16)
    ncol = w_ref.shape[1]
    for c in range(0, ncol, col_chunk):
        w = min(col_chunk, ncol - c)
        out_ref[:, c:c + w] = _dot(xn, w_ref[:, c:c + w]).astype(out_ref.dtype)
    f_ref[...] = _dot(xn, wf_ref[...])
    vt_ref[...] = _dot_nt(wvt_ref[...], xn).astype(vt_ref.dtype)


def norm_proj(h, g, w_main, w_f, w_vt, *, tm, seq, col_chunk):
    n, d = h.shape
    ncol = w_main.shape[1]
    nvt = w_vt.shape[0]
    per_seq = seq // tm
    return pl.pallas_call(
        functools.partial(_norm_proj_kernel, col_chunk=col_chunk),
        grid=(n // tm,),
        in_specs=[pl.BlockSpec((tm, d), lambda i: (i, 0)),
                  _resident((1, d)), _resident((d, ncol)), _resident((d, LANES)),
                  _resident((nvt, d))],
        out_specs=[pl.BlockSpec((tm, ncol), lambda i: (i, 0)),
                   pl.BlockSpec((tm, LANES), lambda i: (i, 0)),
                   pl.BlockSpec((None, nvt, tm), lambda i: (i // per_seq, 0, i % per_seq))],
        out_shape=[jax.ShapeDtypeStruct((n, ncol), BF16),
                   jax.ShapeDtypeStruct((n, LANES), F32),
                   jax.ShapeDtypeStruct((n // seq, nvt, seq), BF16)],
        compiler_params=_params(("arbitrary",)),
        name="norm_proj",
    )(h, g, w_main, w_f, w_vt)


def _fox_prep_kernel(f_ref, b_ref, col_ref, row_ref, carry_ref):
    @pl.when(pl.program_id(1) == 0)
    def _():
        carry_ref[...] = jnp.zeros_like(carry_ref)

    x = f_ref[...] + b_ref[...]
    log_f = (jnp.minimum(x, 0.0) - jnp.log1p(jnp.exp(-jnp.abs(x)))) * LOG2_E
    blk = x.shape[0]
    r = lax.broadcasted_iota(jnp.int32, (blk, blk), 0)
    c = lax.broadcasted_iota(jnp.int32, (blk, blk), 1)
    tri = (c <= r).astype(BF16)
    cum = sum(_dot(tri, piece) for piece in _split_bf16(log_f, 3)) + carry_ref[...]
    col_ref[...] = cum
    row_ref[...] = cum.T[:SUBLANES, :]
    carry_ref[...] = cum[blk - 1:blk, :]


def fox_prep(f, b_pad, *, blk):
    bsz, s, _ = f.shape
    return pl.pallas_call(
        _fox_prep_kernel,
        grid=(bsz, s // blk),
        in_specs=[pl.BlockSpec((None, blk, LANES), lambda b, j: (b, j, 0)),
                  pl.BlockSpec((1, LANES), lambda b, j: (0, 0))],
        out_specs=[pl.BlockSpec((None, blk, LANES), lambda b, j: (b, j, 0)),
                   pl.BlockSpec((None, SUBLANES, blk), lambda b, j: (b, 0, j))],
        out_shape=[jax.ShapeDtypeStruct((bsz, s, LANES), F32),
                   jax.ShapeDtypeStruct((bsz, SUBLANES, s), F32)],
        scratch_shapes=[pltpu.VMEM((1, LANES), F32)],
        compiler_params=_params(("arbitrary", "arbitrary")),
        name="fox_prep",
    )(f, b_pad)


def _attn_kernel(*refs, mode, tq, nblk, out_scale):
    nmap = 2 * nblk
    if mode == "diff":
        lam_ref, q_ref, k_ref, vt_ref, gain_ref, o_ref = refs[:6]
    else:
        q_ref, k_ref, vt_ref, ccol_ref, crow_ref, o_ref = refs[:6]
        cs_scr = refs[6 + 5 * nmap]
    state = tuple(refs[6 + 3 * j:9 + 3 * j] for j in range(nmap))
    st_a = refs[6 + 3 * nmap:6 + 4 * nmap]
    st_b = refs[6 + 4 * nmap:6 + 5 * nmap]
    qi = pl.program_id(2)
    first_head = nmap * pl.program_id(1)
    block = lambda u: slice(u * LANES, (u + 1) * LANES)

    q_maps = []
    for u in range(nblk):
        q = q_ref[:, block(u)]
        lo = _lane_masks(q.shape)
        zero = jnp.zeros_like(q)
        q_maps += [jnp.where(lo, q, zero), jnp.where(lo, zero, q)]

    for m_ref, l_ref, acc_ref in state:
        m_ref[...] = jnp.full_like(m_ref, NEG_INF)
        l_ref[...] = jnp.zeros_like(l_ref)
        acc_ref[...] = jnp.zeros_like(acc_ref)

    if mode == "fox":
        @pl.when(qi == 0)
        def _():
            ccol = ccol_ref[...]
            lane = lax.broadcasted_iota(jnp.int32, ccol.shape, 1)
            for j in range(nmap):
                col = jnp.sum(jnp.where(lane == first_head + j, ccol, 0.0), axis=-1,
                              keepdims=True)
                cs_scr[j] = jnp.broadcast_to(col, ccol.shape)

        ct = tuple(crow_ref[pl.ds(first_head + j, 1), :] for j in range(nmap))

    def scores(c, bufs):
        rows = pl.ds(pl.multiple_of(c * tq, tq), tq)
        for j in range(nmap):
            bufs[j][...] = _dot_nt(k_ref[rows, block(j // 2)], q_maps[j])

    def consume(c, bufs, masked):
        start = pl.multiple_of(c * tq, tq)
        for i in range(nmap):
            vt = vt_ref[block(i // 2), pl.ds(start, tq)]
            m_ref, l_ref, acc_ref = state[i]
            st = bufs[i][...]
            if mode == "fox":
                cs = cs_scr[i, pl.ds(start, tq), :]
                st = st - jnp.tile(cs, (1, tq // LANES))
            if masked:
                key = lax.broadcasted_iota(jnp.int32, st.shape, 0)
                qry = lax.broadcasted_iota(jnp.int32, st.shape, 1)
                st = jnp.where(key <= qry, st, NEG_INF)
            m_old = m_ref[...]
            m_chunk = jnp.max(st, axis=0, keepdims=True)
            if mode == "fox":
                m_new = jnp.maximum(m_old, m_chunk + ct[i])
                p = jnp.exp2(st + (ct[i] - m_new))
            else:
                m_new = jnp.maximum(m_old, m_chunk)
                p = jnp.exp2(st - m_new)
            alpha = jnp.exp2(m_old - m_new)
            l_ref[...] = alpha * l_ref[...] + jnp.sum(p, axis=0, keepdims=True)
            acc_ref[...] = alpha * acc_ref[...] + _dot(vt, p.astype(BF16))
            m_ref[...] = m_new

    def body(j, carry):
        scores(2 * j + 1, st_b)
        consume(2 * j, st_a, False)
        scores(2 * j + 2, st_a)
        consume(2 * j + 1, st_b, False)
        return carry

    scores(0, st_a)
    lax.fori_loop(0, qi // 2, body, 0)

    @pl.when(qi % 2 == 0)
    def _():
        consume(qi, st_a, True)

    @pl.when(qi % 2 == 1)
    def _():
        scores(qi, st_b)
        consume(qi - 1, st_a, False)
        consume(qi, st_b, True)

    for u in range(nblk):
        o1, o2 = (acc_ref[...] * (1.0 / l_ref[...])
                  for _, l_ref, acc_ref in state[2 * u:2 * u + 2])
        if mode == "diff":
            ot = o1 - lam_ref[0] * o2
            ot = ot * lax.rsqrt(jnp.mean(ot * ot, axis=0, keepdims=True) + EPS)
            o = ot.T * (gain_ref[...] * out_scale)
        else:
            chan = lax.broadcasted_iota(jnp.int32, o1.shape, 0)
            o = jnp.where(chan < HEAD_DIM, o1, o2).T
        o_ref[:, block(u)] = o.astype(o_ref.dtype)


def _attn_scratch(tq, nblk):
    nmap = 2 * nblk
    per_map = [pltpu.VMEM((1, tq), F32), pltpu.VMEM((1, tq), F32), pltpu.VMEM((LANES, tq), F32)]
    scores = [pltpu.VMEM((tq, tq), F32)] * (2 * nmap)
    return per_map * nmap + scores


def diff_attention(proj, vt, lam, gain, *, tq, nblk, out_scale):
    bsz, s, _ = proj.shape
    w = nblk * LANES
    kern = functools.partial(_attn_kernel, mode="diff", tq=tq, nblk=nblk, out_scale=out_scale)
    return pl.pallas_call(
        kern,
        grid=(bsz, DIFF_HEADS // nblk, s // tq),
        in_specs=[pl.BlockSpec(memory_space=pltpu.SMEM),
                  pl.BlockSpec((None, tq, w), lambda b, h, i: (b, i, QA // nblk + h)),
                  pl.BlockSpec((None, s, w), lambda b, h, i: (b, 0, KA // nblk + h)),
                  pl.BlockSpec((None, w, s), lambda b, h, i: (b, VT_A // nblk + h, 0)),
                  pl.BlockSpec((1, LANES), lambda b, h, i: (0, 0))],
        out_specs=pl.BlockSpec((None, tq, w), lambda b, h, i: (b, i, h)),
        out_shape=jax.ShapeDtypeStruct((bsz, s, DIFF_HEADS * LANES), BF16),
        scratch_shapes=_attn_scratch(tq, nblk),
        compiler_params=_params(("arbitrary", "arbitrary", "arbitrary")),
        name="diff_attention",
    )(lam, proj, proj, vt, gain)


def fox_attention(proj, vt, cum_col, cum_row, *, tq, nblk):
    bsz, s, _ = proj.shape
    npair = FOX_HEADS // 2
    w = nblk * LANES
    kern = functools.partial(_attn_kernel, mode="fox", tq=tq, nblk=nblk, out_scale=None)
    return pl.pallas_call(
        kern,
        grid=(bsz, npair // nblk, s // tq),
        in_specs=[pl.BlockSpec((None, tq, w), lambda b, h, i: (b, i, QB // nblk + h)),
                  pl.BlockSpec((None, s, w), lambda b, h, i: (b, 0, KB // nblk + h)),
                  pl.BlockSpec((None, w, s), lambda b, h, i: (b, VT_B // nblk + h, 0)),
                  pl.BlockSpec((None, s, LANES), lambda b, h, i: (b, 0, 0)),
                  pl.BlockSpec((None, SUBLANES, tq), lambda b, h, i: (b, 0, i))],
        out_specs=pl.BlockSpec((None, tq, w), lambda b, h, i: (b, i, h)),
        out_shape=jax.ShapeDtypeStruct((bsz, s, npair * LANES), BF16),
        scratch_shapes=_attn_scratch(tq, nblk) + [pltpu.VMEM((2 * nblk, s, LANES), F32)],
        compiler_params=_params(("arbitrary", "arbitrary", "arbitrary")),
        name="fox_attention",
    )(proj, proj, vt, cum_col, cum_row)


def _retention_tables(s, chunk):
    half = HEAD_DIM // 2
    inv = RET_THETA ** (-np.arange(half, dtype=np.float64) / half)
    ang = np.arange(s, dtype=np.float64)[:, None] * inv[None, :]
    cos = np.tile(np.cos(ang), (1, 4))
    sin = np.tile(np.concatenate([-np.sin(ang), np.sin(ang)], axis=1), (1, 2))
    log_g = np.log1p(-np.exp2(-5.0 - np.arange(RET_HEADS, dtype=np.float64)))
    j = np.arange(chunk, dtype=np.float64)
    diff = j[:, None] - j[None, :]
    inner = np.where(diff[None] >= 0, np.exp(np.where(diff[None] >= 0, diff[None], 0.0)
                                             * log_g[:, None, None]), 0.0)
    q_dec = np.exp((j + 1.0)[None, :] * log_g[:, None])
    k_dec = np.exp((chunk - 1.0 - j)[None, :] * log_g[:, None])
    c_dec = np.exp(chunk * log_g)
    npair = RET_HEADS // 2

    def lanes(t):
        t = t.reshape(npair, 2, chunk)
        return np.repeat(t.transpose(0, 2, 1), HEAD_DIM, axis=2)

    blk = np.kron(np.eye(2), np.ones((HEAD_DIM, HEAD_DIM)))
    c_mat = np.stack([np.kron(np.diag(c_dec[2 * p:2 * p + 2]), np.ones((HEAD_DIM, HEAD_DIM)))
                      for p in range(npair)])
    f = lambda a: jnp.asarray(a, F32)
    return dict(cos=f(cos), sin=f(sin), inner=f(inner.reshape(npair, 2, chunk, chunk)),
                q_dec=f(lanes(q_dec)), k_dec=f(lanes(k_dec)), c_mat=f(c_mat), blk=f(blk))


def _retention_kernel(q_ref, k_ref, v_ref, g_ref, cos_ref, sin_ref, inner_ref, qdec_ref,
                      kdec_ref, cmat_ref, blk_ref, gain_ref, o_ref, state_ref):
    @pl.when(pl.program_id(1) == 0)
    def _():
        state_ref[...] = jnp.zeros_like(state_ref)

    cos = cos_ref[...]
    sin = sin_ref[...]
    lane = lax.broadcasted_iota(jnp.int32, cos.shape, 1)
    first_half = (lane % HEAD_DIM) < (HEAD_DIM // 2)
    lo = lane < HEAD_DIM

    def rope(x):
        x = x.astype(F32)
        swapped = jnp.where(first_half, pltpu.roll(x, LANES - HEAD_DIM // 2, 1),
                            pltpu.roll(x, HEAD_DIM // 2, 1))
        return x * cos + swapped * sin

    for bi, p in [(bi, p) for bi in range(q_ref.shape[0]) for p in range(RET_HEADS // 2)]:
        cols = slice(p * LANES, (p + 1) * LANES)
        qr = rope(q_ref[bi, :, cols])
        kr = rope(k_ref[bi, :, cols])
        v = v_ref[bi, :, cols]
        qb = qr.astype(BF16)
        kb = kr.astype(BF16)
        zero = jnp.zeros_like(qb)
        a1 = (_dot_nt(jnp.where(lo, qb, zero), kb) * inner_ref[p, 0]).astype(BF16)
        a2 = (_dot_nt(jnp.where(lo, zero, qb), kb) * inner_ref[p, 1]).astype(BF16)
        state = state_ref[bi, p]
        o = jnp.where(lo, _dot(a1, v), _dot(a2, v))
        o = o + _dot((qr * qdec_ref[p]).astype(BF16), state.astype(BF16))
        kd = (kr * kdec_ref[p]).astype(BF16)
        state_ref[bi, p] = state * cmat_ref[p] + _dot(kd.T, v) * blk_ref[...]

        inv_n = 1.0 / HEAD_DIM
        sum_lo = jnp.sum(jnp.where(lo, o, 0.0), axis=-1, keepdims=True)
        sum_hi = jnp.sum(jnp.where(lo, 0.0, o), axis=-1, keepdims=True)
        d = o - jnp.where(lo, sum_lo, sum_hi) * inv_n
        d2 = d * d
        var_lo = jnp.sum(jnp.where(lo, d2, 0.0), axis=-1, keepdims=True)
        var_hi = jnp.sum(jnp.where(lo, 0.0, d2), axis=-1, keepdims=True)
        y = d * lax.rsqrt(jnp.where(lo, var_lo, var_hi) * inv_n + EPS)
        g = g_ref[bi, :, cols].astype(F32)
        o_ref[bi, :, cols] = (g * _sigmoid(g) * (y * gain_ref[:, cols])).astype(o_ref.dtype)


def retention(proj, gain, *, chunk, nseq):
    bsz, s, _ = proj.shape
    npair = RET_HEADS // 2
    t = _retention_tables(s, chunk)

    width = npair * LANES

    def act(off):
        return pl.BlockSpec((nseq, chunk, width), lambda b, c: (b, c, off // npair))

    pos = pl.BlockSpec((chunk, LANES), lambda b, c: (c, 0))
    return pl.pallas_call(
        _retention_kernel,
        grid=(bsz // nseq, s // chunk),
        in_specs=[act(QC), act(KC), act(VC), act(GC), pos, pos,
                  _resident((npair, 2, chunk, chunk)), _resident((npair, chunk, LANES)),
                  _resident((npair, chunk, LANES)), _resident((npair, LANES, LANES)),
                  _resident((LANES, LANES)), _resident((1, width))],
        out_specs=pl.BlockSpec((nseq, chunk, width), lambda b, c: (b, c, 0)),
        out_shape=jax.ShapeDtypeStruct((bsz, s, width), BF16),
        scratch_shapes=[pltpu.VMEM((nseq, npair, LANES, LANES), F32)],
        compiler_params=_params(("arbitrary", "arbitrary")),
        name="retention",
    )(proj, proj, proj, proj, t["cos"], t["sin"], t["inner"], t["q_dec"], t["k_dec"],
      t["c_mat"], t["blk"], gain)


def _mix_residual(h_ref, oa_ref, ob_ref, oc_ref, wo_ref):
    wa = oa_ref.shape[1]
    wb = ob_ref.shape[1]
    return (h_ref[...] + _dot(oa_ref[...], wo_ref[0:wa, :])
            + _dot(ob_ref[...], wo_ref[wa:wa + wb, :])
            + _dot(oc_ref[...], wo_ref[wa + wb:, :]))


def _swiglu_accumulate(xn, wg_ref, wu_ref, wd_ref, acc_ref, chunk):
    def body(i, carry):
        c = pl.multiple_of(i * chunk, chunk)
        g = _dot(xn, wg_ref[:, pl.ds(c, chunk)])
        u = _dot(xn, wu_ref[:, pl.ds(c, chunk)])
        a = (g * _sigmoid(g) * u).astype(BF16)
        acc_ref[...] += _dot(a, wd_ref[pl.ds(c, chunk), :])
        return carry

    lax.fori_loop(0, wg_ref.shape[1] // chunk, body, 0)


def _ple(h, p_ref, pn_ref, pg_ref, pp_ref):
    gate = _sigmoid(_dot(_rms(h, pn_ref[...]).astype(BF16), pg_ref[...]))
    return h + gate * _dot(p_ref[...].astype(BF16), pp_ref[...])


def _dense_layer_kernel(h_ref, oa_ref, ob_ref, oc_ref, p_ref, wo_ref, gn_ref, wg_ref, wu_ref,
                        wd_ref, pn_ref, pg_ref, pp_ref, out_ref, acc_ref, *, ff_chunk):
    h1 = _mix_residual(h_ref, oa_ref, ob_ref, oc_ref, wo_ref)
    xn = _rms(h1, gn_ref[...]).astype(BF16)
    acc_ref[...] = h1
    _swiglu_accumulate(xn, wg_ref, wu_ref, wd_ref, acc_ref, ff_chunk)
    out_ref[...] = _ple(acc_ref[...], p_ref, pn_ref, pg_ref, pp_ref)


def dense_layer(h, oa, ob, oc, p, wo, gn, wg, wu, wd, pn, pg, pp, *, tm, ff_chunk):
    n, d = h.shape
    row = lambda w: pl.BlockSpec((tm, w), lambda i: (i, 0))
    weights = (wo, gn, wg, wu, wd, pn, pg, pp)
    return pl.pallas_call(
        functools.partial(_dense_layer_kernel, ff_chunk=ff_chunk),
        grid=(n // tm,),
        in_specs=[row(d), row(oa.shape[1]), row(ob.shape[1]), row(oc.shape[1]), row(p.shape[1])]
                 + [_resident(w.shape) for w in weights],
        out_specs=row(d),
        out_shape=jax.ShapeDtypeStruct((n, d), F32),
        scratch_shapes=[pltpu.VMEM((tm, d), F32)],
        compiler_params=_params(("arbitrary",)),
        name="dense_layer",
    )(h, oa, ob, oc, p, *weights)


def _store_slabs(ref, x):
    rows, d = x.shape
    nl = d // LANES
    for c in range(nl):
        ref[pl.ds(c, rows, stride=nl), :] = x[:, c * LANES:(c + 1) * LANES].astype(ref.dtype)


def _load_slabs(ref, nl, dtype=None):
    parts = [ref[pl.ds(c, ref.shape[0] // nl, stride=nl), :] for c in range(nl)]
    if dtype is not None:
        parts = [p.astype(dtype) for p in parts]
    return jnp.concatenate(parts, axis=1)


def _slab_at(ref, row, nl):
    return ref.at[pl.ds(pl.multiple_of(row * nl, nl), nl), :]


def _router_kernel(h_ref, oa_ref, ob_ref, oc_ref, wo_ref, gn_ref, wr_ref,
                   h1_ref, xn_ref, meta_ref, cnt_ref):
    @pl.when(pl.program_id(0) == 0)
    def _():
        cnt_ref[...] = jnp.zeros_like(cnt_ref)

    h1 = _mix_residual(h_ref, oa_ref, ob_ref, oc_ref, wo_ref)
    h1_ref[...] = h1
    xn = _rms(h1, gn_ref[...])
    _store_slabs(xn_ref, xn)

    x_hi, x_lo = _split_bf16(xn, 2)
    w_hi, w_lo = _split_bf16(wr_ref[...], 2)
    logits = _dot(x_hi, w_hi) + (_dot(x_hi, w_lo) + _dot(x_lo, w_hi))
    tm = logits.shape[0]
    lane = lax.broadcasted_iota(jnp.int32, logits.shape, 1)
    logits = jnp.where(lane < N_EXPERTS, logits, -jnp.inf)

    def top(vals):
        best = jnp.max(vals, axis=-1, keepdims=True)
        idx = jnp.min(jnp.where(vals == best, lane, LANES), axis=-1, keepdims=True)
        return best, idx

    v1, i1 = top(logits)
    v2, i2 = top(jnp.where(lane == i1, -jnp.inf, logits))
    e = jnp.exp(v2 - v1)
    g1 = 1.0 / (1.0 + e)
    g2 = e / (1.0 + e)

    oh1 = (lane == i1).astype(F32)
    oh2 = (lane == i2).astype(F32)
    both = oh1 + oh2
    r = lax.broadcasted_iota(jnp.int32, (tm, tm), 0)
    c = lax.broadcasted_iota(jnp.int32, (tm, tm), 1)
    before = _dot((c < r).astype(BF16), both.astype(BF16)) + cnt_ref[...]
    rank1 = jnp.sum(before * oh1, axis=-1, keepdims=True)
    rank2 = jnp.sum(before * oh2, axis=-1, keepdims=True)
    cnt_ref[...] += jnp.sum(both, axis=0, keepdims=True)

    meta = jnp.zeros(logits.shape, F32)
    for k, val in enumerate((i1.astype(F32), i2.astype(F32), rank1, rank2, g1, g2)):
        meta = jnp.where(lane == k, val, meta)
    meta_ref[...] = meta


def route_tokens(h, oa, ob, oc, wo, gn, wr, *, tm):
    n, d = h.shape
    row = lambda w: pl.BlockSpec((tm, w), lambda i: (i, 0))
    weights = (wo, gn, wr)
    return pl.pallas_call(
        _router_kernel,
        grid=(n // tm,),
        in_specs=[row(d), row(oa.shape[1]), row(ob.shape[1]), row(oc.shape[1])]
                 + [_resident(w.shape) for w in weights],
        out_specs=[row(d), pl.BlockSpec((tm * (d // LANES), LANES), lambda i: (i, 0)),
                   row(LANES), pl.BlockSpec((1, LANES), lambda i: (0, 0))],
        out_shape=[jax.ShapeDtypeStruct((n, d), F32),
                   jax.ShapeDtypeStruct((n * (d // LANES), LANES), F32),
                   jax.ShapeDtypeStruct((n, LANES), F32),
                   jax.ShapeDtypeStruct((1, LANES), F32)],
        compiler_params=_params(("arbitrary",)),
        name="router",
    )(h, oa, ob, oc, *weights)


def _dispatch_kernel(last_ref, pos_ref, x_ref, xs_ref, zero_scr, sem, zero_sem, *, tile, nl):
    tm = x_ref.shape[0] // nl
    zrows = zero_scr.shape[0] // nl

    @pl.when(pl.program_id(0) == 0)
    def _():
        zero_scr[...] = jnp.zeros_like(zero_scr)

        def zero_copy(e, q):
            start = pl.multiple_of((last_ref[e] + q * zrows) * nl, zrows * nl)
            return pltpu.make_async_copy(zero_scr, xs_ref.at[pl.ds(start, zrows * nl), :],
                                         zero_sem)

        for e in range(N_EXPERTS):
            for q in range(tile // zrows):
                zero_copy(e, q).start()
        for e in range(N_EXPERTS):
            for q in range(tile // zrows):
                zero_copy(e, q).wait()

        n_trailing = (xs_ref.shape[0] // nl - last_ref[N_EXPERTS]) // zrows

        def trailing(q, carry, wait):
            cp = zero_copy(N_EXPERTS, q)
            cp.wait() if wait else cp.start()
            return carry

        lax.fori_loop(0, n_trailing, functools.partial(trailing, wait=False), 0)
        lax.fori_loop(0, n_trailing, functools.partial(trailing, wait=True), 0)

    def row_copy(r, k):
        return pltpu.make_async_copy(_slab_at(x_ref, r, nl),
                                     _slab_at(xs_ref, pos_ref[0, k, r], nl), sem)

    _start_rows(tm, row_copy)
    for k in range(2):
        pltpu.make_async_copy(x_ref, xs_ref.at[pl.ds(0, tm * nl), :], sem).wait()


def _start_rows(n_rows, row_copy):
    def body(g, carry):
        base = pl.multiple_of(g * ROW_UNROLL, ROW_UNROLL)
        for j in range(ROW_UNROLL):
            row_copy(base + j, 0).start(priority=0)
            row_copy(base + j, 1).start(priority=1)
        return carry

    lax.fori_loop(0, n_rows // ROW_UNROLL, body, 0)


def dispatch(xn, pos, last_tile_start, n_slots, *, tm, tile, zrows, nl):
    n = xn.shape[0] // nl
    assert tile % zrows == 0
    grid_spec = pltpu.PrefetchScalarGridSpec(
        num_scalar_prefetch=1,
        grid=(n // tm,),
        in_specs=[pl.BlockSpec((1, 2, tm), lambda i, last: (i, 0, 0), memory_space=pltpu.SMEM),
                  pl.BlockSpec((tm * nl, LANES), lambda i, last: (i, 0))],
        out_specs=pl.BlockSpec(memory_space=pl.ANY),
        scratch_shapes=[pltpu.VMEM((zrows * nl, LANES), xn.dtype),
                        pltpu.SemaphoreType.DMA(()), pltpu.SemaphoreType.DMA(())],
    )
    return pl.pallas_call(
        functools.partial(_dispatch_kernel, tile=tile, nl=nl),
        grid_spec=grid_spec,
        out_shape=jax.ShapeDtypeStruct((n_slots * nl, LANES), xn.dtype),
        compiler_params=_params(("arbitrary",)),
        name="moe_dispatch",
    )(last_tile_start, pos, xn)


def _experts_kernel(te_ref, nt_ref, rows_ref, x_ref, wg_ref, wu_ref, wd_ref, o_ref, x_scr,
                    acc_ref, *, sub_chunk):
    del te_ref, nt_ref
    t = pl.program_id(0)
    c = pl.program_id(1)
    last = pl.num_programs(1) - 1
    rows = rows_ref[t]
    half = x_scr.shape[0] // 2

    @pl.when(c == 0)
    def _():
        acc_ref[...] = jnp.zeros_like(acc_ref)

    @pl.when(jnp.logical_and(rows > 0, c == 0))
    def _():
        x_scr[...] = _load_slabs(x_ref, x_scr.shape[1] // LANES, BF16)

    @pl.when(rows > half)
    def _():
        _swiglu_accumulate(x_scr[...], wg_ref, wu_ref, wd_ref, acc_ref, sub_chunk)

    @pl.when(jnp.logical_and(rows > 0, rows <= half))
    def _():
        _swiglu_accumulate(x_scr[:half, :], wg_ref, wu_ref, wd_ref,
                           acc_ref.at[pl.ds(0, half), :], sub_chunk)

    @pl.when(c == last)
    def _():
        _store_slabs(o_ref, acc_ref[...])


def experts(xs, tile_expert, n_tiles, tile_rows, wg, wu, wd, *, tile, ff_chunk, sub_chunk):
    d = wg.shape[1]
    nl = d // LANES
    n_slots = xs.shape[0] // nl
    d_ff = wg.shape[2]
    nc = d_ff // ff_chunk
    grid_tiles = n_slots // tile

    def tile_idx(t, c, te, nt, tr):
        return (jnp.maximum(jnp.minimum(t, nt[0] - 1), 0), 0)

    def chunk_idx(t, c, nt):
        return jnp.where(t < nt[0], c, nc - 1)

    grid_spec = pltpu.PrefetchScalarGridSpec(
        num_scalar_prefetch=3,
        grid=(grid_tiles, nc),
        in_specs=[pl.BlockSpec((tile * nl, LANES), tile_idx),
                  pl.BlockSpec((None, d, ff_chunk),
                               lambda t, c, te, nt, tr: (te[t], 0, chunk_idx(t, c, nt))),
                  pl.BlockSpec((None, d, ff_chunk),
                               lambda t, c, te, nt, tr: (te[t], 0, chunk_idx(t, c, nt))),
                  pl.BlockSpec((None, ff_chunk, d),
                               lambda t, c, te, nt, tr: (te[t], chunk_idx(t, c, nt), 0))],
        out_specs=pl.BlockSpec((tile * nl, LANES), lambda t, c, te, nt, tr: (t, 0)),
        scratch_shapes=[pltpu.VMEM((tile, d), BF16), pltpu.VMEM((tile, d), F32)],
    )
    return pl.pallas_call(
        functools.partial(_experts_kernel, sub_chunk=sub_chunk),
        grid_spec=grid_spec,
        out_shape=jax.ShapeDtypeStruct((n_slots * nl, LANES), F32),
        compiler_params=_params(("arbitrary", "arbitrary")),
        name="moe_experts",
    )(tile_expert, n_tiles, tile_rows, xs, wg, wu, wd)


def _combine_kernel(pos_ref, h_ref, meta_ref, p_ref, pn_ref, pg_ref, pp_ref, fn_ref, ys_ref,
                    out_ref, y1_scr, y2_scr, sem):
    tm = h_ref.shape[0]
    bufs = (y1_scr, y2_scr)

    nl = h_ref.shape[1] // LANES

    def row_copy(r, k):
        return pltpu.make_async_copy(_slab_at(ys_ref, pos_ref[0, k, r], nl),
                                     _slab_at(bufs[k], r, nl), sem)

    _start_rows(tm, row_copy)
    for buf in bufs:
        pltpu.make_async_copy(ys_ref.at[pl.ds(0, tm * nl), :], buf, sem).wait()

    meta = meta_ref[...]
    lane = lax.broadcasted_iota(jnp.int32, meta.shape, 1)
    g1 = jnp.sum(jnp.where(lane == 4, meta, 0.0), axis=-1, keepdims=True)
    g2 = jnp.sum(jnp.where(lane == 5, meta, 0.0), axis=-1, keepdims=True)
    h2 = h_ref[...] + g1 * _load_slabs(y1_scr, nl) + g2 * _load_slabs(y2_scr, nl)
    h3 = _ple(h2, p_ref, pn_ref, pg_ref, pp_ref)
    out_ref[...] = _rms(h3, fn_ref[...])


def combine(pos, h1, meta, p, pn, pg, pp, fn, ys, *, tm):
    n, d = h1.shape
    row = lambda w: pl.BlockSpec((tm, w), lambda i: (i, 0))
    weights = (pn, pg, pp, fn)
    return pl.pallas_call(
        _combine_kernel,
        grid=(n // tm,),
        in_specs=[pl.BlockSpec((1, 2, tm), lambda i: (i, 0, 0), memory_space=pltpu.SMEM),
                  row(d), row(LANES), row(p.shape[1])]
                 + [_resident(w.shape) for w in weights]
                 + [pl.BlockSpec(memory_space=pl.ANY)],
        out_specs=row(d),
        out_shape=jax.ShapeDtypeStruct((n, d), F32),
        scratch_shapes=[pltpu.VMEM((tm * (d // LANES), LANES), F32),
                        pltpu.VMEM((tm * (d // LANES), LANES), F32),
                        pltpu.SemaphoreType.DMA(())],
        compiler_params=_params(("arbitrary",)),
        name="moe_combine",
    )(pos, h1, meta, p, *weights, ys)


def _prep_w_in(w):
    scale = HEAD_DIM ** -0.5
    qa, ka, va = w[:, 0:512] * (scale * LOG2_E), w[:, 512:1024], w[:, 1024:1536]
    qb, kb, vb = w[:, 1536:1792] * (scale * LOG2_E), w[:, 1792:2048], w[:, 2048:2304]
    fb = w[:, 2304:2308]
    qc, kc, vc, gc = w[:, 2308:2564], w[:, 2564:2820] * scale, w[:, 2820:3076], w[:, 3076:3332]
    main = jnp.concatenate([qa, ka, qb, kb, qc, kc, vc, gc], axis=1).astype(BF16)
    f = jnp.pad(fb, ((0, 0), (0, LANES - fb.shape[1]))).astype(BF16)
    vt = jnp.concatenate([va, vb], axis=1).T.astype(BF16)
    return main, f, vt


def _pad_lanes(v):
    return jnp.pad(v, (0, LANES - v.shape[0]))[None, :]


class Tiles(NamedTuple):
    rows: int
    proj_cols: int
    attn: int
    diff_blocks: int
    fox_blocks: int
    scan: int
    ret_chunk: int
    ret_seqs: int
    ff_chunk: int
    moe_tile: int
    moe_ff_chunk: int
    moe_sub_chunk: int
    zero_rows: int


TILES = Tiles(rows=1024, proj_cols=512, attn=512, diff_blocks=4, fox_blocks=2, scan=512, ret_chunk=256, ret_seqs=2, ff_chunk=256,
              moe_tile=1024, moe_ff_chunk=1792, moe_sub_chunk=256, zero_rows=256)


def _mixer(h, b, s, norm_g, w_in, b_forget, lq1, lk1, lq2, lk2, subln, ret_gn, layer_idx, t):
    w_main, w_f, w_vt = _prep_w_in(w_in)
    proj, f, vt = norm_proj(h, norm_g[None, :], w_main, w_f, w_vt, tm=t.rows, seq=s,
                            col_chunk=t.proj_cols)
    proj = proj.reshape(b, s, -1)
    cum_col, cum_row = fox_prep(f.reshape(b, s, LANES), _pad_lanes(b_forget), blk=t.scan)
    lam_init = 0.8 - 0.6 * math.exp(-0.3 * layer_idx)
    lam = jnp.exp(jnp.sum(lq1 * lk1)) - jnp.exp(jnp.sum(lq2 * lk2)) + lam_init
    oa = diff_attention(proj, vt, lam.reshape(1), subln[None, :], tq=t.attn,
                        nblk=t.diff_blocks, out_scale=1.0 - lam_init)
    ob = fox_attention(proj, vt, cum_col, cum_row, tq=t.attn, nblk=t.fox_blocks)
    oc = retention(proj, ret_gn[None, :], chunk=t.ret_chunk, nseq=t.ret_seqs)
    n = b * s
    return oa.reshape(n, -1), ob.reshape(n, -1), oc.reshape(n, -1)


def _route_plan(meta, counts, tile, n_slots, tm):
    n = meta.shape[0]
    idx = meta[:, 0:2].astype(jnp.int32)
    rank = meta[:, 2:4].astype(jnp.int32)
    cnt = counts[0, :N_EXPERTS].astype(jnp.int32)
    padded = ((cnt + tile - 1) // tile) * tile
    ends = jnp.cumsum(padded)
    starts = ends - padded
    pos = starts[idx] + rank
    n_tiles = ends[-1] // tile
    tiles = jnp.arange(n_slots // tile, dtype=jnp.int32)
    tile_expert = jnp.sum(tiles[:, None] >= (ends // tile)[None, :], axis=1)
    last = jnp.sum(n_tiles - 1 >= ends // tile)
    tile_expert = jnp.where(tiles < n_tiles, tile_expert, last).astype(jnp.int32)
    tile_expert = jnp.clip(tile_expert, 0, N_EXPERTS - 1)
    pos = jnp.clip(pos, 0, n_slots - 1)
    tile_rows = jnp.clip(cnt[tile_expert] - (tiles * tile - starts[tile_expert]), 0, tile)
    tile_rows = jnp.where(tiles < n_tiles, tile_rows, 0).astype(jnp.int32)
    pos = pos.reshape(n // tm, tm, 2).transpose(0, 2, 1)
    last_tile_start = jnp.concatenate([jnp.clip(ends - tile, 0, n_slots - tile),
                                       jnp.clip(ends[-1:], 0, n_slots)]).astype(jnp.int32)
    return pos, tile_expert, n_tiles.reshape(1).astype(jnp.int32), tile_rows, last_tile_start


def forward(x, p, norm_mix, w_in, b_forget, lambda_q1, lambda_k1, lambda_q2, lambda_k2,
            diff_subln, ret_gn, w_out, norm_ffn, dense_w_gate, dense_w_up, dense_w_down,
            router, moe_w_gate, moe_w_up, moe_w_down, ple_norm, ple_gate, ple_proj, final_norm,
            t=TILES):
    assert w_in.shape[0] == 2, "two layers: dense SwiGLU, then routed experts"
    b, s, d = x.shape
    n = b * s
    h = x.reshape(n, d)
    pf = p.reshape(p.shape[0], n, p.shape[-1])
    bf = lambda a: a.astype(BF16)
    row = lambda a: a[None, :]

    oa, ob, oc = _mixer(h, b, s, norm_mix[0], w_in[0], b_forget[0], lambda_q1[0], lambda_k1[0],
                        lambda_q2[0], lambda_k2[0], diff_subln[0], ret_gn[0], 0, t)
    h = dense_layer(h, oa, ob, oc, pf[0], bf(w_out[0]), row(norm_ffn[0]), bf(dense_w_gate[0]),
                    bf(dense_w_up[0]), bf(dense_w_down[0]), row(ple_norm[0]), bf(ple_gate[0]),
                    bf(ple_proj[0]), tm=t.rows, ff_chunk=t.ff_chunk)

    oa, ob, oc = _mixer(h, b, s, norm_mix[1], w_in[1], b_forget[1], lambda_q1[1], lambda_k1[1],
                        lambda_q2[1], lambda_k2[1], diff_subln[1], ret_gn[1], 1, t)
    wr = jnp.pad(router[0], ((0, 0), (0, LANES - N_EXPERTS)))
    h1, xn, meta, counts = route_tokens(h, oa, ob, oc, bf(w_out[1]), row(norm_ffn[1]), wr,
                                         tm=t.rows)
    n_slots = 2 * n + N_EXPERTS * t.moe_tile
    pos, tile_expert, n_tiles, tile_rows, last_tile_start = _route_plan(
        meta, counts, t.moe_tile, n_slots, t.rows)
    xs = dispatch(xn, pos, last_tile_start, n_slots, tm=t.rows, tile=t.moe_tile,
                  zrows=t.zero_rows, nl=d // LANES)
    ys = experts(xs, tile_expert, n_tiles, tile_rows, bf(moe_w_gate[0]), bf(moe_w_up[0]),
                 bf(moe_w_down[0]), tile=t.moe_tile, ff_chunk=t.moe_ff_chunk,
                 sub_chunk=t.moe_sub_chunk)
    out = combine(pos, h1, meta, pf[1], row(ple_norm[1]), bf(ple_gate[1]), bf(ple_proj[1]),
                  row(final_norm), ys, tm=t.rows)
    return out.reshape(b, s, d)


def kernel(x, p, norm_mix, w_in, b_forget, lambda_q1, lambda_k1, lambda_q2, lambda_k2,
           diff_subln, ret_gn, w_out, norm_ffn, dense_w_gate, dense_w_up, dense_w_down,
           router, moe_w_gate, moe_w_up, moe_w_down, ple_norm, ple_gate, ple_proj, final_norm):
    return forward(x, p, norm_mix, w_in, b_forget, lambda_q1, lambda_k1, lambda_q2, lambda_k2,
                   diff_subln, ret_gn, w_out, norm_ffn, dense_w_gate, dense_w_up, dense_w_down,
                   router, moe_w_gate, moe_w_up, moe_w_down, ple_norm, ple_gate, ple_proj,
                   final_norm)
```
